```python
import jax
import jax.numpy as jnp
from jax import lax
import numpy as np

D_MODEL = 1024
BATCH = 4
SEQ = 4096
DEPTH = 2

EPS = 1e-6
N_MEM = 256
D_MIX = D_MODEL
POOL_WIDTH = D_MIX // 2
POOL_GROUPS = 4
POOL_GROUP_DIM = POOL_WIDTH // POOL_GROUPS
POOL_WINDOWS = (2, 4, 8, 16)
MLSTM_WIDTH = D_MIX - POOL_WIDTH
MLSTM_HEADS = 4
MLSTM_HEAD_DIM = MLSTM_WIDTH // MLSTM_HEADS
CONV_WIDTH = 4
CHUNK = 64
XATTN_HEADS = 4
XATTN_HEAD_DIM = D_MODEL // XATTN_HEADS
D_FF = -(-8 * D_MODEL // (3 * 256)) * 256

OFF_Q = POOL_WIDTH
OFF_K = OFF_Q + MLSTM_WIDTH
OFF_V = OFF_K + MLSTM_WIDTH
OFF_O = OFF_V + MLSTM_WIDTH
OFF_GATE = OFF_O + MLSTM_WIDTH
IN_COLS = OFF_GATE + 2 * MLSTM_HEADS

kernel_name = "hybrid_pool_mlstm_memxattn_block"


def rmsnorm(x, g):
    xf = x.astype(jnp.float32)
    y = xf * lax.rsqrt(jnp.mean(xf * xf, axis=-1, keepdims=True) + EPS)
    return (y * g.astype(jnp.float32)).astype(x.dtype)


def causal_depthwise_conv(x, w):
    C = x.shape[-1]
    return lax.conv_general_dilated(
        x, w[:, None, :].astype(x.dtype), window_strides=(1,),
        padding=[(CONV_WIDTH - 1, 0)], dimension_numbers=('NWC', 'WIO', 'NWC'),
        feature_group_count=C)


def pool_mixer(u, w_pool, scale):
    B, S, _ = u.shape
    uf = u.astype(jnp.float32).reshape(B, S, POOL_GROUPS, POOL_GROUP_DIM)
    cs = jnp.concatenate([jnp.zeros_like(uf[:, :1]), jnp.cumsum(uf, axis=1)], axis=1)
    t = jnp.arange(S)[:, None]
    win = jnp.asarray(POOL_WINDOWS, dtype=jnp.int32)[None, :]
    start = jnp.maximum(t + 1 - win, 0)
    count = (t + 1 - start).astype(jnp.float32)
    window_sum = cs[:, 1:] - cs[:, start, jnp.arange(POOL_GROUPS)[None, :], :]
    d = window_sum / count[None, :, :, None] - uf
    y = jnp.einsum('bsgc,gcd->bsgd', d, w_pool.astype(jnp.float32))
    return (y.reshape(B, S, POOL_WIDTH) * scale.astype(jnp.float32)).astype(u.dtype)


def _chunk_seq(t):
    B, H, S = t.shape[:3]
    t = t.reshape((B, H, S // CHUNK, CHUNK) + t.shape[3:])
    return jnp.moveaxis(t, 2, 0)


def mlstm_heads(q, k, v, i_pre, f_pre):
    B, S, H, dh = q.shape
    q = q * dh ** -0.5
    qc, kc, vc = (_chunk_seq(jnp.swapaxes(t, 1, 2)) for t in (q, k, v))
    ic = _chunk_seq(jnp.swapaxes(i_pre, 1, 2))
    fc = _chunk_seq(jnp.swapaxes(jax.nn.log_sigmoid(f_pre), 1, 2))
    tril = jnp.tril(jnp.ones((CHUNK, CHUNK), dtype=bool))

    def step(carry, inp):
        C, n, m = carry
        qb, kb, vb, ib, fb = inp
        b = jnp.cumsum(fb, axis=-1)
        D = jnp.where(tril, b[..., :, None] - b[..., None, :] + ib[..., None, :], -jnp.inf)
        inter = b + m[..., None]
        m_i = jnp.maximum(jnp.max(D, axis=-1), inter)
        A = jnp.einsum('bhid,bhjd->bhij', qb, kb) * jnp.exp(D - m_i[..., None])
        w_inter = jnp.exp(inter - m_i)
        num = jnp.einsum('bhij,bhjd->bhid', A, vb) + w_inter[..., None] * jnp.einsum('bhid,bhde->bhie', qb, C)
        den = jnp.sum(A, axis=-1) + w_inter * jnp.einsum('bhid,bhd->bhi', qb, n)
        h = num / jnp.maximum(jnp.abs(den), jnp.exp(-m_i))[..., None]
        bL = b[..., -1]
        g = bL[..., None] - b + ib
        m_new = jnp.maximum(bL + m, jnp.max(g, axis=-1))
        decay = jnp.exp(bL + m - m_new)
        wk = jnp.exp(g - m_new[..., None])
        C_new = decay[..., None, None] * C + jnp.einsum('bhj,bhjd,bhje->bhde', wk, kb, vb)
        n_new = decay[..., None] * n + jnp.einsum('bhj,bhjd->bhd', wk, kb)
        return (C_new, n_new, m_new), h

    init = (jnp.zeros((B, H, dh, dh), jnp.float32), jnp.zeros((B, H, dh), jnp.float32),
            jnp.zeros((B, H), jnp.float32))
    _, h = lax.scan(step, init, (qc, kc, vc, ic, fc))
    h = jnp.moveaxis(h, 0, 2).reshape(B, H, S, dh)
    return jnp.swapaxes(h, 1, 2)


def cross_attention(xn, mem_n, wq, wk, wv, wo):
    B, S, _ = xn.shape
    M = mem_n.shape[1]
    q = (xn @ wq).reshape(B, S, XATTN_HEADS, XATTN_HEAD_DIM).astype(jnp.float32)
    k = (mem_n @ wk).reshape(B, M, XATTN_HEADS, XATTN_HEAD_DIM).astype(jnp.float32)
    v = (mem_n @ wv).reshape(B, M, XATTN_HEADS, XATTN_HEAD_DIM).astype(jnp.float32)
    p = jax.nn.softmax(jnp.einsum('bshd,bmhd->bhsm', q, k) * XATTN_HEAD_DIM ** -0.5, axis=-1)
    o = jnp.einsum('bhsm,bmhd->bshd', p, v).reshape(B, S, D_MODEL).astype(xn.dtype)
    return o @ wo


def swiglu(xn, wg, wu, wd):
    return (jax.nn.silu(xn @ wg) * (xn @ wu)) @ wd


def setup_inputs(seed: int = 0) -> dict:
    key = jax.random.key(seed)
    ks = jax.random.split(key, 24)

    def nrm(k, shape, scale):
        return jax.random.normal(k, shape, jnp.float32) * scale

    def gain(k, shape):
        return 1.0 + 0.02 * jax.random.normal(k, shape, jnp.float32)

    H = MLSTM_HEADS
    b_i = nrm(ks[3], (DEPTH, H), 0.1)
    b_f = jnp.linspace(3.0, 6.0, H, dtype=jnp.float32)[None, :] + nrm(ks[4], (DEPTH, H), 0.1)
    return {
        'x': nrm(ks[0], (BATCH, SEQ, D_MODEL), 1.0),
        'mem': nrm(ks[1], (BATCH, N_MEM, D_MODEL), 1.0),
        'w_in': nrm(ks[2], (DEPTH, D_MODEL, IN_COLS), D_MODEL ** -0.5),
        'b_gate': jnp.concatenate([b_i, b_f], axis=-1),
        'conv_qk': nrm(ks[5], (DEPTH, CONV_WIDTH, 2 * MLSTM_WIDTH), CONV_WIDTH ** -0.5),
        'w_pool': nrm(ks[6], (DEPTH, POOL_GROUPS, POOL_GROUP_DIM, POOL_GROUP_DIM), POOL_GROUP_DIM ** -0.5),
        'pool_scale': gain(ks[7], (DEPTH, POOL_WIDTH)),
        'mlstm_norm_g': gain(ks[8], (DEPTH, MLSTM_WIDTH)),
        'w_out': nrm(ks[9], (DEPTH, D_MIX, D_MODEL), D_MIX ** -0.5),
        'g_mix_pre': gain(ks[10], (DEPTH, D_MODEL)),
        'g_mix_post': gain(ks[11], (DEPTH, D_MODEL)),
        'g_mem': gain(ks[12], (DEPTH, D_MODEL)),
        'g_xattn_pre': gain(ks[13], (DEPTH, D_MODEL)),
        'g_xattn_post': gain(ks[14], (DEPTH, D_MODEL)),
        'wq_x': nrm(ks[15], (DEPTH, D_MODEL, D_MODEL), D_MODEL ** -0.5),
        'wk_x': nrm(ks[16], (DEPTH, D_MODEL, D_MODEL), D_MODEL ** -0.5),
        'wv_x': nrm(ks[17], (DEPTH, D_MODEL, D_MODEL), D_MODEL ** -0.5),
        'wo_x': nrm(ks[18], (DEPTH, D_MODEL, D_MODEL), D_MODEL ** -0.5),
        'g_ffn_pre': gain(ks[19], (DEPTH, D_MODEL)),
        'g_ffn_post': gain(ks[20], (DEPTH, D_MODEL)),
        'w_gate': nrm(ks[21], (DEPTH, D_MODEL, D_FF), D_MODEL ** -0.5),
        'w_up': nrm(ks[22], (DEPTH, D_MODEL, D_FF), D_MODEL ** -0.5),
        'w_down': nrm(ks[23], (DEPTH, D_FF, D_MODEL), D_FF ** -0.5),
    }


def reference(x, mem, w_in, b_gate, conv_qk, w_pool, pool_scale, mlstm_norm_g, w_out,
              g_mix_pre, g_mix_post, g_mem, g_xattn_pre, g_xattn_post,
              wq_x, wk_x, wv_x, wo_x, g_ffn_pre, g_ffn_post, w_gate, w_up, w_down):
    B, S, _ = x.shape
    H, dh = MLSTM_HEADS, MLSTM_HEAD_DIM
    for l in range(DEPTH):
        h = rmsnorm(x, g_mix_pre[l])
        proj = h @ w_in[l]
        pool_out = pool_mixer(proj[..., :OFF_Q], w_pool[l], pool_scale[l])
        qk = jax.nn.silu(causal_depthwise_conv(proj[..., OFF_Q:OFF_V], conv_qk[l])).astype(jnp.float32)
        q = qk[..., :MLSTM_WIDTH].reshape(B, S, H, dh)
        k = qk[..., MLSTM_WIDTH:].reshape(B, S, H, dh)
        v = proj[..., OFF_V:OFF_O].astype(jnp.float32).reshape(B, S, H, dh)
        o_gate = jax.nn.sigmoid(proj[..., OFF_O:OFF_GATE].astype(jnp.float32))
        gates = proj[..., OFF_GATE:].astype(jnp.float32) + b_gate[l].astype(jnp.float32)
        hm = mlstm_heads(q, k, v, gates[..., :H], gates[..., H:])
        hm = hm * lax.rsqrt(jnp.mean(hm * hm, axis=-1, keepdims=True) + EPS)
        m_out = (hm.reshape(B, S, MLSTM_WIDTH) * mlstm_norm_g[l].astype(jnp.float32) * o_gate).astype(x.dtype)
        mix = jnp.concatenate([pool_out, m_out], axis=-1) @ w_out[l]
        x = x + rmsnorm(mix, g_mix_post[l])
        h = rmsnorm(x, g_xattn_pre[l])
        mem_n = rmsnorm(mem, g_mem[l])
        x = x + rmsnorm(cross_attention(h, mem_n, wq_x[l], wk_x[l], wv_x[l], wo_x[l]), g_xattn_post[l])
        h = rmsnorm(x, g_ffn_pre[l])
        x = x + rmsnorm(swiglu(h, w_gate[l], w_up[l], w_down[l]), g_ffn_post[l])
    return x
```

```python
import functools

import jax
import jax.numpy as jnp
from jax import lax
from jax.experimental import pallas as pl
from jax.experimental.pallas import tpu as pltpu

F32 = jnp.float32
BF16 = jnp.bfloat16

EPS = 1e-6
D_MODEL = 1024
N_MEM = 256
POOL_WIDTH = 512
POOL_GROUPS = 4
POOL_GROUP_DIM = 128
POOL_WINDOWS = (2, 4, 8, 16)
MLSTM_WIDTH = 512
MLSTM_HEADS = 4
MLSTM_HEAD_DIM = 128
CONV_WIDTH = 4
XATTN_HEADS = 4
XATTN_HEAD_DIM = 256
OFF_Q = POOL_WIDTH
OFF_K = OFF_Q + MLSTM_WIDTH
OFF_V = OFF_K + MLSTM_WIDTH
OFF_O = OFF_V + MLSTM_WIDTH
OFF_GATE = OFF_O + MLSTM_WIDTH

V7X_LANES = 128
V7X_SUBLANES = 8
V7X_VMEM_LIMIT_BYTES = 56 * 1024 * 1024

GATE_ROWS = 16
POOL_HALO = 16
CONV_HALO = 8

TM_PROJ = 512
TS_MIX = 512
L_CHUNK = 256
TS_ATT = 512
TM_FFN = 512
TF_FFN = 256


def _rmsnorm(x, g):
    return x * lax.rsqrt(jnp.mean(x * x, axis=-1, keepdims=True) + EPS) * g


def _sigmoid(x):
    return 1.0 / (1.0 + jnp.exp(-x))


def _log_sigmoid(x):
    return jnp.minimum(x, 0.0) - jnp.log(1.0 + jnp.exp(-jnp.abs(x)))


def _dot(a, b):
    return jnp.dot(a, b, preferred_element_type=F32)


def _dot_nt(a, b):
    return lax.dot_general(a, b, (((1,), (1,)), ((), ())), preferred_element_type=F32)


def _dot_tn(a, b):
    return lax.dot_general(a, b, (((0,), (0,)), ((), ())), preferred_element_type=F32)


def _split3(x):
    x1 = x.astype(BF16)
    r1 = x - x1.astype(F32)
    x2 = r1.astype(BF16)
    x3 = (r1 - x2.astype(F32)).astype(BF16)
    return x1, x2, x3


def _const_spec(shape):
    nd = len(shape)
    return pl.BlockSpec(shape, lambda *_: (0,) * nd)


def _inproj_kernel(x_ref, g_ref, wm_ref, wgc_ref, wgr_ref, bc_ref, br_ref,
                   pm_ref, gcol_ref, grow_ref):
    hb = _rmsnorm(x_ref[...], g_ref[...]).astype(BF16)
    pm_ref[...] = _dot(hb, wm_ref[...]).astype(BF16)
    gcol_ref[...] = _dot(hb, wgc_ref[...]) + bc_ref[...]
    grow_ref[...] = _dot_nt(wgr_ref[...], hb) + br_ref[...]


def _inproj(x2, g, wm, wgc, wgr, bc, br):
    T = x2.shape[0]
    tm = TM_PROJ
    ncol = wm.shape[1]
    return pl.pallas_call(
        _inproj_kernel,
        grid=(T // tm,),
        in_specs=[
            pl.BlockSpec((tm, D_MODEL), lambda i: (i, 0)),
            _const_spec((1, D_MODEL)),
            _const_spec((D_MODEL, ncol)),
            _const_spec((D_MODEL, V7X_LANES)),
            _const_spec((GATE_ROWS, D_MODEL)),
            _const_spec((1, V7X_LANES)),
            _const_spec((GATE_ROWS, 1)),
        ],
        out_specs=[
            pl.BlockSpec((tm, ncol), lambda i: (i, 0)),
            pl.BlockSpec((tm, V7X_LANES), lambda i: (i, 0)),
            pl.BlockSpec((GATE_ROWS, tm), lambda i: (0, i)),
        ],
        out_shape=[
            jax.ShapeDtypeStruct((T, ncol), BF16),
            jax.ShapeDtypeStruct((T, V7X_LANES), F32),
            jax.ShapeDtypeStruct((GATE_ROWS, T), F32),
        ],
        compiler_params=pltpu.CompilerParams(
            dimension_semantics=("arbitrary",), vmem_limit_bytes=V7X_VMEM_LIMIT_BYTES),
        name="inproj",
    )(x2, g, wm, wgc, wgr, bc, br)


def _mixer_kernel(pm_ref, gcol_ref, grow_ref, x_ref, conv_ref, wpool_ref, pscale_ref,
                  gnorm_ref, wout_ref, gpost_ref, o_ref,
                  uext_ref, qkext_ref, c_ref, n_ref, m_ref, mix_ref):
    ts = x_ref.shape[0]
    L = L_CHUNK
    t = pl.program_id(1)

    @pl.when(t == 0)
    def _():
        uext_ref[0:POOL_HALO, :] = jnp.zeros((POOL_HALO, POOL_WIDTH), F32)
        qkext_ref[0:CONV_HALO, :] = jnp.zeros((CONV_HALO, 2 * MLSTM_WIDTH), F32)
        c_ref[...] = jnp.zeros_like(c_ref)
        n_ref[...] = jnp.zeros_like(n_ref)
        m_ref[...] = jnp.zeros_like(m_ref)

    uext_ref[POOL_HALO:POOL_HALO + ts, :] = pm_ref[:, 0:POOL_WIDTH].astype(F32)
    qkext_ref[CONV_HALO:CONV_HALO + ts, :] = pm_ref[:, OFF_Q:OFF_V].astype(F32)

    pos = t * ts + lax.broadcasted_iota(jnp.int32, (ts, 1), 0)
    for g, win in enumerate(POOL_WINDOWS):
        c0 = g * POOL_GROUP_DIM
        cur = uext_ref[POOL_HALO:POOL_HALO + ts, c0:c0 + POOL_GROUP_DIM]
        acc = cur
        for k in range(1, win):
            acc = acc + uext_ref[POOL_HALO - k:POOL_HALO - k + ts, c0:c0 + POOL_GROUP_DIM]
        count = jnp.minimum(pos + 1, win).astype(F32)
        d = acc / count - cur
        y = _dot(d.astype(BF16), wpool_ref[g]) * pscale_ref[:, c0:c0 + POOL_GROUP_DIM]
        mix_ref[:, c0:c0 + POOL_GROUP_DIM] = y.astype(BF16)

    ri = lax.broadcasted_iota(jnp.int32, (L, L), 0)
    ci = lax.broadcasted_iota(jnp.int32, (L, L), 1)
    causal = ci <= ri
    tri_lo = causal.astype(BF16)
    tri_up = (ri <= ci).astype(BF16)
    cw = conv_ref[...]

    cs = [c_ref[h] for h in range(MLSTM_HEADS)]
    ns = [n_ref[h:h + 1, :] for h in range(MLSTM_HEADS)]
    ms = [m_ref[h:h + 1, 0:1] for h in range(MLSTM_HEADS)]

    def conv_silu(col0, r0):
        acc = None
        for j in range(CONV_WIDTH):
            off = CONV_HALO + r0 - (CONV_WIDTH - 1) + j
            term = cw[j:j + 1, col0:col0 + MLSTM_HEAD_DIM] * qkext_ref[off:off + L, col0:col0 + MLSTM_HEAD_DIM]
            acc = term if acc is None else acc + term
        return acc * _sigmoid(acc)

    for c in range(ts // L):
        r0 = c * L
        gcol = gcol_ref[r0:r0 + L, :]
        lf1, lf2, lf3 = _split3(_log_sigmoid(gcol))
        bcol_all = _dot(tri_lo, lf1) + _dot(tri_lo, lf2) + _dot(tri_lo, lf3)
        grow = grow_ref[:, r0:r0 + L]
        lr1, lr2, lr3 = _split3(_log_sigmoid(grow))
        brow_all = _dot(lr1, tri_up) + _dot(lr2, tri_up) + _dot(lr3, tri_up)

        for h in range(MLSTM_HEADS):
            hc = h * MLSTM_HEAD_DIM
            q = conv_silu(hc, r0) * (MLSTM_HEAD_DIM ** -0.5)
            k = conv_silu(MLSTM_WIDTH + hc, r0)
            vb = pm_ref[r0:r0 + L, OFF_V + hc:OFF_V + hc + MLSTM_HEAD_DIM]
            ig_col = gcol[:, h:h + 1]
            b_col = bcol_all[:, MLSTM_HEADS + h:MLSTM_HEADS + h + 1]
            ig_row = grow[h:h + 1, :]
            b_row = brow_all[MLSTM_HEADS + h:MLSTM_HEADS + h + 1, :]
            C, n, m = cs[h], ns[h], ms[h]

            qb = q.astype(BF16)
            kb = k.astype(BF16)
            S = _dot_nt(qb, kb)
            Dm = jnp.where(causal, b_col - b_row + ig_row, -jnp.inf)
            inter = b_col + m
            m_i = jnp.maximum(jnp.max(Dm, axis=1, keepdims=True), inter)
            A = S * jnp.exp(Dm - m_i)
            w_inter = jnp.exp(inter - m_i)
            num = _dot(A.astype(BF16), vb) + w_inter * _dot(qb, C.astype(BF16))
            den = jnp.sum(A, axis=1, keepdims=True) + w_inter * jnp.sum(q * n, axis=1, keepdims=True)
            hout = num / jnp.maximum(jnp.abs(den), jnp.exp(-m_i))

            bL = b_col[L - 1:L, :]
            gk = bL - b_col + ig_col
            m_new = jnp.maximum(bL + m, jnp.max(gk, axis=0, keepdims=True))
            decay = jnp.exp(bL + m - m_new)
            kw = jnp.exp(gk - m_new) * k
            cs[h] = decay * C + _dot_tn(kw.astype(BF16), vb)
            ns[h] = decay * n + jnp.sum(kw, axis=0, keepdims=True)
            ms[h] = m_new

            hn = hout * lax.rsqrt(jnp.mean(hout * hout, axis=1, keepdims=True) + EPS)
            og = _sigmoid(pm_ref[r0:r0 + L, OFF_O + hc:OFF_O + hc + MLSTM_HEAD_DIM].astype(F32))
            mout = hn * gnorm_ref[:, hc:hc + MLSTM_HEAD_DIM] * og
            mix_ref[r0:r0 + L, POOL_WIDTH + hc:POOL_WIDTH + hc + MLSTM_HEAD_DIM] = mout.astype(BF16)

    for h in range(MLSTM_HEADS):
        c_ref[h] = cs[h]
        n_ref[h:h + 1, :] = ns[h]
        m_ref[h:h + 1, :] = jnp.broadcast_to(ms[h], (1, V7X_LANES))

    uext_ref[0:POOL_HALO, :] = uext_ref[ts:ts + POOL_HALO, :]
    qkext_ref[0:CONV_HALO, :] = qkext_ref[ts:ts + CONV_HALO, :]

    mix = _dot(mix_ref[...], wout_ref[...])
    o_ref[...] = x_ref[...] + _rmsnorm(mix, gpost_ref[...])


def _mixer(pm, gcol, grow, x2, conv, wpool, pscale, gnorm, wout, gpost, batch, seq):
    T = x2.shape[0]
    ts = TS_MIX
    nt = seq // ts
    ncol = pm.shape[1]
    tok = lambda b, t: (b * nt + t, 0)
    return pl.pallas_call(
        _mixer_kernel,
        grid=(batch, nt),
        in_specs=[
            pl.BlockSpec((ts, ncol), tok),
            pl.BlockSpec((ts, V7X_LANES), tok),
            pl.BlockSpec((GATE_ROWS, ts), lambda b, t: (0, b * nt + t)),
            pl.BlockSpec((ts, D_MODEL), tok),
            _const_spec((CONV_WIDTH, 2 * MLSTM_WIDTH)),
            _const_spec((POOL_GROUPS, POOL_GROUP_DIM, POOL_GROUP_DIM)),
            _const_spec((1, POOL_WIDTH)),
            _const_spec((1, MLSTM_WIDTH)),
            _const_spec((D_MODEL, D_MODEL)),
            _const_spec((1, D_MODEL)),
        ],
        out_specs=pl.BlockSpec((ts, D_MODEL), tok),
        out_shape=jax.ShapeDtypeStruct((T, D_MODEL), F32),
        scratch_shapes=[
            pltpu.VMEM((ts + POOL_HALO, POOL_WIDTH), F32),
            pltpu.VMEM((ts + CONV_HALO, 2 * MLSTM_WIDTH), F32),
            pltpu.VMEM((MLSTM_HEADS, MLSTM_HEAD_DIM, MLSTM_HEAD_DIM), F32),
            pltpu.VMEM((V7X_SUBLANES, MLSTM_HEAD_DIM), F32),
            pltpu.VMEM((V7X_SUBLANES, V7X_LANES), F32),
            pltpu.VMEM((ts, D_MODEL), BF16),
        ],
        compiler_params=pltpu.CompilerParams(
            dimension_semantics=("arbitrary", "arbitrary"), vmem_limit_bytes=V7X_VMEM_LIMIT_BYTES),
        name="mixer",
    )(pm, gcol, grow, x2, conv, wpool, pscale, gnorm, wout, gpost)


def _memkv_kernel(mem_ref, g_ref, wkv_ref, k_ref, v_ref):
    mb = _rmsnorm(mem_ref[0], g_ref[...]).astype(BF16)
    kv = _dot(mb, wkv_ref[...])
    k_ref[0] = kv[:, :D_MODEL].astype(BF16)
    v_ref[0] = kv[:, D_MODEL:].astype(BF16)


def _memkv(mem, g, wkv):
    B = mem.shape[0]
    blk = pl.BlockSpec((1, N_MEM, D_MODEL), lambda b: (b, 0, 0))
    return pl.pallas_call(
        _memkv_kernel,
        grid=(B,),
        in_specs=[blk, _const_spec((1, D_MODEL)), _const_spec((D_MODEL, 2 * D_MODEL))],
        out_specs=[blk, blk],
        out_shape=[jax.ShapeDtypeStruct((B, N_MEM, D_MODEL), BF16)] * 2,
        compiler_params=pltpu.CompilerParams(
            dimension_semantics=("arbitrary",), vmem_limit_bytes=V7X_VMEM_LIMIT_BYTES),
        name="memkv",
    )(mem, g, wkv)


def _xattn_kernel(x_ref, k_ref, v_ref, gpre_ref, wq_ref, wo_ref, gpost_ref, o_ref, att_ref):
    x = x_ref[...]
    hb = _rmsnorm(x, gpre_ref[...]).astype(BF16)
    qb = (_dot(hb, wq_ref[...]) * (XATTN_HEAD_DIM ** -0.5)).astype(BF16)
    for h in range(XATTN_HEADS):
        hc = h * XATTN_HEAD_DIM
        s = _dot_nt(qb[:, hc:hc + XATTN_HEAD_DIM], k_ref[0, :, hc:hc + XATTN_HEAD_DIM])
        p = jnp.exp(s - jnp.max(s, axis=1, keepdims=True))
        o = _dot(p.astype(BF16), v_ref[0, :, hc:hc + XATTN_HEAD_DIM]) / jnp.sum(p, axis=1, keepdims=True)
        att_ref[:, hc:hc + XATTN_HEAD_DIM] = o.astype(BF16)
    y = _dot(att_ref[...], wo_ref[...])
    o_ref[...] = x + _rmsnorm(y, gpost_ref[...])


def _xattn(x2, k, v, gpre, wq, wo, gpost, batch, seq):
    T = x2.shape[0]
    ts = TS_ATT
    nt = seq // ts
    tok = lambda b, t: (b * nt + t, 0)
    kvb = pl.BlockSpec((1, N_MEM, D_MODEL), lambda b, t: (b, 0, 0))
    return pl.pallas_call(
        _xattn_kernel,
        grid=(batch, nt),
        in_specs=[
            pl.BlockSpec((ts, D_MODEL), tok), kvb, kvb,
            _const_spec((1, D_MODEL)),
            _const_spec((D_MODEL, D_MODEL)),
            _const_spec((D_MODEL, D_MODEL)),
            _const_spec((1, D_MODEL)),
        ],
        out_specs=pl.BlockSpec((ts, D_MODEL), tok),
        out_shape=jax.ShapeDtypeStruct((T, D_MODEL), F32),
        scratch_shapes=[pltpu.VMEM((ts, D_MODEL), BF16)],
        compiler_params=pltpu.CompilerParams(
            dimension_semantics=("arbitrary", "arbitrary"), vmem_limit_bytes=V7X_VMEM_LIMIT_BYTES),
        name="xattn",
    )(x2, k, v, gpre, wq, wo, gpost)


def _ffn_kernel(x_ref, gpre_ref, wgu_ref, wd_ref, gpost_ref, o_ref, acc_ref):
    x = x_ref[...]
    hb = _rmsnorm(x, gpre_ref[...]).astype(BF16)
    nchunk = wgu_ref.shape[0]
    tf = wd_ref.shape[1]

    def body(c, carry):
        gu = _dot(hb, wgu_ref[c])
        gate = gu[:, :tf]
        a = gate * _sigmoid(gate) * gu[:, tf:]
        contrib = _dot(a.astype(BF16), wd_ref[c])

        @pl.when(c == 0)
        def _():
            acc_ref[...] = contrib

        @pl.when(c > 0)
        def _():
            acc_ref[...] += contrib
        return carry

    lax.fori_loop(0, nchunk, body, 0)
    o_ref[...] = x + _rmsnorm(acc_ref[...], gpost_ref[...])


def _ffn(x2, gpre, wgu, wd, gpost):
    T = x2.shape[0]
    tm = TM_FFN
    nchunk, _, tf2 = wgu.shape
    tf = tf2 // 2
    return pl.pallas_call(
        _ffn_kernel,
        grid=(T // tm,),
        in_specs=[
            pl.BlockSpec((tm, D_MODEL), lambda i: (i, 0)),
            _const_spec((1, D_MODEL)),
            _const_spec((nchunk, D_MODEL, tf2)),
            _const_spec((nchunk, tf, D_MODEL)),
            _const_spec((1, D_MODEL)),
        ],
        out_specs=pl.BlockSpec((tm, D_MODEL), lambda i: (i, 0)),
        out_shape=jax.ShapeDtypeStruct((T, D_MODEL), F32),
        scratch_shapes=[pltpu.VMEM((tm, D_MODEL), F32)],
        compiler_params=pltpu.CompilerParams(
            dimension_semantics=("arbitrary",), vmem_limit_bytes=V7X_VMEM_LIMIT_BYTES),
        name="ffn",
    )(x2, gpre, wgu, wd, gpost)


def kernel(x, mem, w_in, b_gate, conv_qk, w_pool, pool_scale, mlstm_norm_g, w_out, g_mix_pre, g_mix_post, g_mem, g_xattn_pre, g_xattn_post, wq_x, wk_x, wv_x, wo_x, g_ffn_pre, g_ffn_post, w_gate, w_up, w_down):
    B, S, D = x.shape
    depth = w_in.shape[0]
    d_ff = w_gate.shape[-1]
    nchunk = d_ff // TF_FFN
    ngate = 2 * MLSTM_HEADS
    x2 = x.reshape(B * S, D)
    row = lambda v: v.reshape(1, -1).astype(F32)

    for l in range(depth):
        wm = w_in[l, :, :OFF_GATE].astype(BF16)
        wg = w_in[l, :, OFF_GATE:]
        wgc = jnp.pad(wg, ((0, 0), (0, V7X_LANES - ngate))).astype(BF16)
        wgr = jnp.pad(wg.T, ((0, GATE_ROWS - ngate), (0, 0))).astype(BF16)
        bc = jnp.pad(b_gate[l].astype(F32), (0, V7X_LANES - ngate)).reshape(1, V7X_LANES)
        br = jnp.pad(b_gate[l].astype(F32), (0, GATE_ROWS - ngate)).reshape(GATE_ROWS, 1)
        pm, gcol, grow = _inproj(x2, row(g_mix_pre[l]), wm, wgc, wgr, bc, br)
        x2 = _mixer(pm, gcol, grow, x2, conv_qk[l].astype(F32), w_pool[l].astype(BF16),
                    row(pool_scale[l]), row(mlstm_norm_g[l]), w_out[l].astype(BF16),
                    row(g_mix_post[l]), B, S)

        wkv = jnp.concatenate([wk_x[l], wv_x[l]], axis=1).astype(BF16)
        k, v = _memkv(mem, row(g_mem[l]), wkv)
        x2 = _xattn(x2, k, v, row(g_xattn_pre[l]), wq_x[l].astype(BF16), wo_x[l].astype(BF16),
                    row(g_xattn_post[l]), B, S)

        wgu = jnp.concatenate(
            [w_gate[l].reshape(D, nchunk, TF_FFN), w_up[l].reshape(D, nchunk, TF_FFN)], axis=2)
        wgu = jnp.transpose(wgu, (1, 0, 2)).astype(BF16)
        wd = w_down[l].reshape(nchunk, TF_FFN, D).astype(BF16)
        x2 = _ffn(x2, row(g_ffn_pre[l]), wgu, wd, row(g_ffn_post[l]))

    return x2.reshape(B, S, D)
```

```python
import functools

import numpy as np
import jax
import jax.numpy as jnp
from jax import lax
from jax.experimental import pallas as pl
from jax.experimental.pallas import tpu as pltpu

F32 = jnp.float32
BF16 = jnp.bfloat16

EPS = 1e-6
D_MODEL = 1024
N_MEM = 256
POOL_WIDTH = 512
POOL_GROUPS = 4
POOL_GROUP_DIM = 128
POOL_WINDOWS = (2, 4, 8, 16)
MLSTM_WIDTH = 512
MLSTM_HEADS = 4
MLSTM_HEAD_DIM = 128
CONV_WIDTH = 4
XATTN_HEADS = 4
XATTN_HEAD_DIM = 256
OFF_Q = POOL_WIDTH
OFF_K = OFF_Q + MLSTM_WIDTH
OFF_V = OFF_K + MLSTM_WIDTH
OFF_O = OFF_V + MLSTM_WIDTH
OFF_GATE = OFF_O + MLSTM_WIDTH

V7X_LANES = 128
V7X_SUBLANES = 8
V7X_VMEM_LIMIT_BYTES = 56 * 1024 * 1024

IN_COLS_PAD = OFF_GATE + V7X_LANES
POOL_HALO = 16
CONV_HALO = 8
POOL_BLOCK = 256

TS_MIX = 512
L_CHUNK = 256
TS_ATT = 512
TM_FFN = 512
TF_FFN = 256


def _rmsnorm(x, g):
    return x * lax.rsqrt(jnp.mean(x * x, axis=-1, keepdims=True) + EPS) * g


def _sigmoid(x):
    return 1.0 / (1.0 + jnp.exp(-x))


def _log_sigmoid(x):
    return jnp.minimum(x, 0.0) - jnp.log(1.0 + jnp.exp(-jnp.abs(x)))


def _dot(a, b):
    return jnp.dot(a, b, preferred_element_type=F32)


def _dot_nt(a, b):
    return lax.dot_general(a, b, (((1,), (1,)), ((), ())), preferred_element_type=F32)


def _dot_tn(a, b):
    return lax.dot_general(a, b, (((0,), (0,)), ((), ())), preferred_element_type=F32)


def _const_spec(shape):
    nd = len(shape)
    return pl.BlockSpec(shape, lambda *_: (0,) * nd)


def _pool_matrices():
    i = np.arange(POOL_BLOCK)[:, None]
    j = np.arange(POOL_BLOCK)[None, :]
    mats = []
    for win in POOL_WINDOWS:
        inside = (j <= i) & (j > i - win)
        mats.append(inside.astype(np.float32) / win - (i == j).astype(np.float32))
    return jnp.asarray(np.stack(mats), dtype=BF16)


def _mixer_kernel(xa_ref, xb_ref, gpre_ref, win_ref, bias_ref, conv_ref, pmat_ref, wpool_ref,
                  pscale_ref, gnorm_ref, wout_ref, gpost_ref, o_ref,
                  uext_ref, qkext_ref, v_ref, og_ref, g_ref, c_ref, n_ref, m_ref, mix_ref, *, nt):
    ts = xa_ref.shape[0]
    L = L_CHUNK
    H = MLSTM_HEADS
    DH = MLSTM_HEAD_DIM
    s = pl.program_id(0)
    a = lax.rem(s, 2)
    b = 1 - a
    first_a = lax.rem(s, nt) == 0
    tb = lax.rem(s + nt - 1, nt)
    first_b = tb == 0

    @pl.when(s == 0)
    def _():
        uext_ref[1] = jnp.zeros(uext_ref.shape[1:], F32)
        qkext_ref[1] = jnp.zeros(qkext_ref.shape[1:], F32)
        v_ref[1] = jnp.zeros(v_ref.shape[1:], BF16)
        og_ref[1] = jnp.zeros(og_ref.shape[1:], F32)
        g_ref[1] = jnp.zeros(g_ref.shape[1:], F32)
        c_ref[...] = jnp.zeros_like(c_ref)
        n_ref[...] = jnp.zeros_like(n_ref)
        m_ref[...] = jnp.zeros_like(m_ref)

    hb = _rmsnorm(xa_ref[...], gpre_ref[...]).astype(BF16)
    uext_ref[a, POOL_HALO:POOL_HALO + ts, :] = _dot(hb, win_ref[:, 0:OFF_Q])
    qkext_ref[a, CONV_HALO:CONV_HALO + ts, :] = _dot(hb, win_ref[:, OFF_Q:OFF_V])
    v_ref[a] = _dot(hb, win_ref[:, OFF_V:OFF_O]).astype(BF16)
    og_ref[a] = _dot(hb, win_ref[:, OFF_O:OFF_GATE])
    g_ref[a] = _dot(hb, win_ref[:, OFF_GATE:IN_COLS_PAD]) + bias_ref[...]
    uext_ref[a, 0:POOL_HALO, :] = jnp.where(first_a, 0.0, uext_ref[b, ts:ts + POOL_HALO, :])
    qkext_ref[a, 0:CONV_HALO, :] = jnp.where(first_a, 0.0, qkext_ref[b, ts:ts + CONV_HALO, :])

    for blk in range(ts // POOL_BLOCK):
        r0 = blk * POOL_BLOCK
        pos = tb * ts + r0 + lax.broadcasted_iota(jnp.int32, (POOL_HALO, 1), 0)
        for g, win in enumerate(POOL_WINDOWS):
            c0 = g * POOL_GROUP_DIM
            cols = slice(c0, c0 + POOL_GROUP_DIM)
            ublk = uext_ref[b, POOL_HALO + r0:POOL_HALO + r0 + POOL_BLOCK, cols]
            d_main = _dot(pmat_ref[g], ublk.astype(BF16))
            cur = uext_ref[b, POOL_HALO + r0:POOL_HALO + r0 + POOL_HALO, cols]
            acc = cur
            for k in range(1, win):
                acc = acc + uext_ref[b, POOL_HALO + r0 - k:POOL_HALO + r0 - k + POOL_HALO, cols]
            count = jnp.minimum(pos + 1, win).astype(F32)
            d_head = acc / count - cur
            d = jnp.concatenate([d_head, d_main[POOL_HALO:, :]], axis=0)
            y = _dot(d.astype(BF16), wpool_ref[g]) * pscale_ref[:, cols]
            mix_ref[r0:r0 + POOL_BLOCK, cols] = y.astype(BF16)

    ri = lax.broadcasted_iota(jnp.int32, (L, L), 0)
    ci = lax.broadcasted_iota(jnp.int32, (L, L), 1)
    causal = ci <= ri
    lane_x = lax.broadcasted_iota(jnp.int32, (L, V7X_LANES), 1)
    sub8 = lax.broadcasted_iota(jnp.int32, (V7X_SUBLANES, L), 0)
    lane8 = lax.broadcasted_iota(jnp.int32, (V7X_SUBLANES, L), 1)
    zpad = jnp.zeros((V7X_LANES - V7X_SUBLANES, L), F32)
    cw = conv_ref[...]

    cs = [jnp.where(first_b, 0.0, c_ref[h]) for h in range(H)]
    ns = [jnp.where(first_b, 0.0, n_ref[h:h + 1, :]) for h in range(H)]
    m_row = jnp.where(first_b, 0.0, m_ref[0:1, :])

    def conv_silu(col0, r0):
        acc = None
        for j in range(CONV_WIDTH):
            off = CONV_HALO + r0 - (CONV_WIDTH - 1) + j
            term = cw[j:j + 1, col0:col0 + DH] * qkext_ref[b, off:off + L, col0:col0 + DH]
            acc = term if acc is None else acc + term
        return acc * _sigmoid(acc)

    for c in range(ts // L):
        r0 = c * L
        G = g_ref[b, r0:r0 + L, :]
        R8 = G.T[0:V7X_SUBLANES, :]
        B8 = jnp.where(sub8 < H, _log_sigmoid(R8), 0.0)
        sh = 1
        while sh < L:
            B8 = B8 + jnp.where(lane8 >= sh, pltpu.roll(B8, sh, axis=1), 0.0)
            sh *= 2
        bX = jnp.concatenate([B8, zpad], axis=0).T
        igX = jnp.where(lane_x < H, pltpu.roll(G, V7X_LANES - H, axis=1), 0.0)
        interX = bX + m_row
        bL = bX[L - 1:L, :]
        gkX = bL - bX + igX
        m_new = jnp.maximum(bL + m_row, jnp.max(gkX, axis=0, keepdims=True))
        decay = jnp.exp(bL + m_row - m_new)
        wkX = jnp.exp(gkX - m_new)

        miX = jnp.zeros((L, V7X_LANES), F32)
        rsX = jnp.zeros((L, V7X_LANES), F32)
        qnX = jnp.zeros((L, V7X_LANES), F32)
        per_head = []
        for h in range(H):
            hc = h * DH
            q = conv_silu(hc, r0) * (DH ** -0.5)
            k = conv_silu(MLSTM_WIDTH + hc, r0)
            vb = v_ref[b, r0:r0 + L, hc:hc + DH]
            qb = q.astype(BF16)
            S = _dot_nt(qb, k.astype(BF16))
            rrow = R8[H + h:H + h + 1, :] - B8[h:h + 1, :]
            Dm = jnp.where(causal, bX[:, h:h + 1] + rrow, -jnp.inf)
            m_i = jnp.maximum(jnp.max(Dm, axis=1, keepdims=True), interX[:, h:h + 1])
            A = S * jnp.exp(Dm - m_i)
            av = _dot(A.astype(BF16), vb)
            qc = _dot(qb, cs[h].astype(BF16))
            sel = lane_x == h
            miX = jnp.where(sel, m_i, miX)
            rsX = jnp.where(sel, jnp.sum(A, axis=1, keepdims=True), rsX)
            qnX = jnp.where(sel, jnp.sum(q * ns[h], axis=1, keepdims=True), qnX)
            per_head.append((k, vb, av, qc))

        w_interX = jnp.exp(interX - miX)
        denX = rsX + w_interX * qnX
        rX = 1.0 / jnp.maximum(jnp.abs(denX), jnp.exp(-miX))

        for h in range(H):
            hc = h * DH
            k, vb, av, qc = per_head[h]
            hout = (av + w_interX[:, h:h + 1] * qc) * rX[:, h:h + 1]
            hn = hout * lax.rsqrt(jnp.mean(hout * hout, axis=1, keepdims=True) + EPS)
            og = _sigmoid(og_ref[b, r0:r0 + L, hc:hc + DH])
            mout = hn * gnorm_ref[:, hc:hc + DH] * og
            mix_ref[r0:r0 + L, POOL_WIDTH + hc:POOL_WIDTH + hc + DH] = mout.astype(BF16)

            kw = wkX[:, h:h + 1] * k
            dec = decay[:, h:h + 1]
            cs[h] = dec * cs[h] + _dot_tn(kw.astype(BF16), vb)
            ns[h] = dec * ns[h] + jnp.sum(kw, axis=0, keepdims=True)
        m_row = m_new

    for h in range(H):
        c_ref[h] = cs[h]
        n_ref[h:h + 1, :] = ns[h]
    m_ref[0:1, :] = m_row

    mix = _dot(mix_ref[...], wout_ref[...])
    o_ref[...] = xb_ref[...] + _rmsnorm(mix, gpost_ref[...])


def _mixer(x2, gpre, win, bias, conv, pmat, wpool, pscale, gnorm, wout, gpost, seq):
    T = x2.shape[0]
    ts = TS_MIX
    nt = seq // ts
    ntot = T // ts
    return pl.pallas_call(
        functools.partial(_mixer_kernel, nt=nt),
        grid=(ntot + 1,),
        in_specs=[
            pl.BlockSpec((ts, D_MODEL), lambda s: (jnp.minimum(s, ntot - 1), 0)),
            pl.BlockSpec((ts, D_MODEL), lambda s: (jnp.maximum(s - 1, 0), 0)),
            _const_spec((1, D_MODEL)),
            _const_spec((D_MODEL, IN_COLS_PAD)),
            _const_spec((1, V7X_LANES)),
            _const_spec((CONV_WIDTH, 2 * MLSTM_WIDTH)),
            _const_spec((POOL_GROUPS, POOL_BLOCK, POOL_BLOCK)),
            _const_spec((POOL_GROUPS, POOL_GROUP_DIM, POOL_GROUP_DIM)),
            _const_spec((1, POOL_WIDTH)),
            _const_spec((1, MLSTM_WIDTH)),
            _const_spec((D_MODEL, D_MODEL)),
            _const_spec((1, D_MODEL)),
        ],
        out_specs=pl.BlockSpec((ts, D_MODEL), lambda s: (jnp.maximum(s - 1, 0), 0)),
        out_shape=jax.ShapeDtypeStruct((T, D_MODEL), F32),
        scratch_shapes=[
            pltpu.VMEM((2, ts + POOL_HALO, POOL_WIDTH), F32),
            pltpu.VMEM((2, ts + CONV_HALO, 2 * MLSTM_WIDTH), F32),
            pltpu.VMEM((2, ts, MLSTM_WIDTH), BF16),
            pltpu.VMEM((2, ts, MLSTM_WIDTH), F32),
            pltpu.VMEM((2, ts, V7X_LANES), F32),
            pltpu.VMEM((MLSTM_HEADS, MLSTM_HEAD_DIM, MLSTM_HEAD_DIM), F32),
            pltpu.VMEM((V7X_SUBLANES, MLSTM_HEAD_DIM), F32),
            pltpu.VMEM((V7X_SUBLANES, V7X_LANES), F32),
            pltpu.VMEM((ts, D_MODEL), BF16),
        ],
        compiler_params=pltpu.CompilerParams(
            dimension_semantics=("arbitrary",), vmem_limit_bytes=V7X_VMEM_LIMIT_BYTES),
        name="mixer",
    )(x2, x2, gpre, win, bias, conv, pmat, wpool, pscale, gnorm, wout, gpost)


def _memkv_kernel(mem_ref, g_ref, wkv_ref, k_ref, v_ref):
    mb = _rmsnorm(mem_ref[0], g_ref[...]).astype(BF16)
    kv = _dot(mb, wkv_ref[...])
    k_ref[0] = kv[:, :D_MODEL].astype(BF16)
    v_ref[0] = kv[:, D_MODEL:].astype(BF16)


def _memkv(mem, g, wkv):
    B = mem.shape[0]
    blk = pl.BlockSpec((1, N_MEM, D_MODEL), lambda b: (b, 0, 0))
    return pl.pallas_call(
        _memkv_kernel,
        grid=(B,),
        in_specs=[blk, _const_spec((1, D_MODEL)), _const_spec((D_MODEL, 2 * D_MODEL))],
        out_specs=[blk, blk],
        out_shape=[jax.ShapeDtypeStruct((B, N_MEM, D_MODEL), BF16)] * 2,
        compiler_params=pltpu.CompilerParams(
            dimension_semantics=("arbitrary",), vmem_limit_bytes=V7X_VMEM_LIMIT_BYTES),
        name="memkv",
    )(mem, g, wkv)


def _xattn_kernel(x_ref, k_ref, v_ref, gpre_ref, wq_ref, wo_ref, gpost_ref, o_ref, att_ref):
    x = x_ref[...]
    hb = _rmsnorm(x, gpre_ref[...]).astype(BF16)
    qb = (_dot(hb, wq_ref[...]) * (XATTN_HEAD_DIM ** -0.5)).astype(BF16)
    for h in range(XATTN_HEADS):
        hc = h * XATTN_HEAD_DIM
        s = _dot_nt(qb[:, hc:hc + XATTN_HEAD_DIM], k_ref[0, :, hc:hc + XATTN_HEAD_DIM])
        p = jnp.exp(s - jnp.max(s, axis=1, keepdims=True))
        o = _dot(p.astype(BF16), v_ref[0, :, hc:hc + XATTN_HEAD_DIM]) / jnp.sum(p, axis=1, keepdims=True)
        att_ref[:, hc:hc + XATTN_HEAD_DIM] = o.astype(BF16)
    y = _dot(att_ref[...], wo_ref[...])
    o_ref[...] = x + _rmsnorm(y, gpost_ref[...])


def _xattn(x2, k, v, gpre, wq, wo, gpost, batch, seq):
    T = x2.shape[0]
    ts = TS_ATT
    nt = seq // ts
    tok = lambda b, t: (b * nt + t, 0)
    kvb = pl.BlockSpec((1, N_MEM, D_MODEL), lambda b, t: (b, 0, 0))
    return pl.pallas_call(
        _xattn_kernel,
        grid=(batch, nt),
        in_specs=[
            pl.BlockSpec((ts, D_MODEL), tok), kvb, kvb,
            _const_spec((1, D_MODEL)),
            _const_spec((D_MODEL, D_MODEL)),
            _const_spec((D_MODEL, D_MODEL)),
            _const_spec((1, D_MODEL)),
        ],
        out_specs=pl.BlockSpec((ts, D_MODEL), tok),
        out_shape=jax.ShapeDtypeStruct((T, D_MODEL), F32),
        scratch_shapes=[pltpu.VMEM((ts, D_MODEL), BF16)],
        compiler_params=pltpu.CompilerParams(
            dimension_semantics=("arbitrary", "arbitrary"), vmem_limit_bytes=V7X_VMEM_LIMIT_BYTES),
        name="xattn",
    )(x2, k, v, gpre, wq, wo, gpost)


def _ffn_kernel(x_ref, gpre_ref, wg_ref, wu_ref, wd_ref, gpost_ref, o_ref):
    x = x_ref[...]
    hb = _rmsnorm(x, gpre_ref[...]).astype(BF16)
    tf = TF_FFN
    acc = None
    for c in range(wg_ref.shape[1] // tf):
        cols = slice(c * tf, (c + 1) * tf)
        gate = _dot(hb, wg_ref[:, cols])
        a = gate * _sigmoid(gate) * _dot(hb, wu_ref[:, cols])
        contrib = _dot(a.astype(BF16), wd_ref[cols, :])
        acc = contrib if acc is None else acc + contrib
    o_ref[...] = x + _rmsnorm(acc, gpost_ref[...])


def _ffn(x2, gpre, wg, wu, wd, gpost):
    T = x2.shape[0]
    tm = TM_FFN
    d_ff = wg.shape[1]
    return pl.pallas_call(
        _ffn_kernel,
        grid=(T // tm,),
        in_specs=[
            pl.BlockSpec((tm, D_MODEL), lambda i: (i, 0)),
            _const_spec((1, D_MODEL)),
            _const_spec((D_MODEL, d_ff)),
            _const_spec((D_MODEL, d_ff)),
            _const_spec((d_ff, D_MODEL)),
            _const_spec((1, D_MODEL)),
        ],
        out_specs=pl.BlockSpec((tm, D_MODEL), lambda i: (i, 0)),
        out_shape=jax.ShapeDtypeStruct((T, D_MODEL), F32),
        compiler_params=pltpu.CompilerParams(
            dimension_semantics=("arbitrary",), vmem_limit_bytes=V7X_VMEM_LIMIT_BYTES),
        name="ffn",
    )(x2, gpre, wg, wu, wd, gpost)


def kernel(x, mem, w_in, b_gate, conv_qk, w_pool, pool_scale, mlstm_norm_g, w_out, g_mix_pre, g_mix_post, g_mem, g_xattn_pre, g_xattn_post, wq_x, wk_x, wv_x, wo_x, g_ffn_pre, g_ffn_post, w_gate, w_up, w_down):
    B, S, D = x.shape
    depth = w_in.shape[0]
    H = MLSTM_HEADS
    x2 = x.reshape(B * S, D)
    row = lambda v: v.reshape(1, -1).astype(F32)
    pmat = _pool_matrices()

    for l in range(depth):
        win = jnp.concatenate(
            [w_in[l, :, :OFF_GATE], w_in[l, :, OFF_GATE + H:], w_in[l, :, OFF_GATE:OFF_GATE + H],
             jnp.zeros((D, V7X_LANES - 2 * H), w_in.dtype)], axis=1).astype(BF16)
        bias = jnp.concatenate(
            [b_gate[l, H:], b_gate[l, :H], jnp.zeros((V7X_LANES - 2 * H,), b_gate.dtype)]).reshape(1, V7X_LANES)
        x2 = _mixer(x2, row(g_mix_pre[l]), win, bias.astype(F32), conv_qk[l].astype(F32), pmat,
                    w_pool[l].astype(BF16), row(pool_scale[l]), row(mlstm_norm_g[l]),
                    w_out[l].astype(BF16), row(g_mix_post[l]), S)

        wkv = jnp.concatenate([wk_x[l], wv_x[l]], axis=1).astype(BF16)
        k, v = _memkv(mem, row(g_mem[l]), wkv)
        x2 = _xattn(x2, k, v, row(g_xattn_pre[l]), wq_x[l].astype(BF16), wo_x[l].astype(BF16),
                    row(g_xattn_post[l]), B, S)

        x2 = _ffn(x2, row(g_ffn_pre[l]), w_gate[l].astype(BF16), w_up[l].astype(BF16),
                  w_down[l].astype(BF16), row(g_ffn_post[l]))

    return x2.reshape(B, S, D)
```

```python
import functools

import numpy as np
import jax
import jax.numpy as jnp
from jax import lax
from jax.experimental import pallas as pl
from jax.experimental.pallas import tpu as pltpu

F32 = jnp.float32
BF16 = jnp.bfloat16

EPS = 1e-6
D_MODEL = 1024
N_MEM = 256
POOL_WIDTH = 512
POOL_GROUPS = 4
POOL_GROUP_DIM = 128
POOL_WINDOWS = (2, 4, 8, 16)
MLSTM_WIDTH = 512
MLSTM_HEADS = 4
MLSTM_HEAD_DIM = 128
CONV_WIDTH = 4
XATTN_HEADS = 4
XATTN_HEAD_DIM = 256
OFF_Q = POOL_WIDTH
OFF_K = OFF_Q + MLSTM_WIDTH
OFF_V = OFF_K + MLSTM_WIDTH
OFF_O = OFF_V + MLSTM_WIDTH
OFF_GATE = OFF_O + MLSTM_WIDTH

V7X_LANES = 128
V7X_SUBLANES = 8
V7X_VMEM_LIMIT_BYTES = 56 * 1024 * 1024

IN_COLS_PAD = OFF_GATE + V7X_LANES
POOL_HALO = 16
CONV_HALO = 8
POOL_BLOCK = 256

TS_MIX = 512
L_CHUNK = 256
TS_ATT = 1024
TM_FFN = 1024
SUB_ROWS = 512
TF_FFN = 256


def _rmsnorm(x, g):
    return x * lax.rsqrt(jnp.mean(x * x, axis=-1, keepdims=True) + EPS) * g


def _sigmoid(x):
    return 1.0 / (1.0 + jnp.exp(-x))


def _log_sigmoid(x):
    return jnp.minimum(x, 0.0) - jnp.log(1.0 + jnp.exp(-jnp.abs(x)))


def _dot(a, b):
    return jnp.dot(a, b, preferred_element_type=F32)


def _dot_nt(a, b):
    return lax.dot_general(a, b, (((1,), (1,)), ((), ())), preferred_element_type=F32)


def _dot_tn(a, b):
    return lax.dot_general(a, b, (((0,), (0,)), ((), ())), preferred_element_type=F32)


def _const_spec(shape):
    nd = len(shape)
    return pl.BlockSpec(shape, lambda *_: (0,) * nd, pipeline_mode=pl.Buffered(1))


def _pool_matrices():
    i = np.arange(POOL_BLOCK)[:, None]
    j = np.arange(POOL_BLOCK)[None, :]
    mats = []
    for win in POOL_WINDOWS:
        inside = (j <= i) & (j > i - win)
        mats.append(inside.astype(np.float32) / win - (i == j).astype(np.float32))
    return jnp.asarray(np.stack(mats), dtype=BF16)


def _mixer_kernel(*refs, nt):
    parity = lax.rem(pl.program_id(0), 2)
    pl.when(parity == 0)(functools.partial(_mixer_step, 0, 1, *refs, nt=nt))
    pl.when(parity == 1)(functools.partial(_mixer_step, 1, 0, *refs, nt=nt))


def _mixer_step(a, b, xa_ref, xb_ref, gpre_ref, win_ref, bias_ref, conv_ref, pmat_ref, wpool_ref,
                pscale_ref, gnorm_ref, wout_ref, gpost_ref, o_ref,
                uext_ref, qkext_ref, v_ref, og_ref, g_ref, c_ref, n_ref, m_ref, mix_ref, *, nt):
    ts = xa_ref.shape[0]
    L = L_CHUNK
    H = MLSTM_HEADS
    DH = MLSTM_HEAD_DIM
    s = pl.program_id(0)
    first_a = lax.rem(s, nt) == 0
    tb = lax.rem(s + nt - 1, nt)
    first_b = tb == 0

    @pl.when(s == 0)
    def _():
        uext_ref[1] = jnp.zeros(uext_ref.shape[1:], F32)
        qkext_ref[1] = jnp.zeros(qkext_ref.shape[1:], F32)
        v_ref[1] = jnp.zeros(v_ref.shape[1:], BF16)
        og_ref[1] = jnp.zeros(og_ref.shape[1:], F32)
        g_ref[1] = jnp.zeros(g_ref.shape[1:], F32)
        c_ref[...] = jnp.zeros_like(c_ref)
        n_ref[...] = jnp.zeros_like(n_ref)
        m_ref[...] = jnp.zeros_like(m_ref)

    hb = _rmsnorm(xa_ref[...], gpre_ref[...]).astype(BF16)
    uext_ref[a, POOL_HALO:POOL_HALO + ts, :] = _dot(hb, win_ref[:, 0:OFF_Q])
    qkext_ref[a, CONV_HALO:CONV_HALO + ts, :] = _dot(hb, win_ref[:, OFF_Q:OFF_V])
    v_ref[a] = _dot(hb, win_ref[:, OFF_V:OFF_O]).astype(BF16)
    og_ref[a] = _dot(hb, win_ref[:, OFF_O:OFF_GATE])
    g_ref[a] = _dot(hb, win_ref[:, OFF_GATE:IN_COLS_PAD]) + bias_ref[...]
    uext_ref[a, 0:POOL_HALO, :] = jnp.where(first_a, 0.0, uext_ref[b, ts:ts + POOL_HALO, :])
    qkext_ref[a, 0:CONV_HALO, :] = jnp.where(first_a, 0.0, qkext_ref[b, ts:ts + CONV_HALO, :])

    for blk in range(ts // POOL_BLOCK):
        r0 = blk * POOL_BLOCK
        pos = tb * ts + r0 + lax.broadcasted_iota(jnp.int32, (POOL_HALO, 1), 0)
        for g, win in enumerate(POOL_WINDOWS):
            c0 = g * POOL_GROUP_DIM
            cols = slice(c0, c0 + POOL_GROUP_DIM)
            ublk = uext_ref[b, POOL_HALO + r0:POOL_HALO + r0 + POOL_BLOCK, cols]
            d_main = _dot(pmat_ref[g], ublk.astype(BF16))
            cur = uext_ref[b, POOL_HALO + r0:POOL_HALO + r0 + POOL_HALO, cols]
            acc = cur
            for k in range(1, win):
                acc = acc + uext_ref[b, POOL_HALO + r0 - k:POOL_HALO + r0 - k + POOL_HALO, cols]
            count = jnp.minimum(pos + 1, win).astype(F32)
            d_head = acc / count - cur
            d = jnp.concatenate([d_head, d_main[POOL_HALO:, :]], axis=0)
            y = _dot(d.astype(BF16), wpool_ref[g]) * pscale_ref[:, cols]
            mix_ref[r0:r0 + POOL_BLOCK, cols] = y.astype(BF16)

    ri = lax.broadcasted_iota(jnp.int32, (L, L), 0)
    ci = lax.broadcasted_iota(jnp.int32, (L, L), 1)
    causal = ci <= ri
    lane_x = lax.broadcasted_iota(jnp.int32, (L, V7X_LANES), 1)
    sub8 = lax.broadcasted_iota(jnp.int32, (V7X_SUBLANES, L), 0)
    lane8 = lax.broadcasted_iota(jnp.int32, (V7X_SUBLANES, L), 1)
    zpad = jnp.zeros((V7X_LANES - V7X_SUBLANES, L), F32)
    cw = conv_ref[...]

    cs = [jnp.where(first_b, 0.0, c_ref[h]) for h in range(H)]
    ns = [jnp.where(first_b, 0.0, n_ref[h:h + 1, :]) for h in range(H)]
    m_row = jnp.where(first_b, 0.0, m_ref[0:1, :])

    def conv_silu(col0, r0):
        acc = None
        for j in range(CONV_WIDTH):
            off = CONV_HALO + r0 - (CONV_WIDTH - 1) + j
            term = cw[j:j + 1, col0:col0 + DH] * qkext_ref[b, off:off + L, col0:col0 + DH]
            acc = term if acc is None else acc + term
        return acc * _sigmoid(acc)

    for c in range(ts // L):
        r0 = c * L
        G = g_ref[b, r0:r0 + L, :]
        R8 = G.T[0:V7X_SUBLANES, :]
        B8 = jnp.where(sub8 < H, _log_sigmoid(R8), 0.0)
        sh = 1
        while sh < L:
            B8 = B8 + jnp.where(lane8 >= sh, pltpu.roll(B8, sh, axis=1), 0.0)
            sh *= 2
        bX = jnp.concatenate([B8, zpad], axis=0).T
        igX = jnp.where(lane_x < H, pltpu.roll(G, V7X_LANES - H, axis=1), 0.0)
        interX = bX + m_row
        bL = bX[L - 1:L, :]
        gkX = bL - bX + igX
        m_new = jnp.maximum(bL + m_row, jnp.max(gkX, axis=0, keepdims=True))
        decay = jnp.exp(bL + m_row - m_new)
        wkX = jnp.exp(gkX - m_new)

        miX = jnp.zeros((L, V7X_LANES), F32)
        rsX = jnp.zeros((L, V7X_LANES), F32)
        qnX = jnp.zeros((L, V7X_LANES), F32)
        per_head = []
        for h in range(H):
            hc = h * DH
            q = conv_silu(hc, r0) * (DH ** -0.5)
            k = conv_silu(MLSTM_WIDTH + hc, r0)
            vb = v_ref[b, r0:r0 + L, hc:hc + DH]
            qb = q.astype(BF16)
            S = _dot_nt(qb, k.astype(BF16))
            rrow = R8[H + h:H + h + 1, :] - B8[h:h + 1, :]
            Dm = jnp.where(causal, bX[:, h:h + 1] + rrow, -jnp.inf)
            m_i = jnp.maximum(jnp.max(Dm, axis=1, keepdims=True), interX[:, h:h + 1])
            A = S * jnp.exp(Dm - m_i)
            av = _dot(A.astype(BF16), vb)
            qc = _dot(qb, cs[h].astype(BF16))
            sel = lane_x == h
            miX = jnp.where(sel, m_i, miX)
            rsX = jnp.where(sel, jnp.sum(A, axis=1, keepdims=True), rsX)
            qnX = jnp.where(sel, jnp.sum(q * ns[h], axis=1, keepdims=True), qnX)
            per_head.append((k, vb, av, qc))

        w_interX = jnp.exp(interX - miX)
        denX = rsX + w_interX * qnX
        rX = 1.0 / jnp.maximum(jnp.abs(denX), jnp.exp(-miX))

        for h in range(H):
            hc = h * DH
            k, vb, av, qc = per_head[h]
            hout = (av + w_interX[:, h:h + 1] * qc) * rX[:, h:h + 1]
            hn = hout * lax.rsqrt(jnp.mean(hout * hout, axis=1, keepdims=True) + EPS)
            og = _sigmoid(og_ref[b, r0:r0 + L, hc:hc + DH])
            mout = hn * gnorm_ref[:, hc:hc + DH] * og
            mix_ref[r0:r0 + L, POOL_WIDTH + hc:POOL_WIDTH + hc + DH] = mout.astype(BF16)

            kw = wkX[:, h:h + 1] * k
            dec = decay[:, h:h + 1]
            cs[h] = dec * cs[h] + _dot_tn(kw.astype(BF16), vb)
            ns[h] = dec * ns[h] + jnp.sum(kw, axis=0, keepdims=True)
        m_row = m_new

    for h in range(H):
        c_ref[h] = cs[h]
        n_ref[h:h + 1, :] = ns[h]
    m_ref[0:1, :] = m_row

    mix = _dot(mix_ref[...], wout_ref[...])
    o_ref[...] = xb_ref[...] + _rmsnorm(mix, gpost_ref[...])


def _mixer(x2, gpre, win, bias, conv, pmat, wpool, pscale, gnorm, wout, gpost, seq):
    T = x2.shape[0]
    ts = TS_MIX
    nt = seq // ts
    ntot = T // ts
    return pl.pallas_call(
        functools.partial(_mixer_kernel, nt=nt),
        grid=(ntot + 1,),
        in_specs=[
            pl.BlockSpec((ts, D_MODEL), lambda s: (jnp.minimum(s, ntot - 1), 0)),
            pl.BlockSpec((ts, D_MODEL), lambda s: (jnp.maximum(s - 1, 0), 0)),
            _const_spec((1, D_MODEL)),
            _const_spec((D_MODEL, IN_COLS_PAD)),
            _const_spec((1, V7X_LANES)),
            _const_spec((CONV_WIDTH, 2 * MLSTM_WIDTH)),
            _const_spec((POOL_GROUPS, POOL_BLOCK, POOL_BLOCK)),
            _const_spec((POOL_GROUPS, POOL_GROUP_DIM, POOL_GROUP_DIM)),
            _const_spec((1, POOL_WIDTH)),
            _const_spec((1, MLSTM_WIDTH)),
            _const_spec((D_MODEL, D_MODEL)),
            _const_spec((1, D_MODEL)),
        ],
        out_specs=pl.BlockSpec((ts, D_MODEL), lambda s: (jnp.maximum(s - 1, 0), 0)),
        out_shape=jax.ShapeDtypeStruct((T, D_MODEL), F32),
        scratch_shapes=[
            pltpu.VMEM((2, ts + POOL_HALO, POOL_WIDTH), F32),
            pltpu.VMEM((2, ts + CONV_HALO, 2 * MLSTM_WIDTH), F32),
            pltpu.VMEM((2, ts, MLSTM_WIDTH), BF16),
            pltpu.VMEM((2, ts, MLSTM_WIDTH), F32),
            pltpu.VMEM((2, ts, V7X_LANES), F32),
            pltpu.VMEM((MLSTM_HEADS, MLSTM_HEAD_DIM, MLSTM_HEAD_DIM), F32),
            pltpu.VMEM((V7X_SUBLANES, MLSTM_HEAD_DIM), F32),
            pltpu.VMEM((V7X_SUBLANES, V7X_LANES), F32),
            pltpu.VMEM((ts, D_MODEL), BF16),
        ],
        compiler_params=pltpu.CompilerParams(
            dimension_semantics=("arbitrary",), vmem_limit_bytes=V7X_VMEM_LIMIT_BYTES),
        name="mixer",
    )(x2, x2, gpre, win, bias, conv, pmat, wpool, pscale, gnorm, wout, gpost)


def _memkv_kernel(mem_ref, g_ref, wkv_ref, k_ref, v_ref):
    mb = _rmsnorm(mem_ref[0], g_ref[...]).astype(BF16)
    kv = _dot(mb, wkv_ref[...])
    k_ref[0] = kv[:, :D_MODEL].astype(BF16)
    v_ref[0] = kv[:, D_MODEL:].astype(BF16)


def _memkv(mem, g, wkv):
    B = mem.shape[0]
    blk = pl.BlockSpec((1, N_MEM, D_MODEL), lambda b: (b, 0, 0))
    return pl.pallas_call(
        _memkv_kernel,
        grid=(B,),
        in_specs=[blk, _const_spec((1, D_MODEL)), _const_spec((D_MODEL, 2 * D_MODEL))],
        out_specs=[blk, blk],
        out_shape=[jax.ShapeDtypeStruct((B, N_MEM, D_MODEL), BF16)] * 2,
        compiler_params=pltpu.CompilerParams(
            dimension_semantics=("arbitrary",), vmem_limit_bytes=V7X_VMEM_LIMIT_BYTES),
        name="memkv",
    )(mem, g, wkv)


def _xattn_kernel(x_ref, k_ref, v_ref, gpre_ref, wq_ref, wo_ref, gpost_ref, o_ref, att_ref):
    for r0 in range(0, x_ref.shape[0], SUB_ROWS):
        rows = slice(r0, r0 + SUB_ROWS)
        x = x_ref[rows, :]
        hb = _rmsnorm(x, gpre_ref[...]).astype(BF16)
        qb = (_dot(hb, wq_ref[...]) * (XATTN_HEAD_DIM ** -0.5)).astype(BF16)
        for h in range(XATTN_HEADS):
            hc = h * XATTN_HEAD_DIM
            s = _dot_nt(qb[:, hc:hc + XATTN_HEAD_DIM], k_ref[0, :, hc:hc + XATTN_HEAD_DIM])
            p = jnp.exp(s - jnp.max(s, axis=1, keepdims=True))
            o = _dot(p.astype(BF16), v_ref[0, :, hc:hc + XATTN_HEAD_DIM]) / jnp.sum(p, axis=1, keepdims=True)
            att_ref[rows, hc:hc + XATTN_HEAD_DIM] = o.astype(BF16)
        y = _dot(att_ref[rows, :], wo_ref[...])
        o_ref[rows, :] = x + _rmsnorm(y, gpost_ref[...])


def _xattn(x2, k, v, gpre, wq, wo, gpost, batch, seq):
    T = x2.shape[0]
    ts = TS_ATT
    nt = seq // ts
    tok = lambda b, t: (b * nt + t, 0)
    kvb = pl.BlockSpec((1, N_MEM, D_MODEL), lambda b, t: (b, 0, 0))
    return pl.pallas_call(
        _xattn_kernel,
        grid=(batch, nt),
        in_specs=[
            pl.BlockSpec((ts, D_MODEL), tok), kvb, kvb,
            _const_spec((1, D_MODEL)),
            _const_spec((D_MODEL, D_MODEL)),
            _const_spec((D_MODEL, D_MODEL)),
            _const_spec((1, D_MODEL)),
        ],
        out_specs=pl.BlockSpec((ts, D_MODEL), tok),
        out_shape=jax.ShapeDtypeStruct((T, D_MODEL), F32),
        scratch_shapes=[pltpu.VMEM((ts, D_MODEL), BF16)],
        compiler_params=pltpu.CompilerParams(
            dimension_semantics=("arbitrary", "arbitrary"), vmem_limit_bytes=V7X_VMEM_LIMIT_BYTES),
        name="xattn",
    )(x2, k, v, gpre, wq, wo, gpost)


def _ffn_kernel(x_ref, gpre_ref, wg_ref, wu_ref, wd_ref, gpost_ref, o_ref):
    tf = TF_FFN
    for r0 in range(0, x_ref.shape[0], SUB_ROWS):
        rows = slice(r0, r0 + SUB_ROWS)
        x = x_ref[rows, :]
        hb = _rmsnorm(x, gpre_ref[...]).astype(BF16)
        acc = None
        for c in range(wg_ref.shape[1] // tf):
            cols = slice(c * tf, (c + 1) * tf)
            gate = _dot(hb, wg_ref[:, cols])
            a = gate * _sigmoid(gate) * _dot(hb, wu_ref[:, cols])
            contrib = _dot(a.astype(BF16), wd_ref[cols, :])
            acc = contrib if acc is None else acc + contrib
        o_ref[rows, :] = x + _rmsnorm(acc, gpost_ref[...])


def _ffn(x2, gpre, wg, wu, wd, gpost):
    T = x2.shape[0]
    tm = TM_FFN
    d_ff = wg.shape[1]
    return pl.pallas_call(
        _ffn_kernel,
        grid=(T // tm,),
        in_specs=[
            pl.BlockSpec((tm, D_MODEL), lambda i: (i, 0)),
            _const_spec((1, D_MODEL)),
            _const_spec((D_MODEL, d_ff)),
            _const_spec((D_MODEL, d_ff)),
            _const_spec((d_ff, D_MODEL)),
            _const_spec((1, D_MODEL)),
        ],
        out_specs=pl.BlockSpec((tm, D_MODEL), lambda i: (i, 0)),
        out_shape=jax.ShapeDtypeStruct((T, D_MODEL), F32),
        compiler_params=pltpu.CompilerParams(
            dimension_semantics=("arbitrary",), vmem_limit_bytes=V7X_VMEM_LIMIT_BYTES),
        name="ffn",
    )(x2, gpre, wg, wu, wd, gpost)


def kernel(x, mem, w_in, b_gate, conv_qk, w_pool, pool_scale, mlstm_norm_g, w_out, g_mix_pre, g_mix_post, g_mem, g_xattn_pre, g_xattn_post, wq_x, wk_x, wv_x, wo_x, g_ffn_pre, g_ffn_post, w_gate, w_up, w_down):
    B, S, D = x.shape
    depth = w_in.shape[0]
    H = MLSTM_HEADS
    x2 = x.reshape(B * S, D)
    row = lambda v: v.reshape(1, -1).astype(F32)
    pmat = _pool_matrices()

    for l in range(depth):
        win = jnp.concatenate(
            [w_in[l, :, :OFF_GATE], w_in[l, :, OFF_GATE + H:], w_in[l, :, OFF_GATE:OFF_GATE + H],
             jnp.zeros((D, V7X_LANES - 2 * H), w_in.dtype)], axis=1).astype(BF16)
        bias = jnp.concatenate(
            [b_gate[l, H:], b_gate[l, :H], jnp.zeros((V7X_LANES - 2 * H,), b_gate.dtype)]).reshape(1, V7X_LANES)
        x2 = _mixer(x2, row(g_mix_pre[l]), win, bias.astype(F32), conv_qk[l].astype(F32), pmat,
                    w_pool[l].astype(BF16), row(pool_scale[l]), row(mlstm_norm_g[l]),
                    w_out[l].astype(BF16), row(g_mix_post[l]), S)

        wkv = jnp.concatenate([wk_x[l], wv_x[l]], axis=1).astype(BF16)
        k, v = _memkv(mem, row(g_mem[l]), wkv)
        x2 = _xattn(x2, k, v, row(g_xattn_pre[l]), wq_x[l].astype(BF16), wo_x[l].astype(BF16),
                    row(g_xattn_post[l]), B, S)

        x2 = _ffn(x2, row(g_ffn_pre[l]), w_gate[l].astype(BF16), w_up[l].astype(BF16),
                  w_down[l].astype(BF16), row(g_ffn_post[l]))

    return x2.reshape(B, S, D)
```

```python
import functools

import numpy as np
import jax
import jax.numpy as jnp
from jax import lax
from jax.experimental import pallas as pl
from jax.experimental.pallas import tpu as pltpu

F32 = jnp.float32
BF16 = jnp.bfloat16

EPS = 1e-6
D_MODEL = 1024
N_MEM = 256
POOL_WIDTH = 512
POOL_GROUPS = 4
POOL_GROUP_DIM = 128
POOL_WINDOWS = (2, 4, 8, 16)
MLSTM_WIDTH = 512
MLSTM_HEADS = 4
MLSTM_HEAD_DIM = 128
CONV_WIDTH = 4
XATTN_HEADS = 4
XATTN_HEAD_DIM = 256
OFF_Q = POOL_WIDTH
OFF_K = OFF_Q + MLSTM_WIDTH
OFF_V = OFF_K + MLSTM_WIDTH
OFF_O = OFF_V + MLSTM_WIDTH
OFF_GATE = OFF_O + MLSTM_WIDTH

V7X_LANES = 128
V7X_SUBLANES = 8
V7X_VMEM_LIMIT_BYTES = 56 * 1024 * 1024

IN_COLS_PAD = OFF_GATE + V7X_LANES
POOL_HALO = 16
CONV_HALO = 8
POOL_BLOCK = 256
PIECE_COLS = 256

TS_MIX = 512
L_CHUNK = 256
TS_ATT = 1024
TM_FFN = 1024
SUB_ROWS = 512
TF_FFN = 256


def _rmsnorm(x, g):
    return x * lax.rsqrt(jnp.mean(x * x, axis=-1, keepdims=True) + EPS) * g


def _sigmoid(x):
    return 1.0 / (1.0 + jnp.exp(-x))


def _log_sigmoid(x):
    return jnp.minimum(x, 0.0) - jnp.log(1.0 + jnp.exp(-jnp.abs(x)))


def _dot(a, b):
    return jnp.dot(a, b, preferred_element_type=F32)


def _dot_nt(a, b):
    return lax.dot_general(a, b, (((1,), (1,)), ((), ())), preferred_element_type=F32)


def _dot_tn(a, b):
    return lax.dot_general(a, b, (((0,), (0,)), ((), ())), preferred_element_type=F32)


def _const_spec(shape):
    nd = len(shape)
    return pl.BlockSpec(shape, lambda *_: (0,) * nd, pipeline_mode=pl.Buffered(1))


def _pool_matrices():
    i = np.arange(POOL_BLOCK)[:, None]
    j = np.arange(POOL_BLOCK)[None, :]
    mats = []
    for win in POOL_WINDOWS:
        inside = (j <= i) & (j > i - win)
        mats.append(inside.astype(np.float32) / win - (i == j).astype(np.float32))
    return jnp.asarray(np.stack(mats), dtype=BF16)


def _mixer_kernel(*refs, nt):
    parity = lax.rem(pl.program_id(0), 2)
    pl.when(parity == 0)(functools.partial(_mixer_step, 1, 0, *refs, nt=nt))
    pl.when(parity == 1)(functools.partial(_mixer_step, 0, 1, *refs, nt=nt))


def _mixer_step(a, b, xn_ref, xb_ref, gpre_ref, win_ref, bias_ref, conv_ref, pmat_ref, wpool_ref,
                pscale_ref, gnorm_ref, wout_ref, gpost_ref, o_ref,
                hb0_ref, hb1_ref, u0_ref, u1_ref, qk0_ref, qk1_ref, v0_ref, v1_ref, og0_ref, og1_ref,
                g0_ref, g1_ref, mix0_ref, mix1_ref, c_ref, m_ref, *, nt):
    hb_a, hb_b = (hb0_ref, hb1_ref)[a], (hb0_ref, hb1_ref)[b]
    uext_a, uext_b = (u0_ref, u1_ref)[a], (u0_ref, u1_ref)[b]
    qkext_a, qkext_b = (qk0_ref, qk1_ref)[a], (qk0_ref, qk1_ref)[b]
    v_a, v_b = (v0_ref, v1_ref)[a], (v0_ref, v1_ref)[b]
    og_a, og_b = (og0_ref, og1_ref)[a], (og0_ref, og1_ref)[b]
    g_a, g_b = (g0_ref, g1_ref)[a], (g0_ref, g1_ref)[b]
    mix_a, mix_b = (mix0_ref, mix1_ref)[a], (mix0_ref, mix1_ref)[b]
    ts = xn_ref.shape[0]
    L = L_CHUNK
    H = MLSTM_HEADS
    DH = MLSTM_HEAD_DIM
    PC = PIECE_COLS
    s = pl.program_id(0)
    first_a = lax.rem(s + nt - 1, nt) == 0
    tb = lax.rem(s + 2 * nt - 2, nt)
    first_b = tb == 0

    @pl.when(s == 0)
    def _():
        for ref in (hb1_ref, u0_ref, qk0_ref, v0_ref, og0_ref, g0_ref, mix1_ref, c_ref, m_ref):
            ref[...] = jnp.zeros_like(ref)

    hb_b[...] = _rmsnorm(xn_ref[...], gpre_ref[...]).astype(BF16)

    out_parts = []

    def out_piece(j):
        def run():
            out_parts.append(_dot(mix_a[...], wout_ref[:, j * PC:(j + 1) * PC]))
            if len(out_parts) == D_MODEL // PC:
                ss = sum(jnp.sum(y * y, axis=1, keepdims=True) for y in out_parts)
                rs = lax.rsqrt(ss * (1.0 / D_MODEL) + EPS)
                for jj, y in enumerate(out_parts):
                    cols = slice(jj * PC, (jj + 1) * PC)
                    o_ref[:, cols] = xb_ref[:, cols] + y * rs * gpost_ref[:, cols]
        return run

    def proj_piece(c0):
        def run():
            y = _dot(hb_a[...], win_ref[:, c0:c0 + min(PC, IN_COLS_PAD - c0)])
            if c0 < OFF_Q:
                uext_a[POOL_HALO:POOL_HALO + ts, c0:c0 + PC] = y
            elif c0 < OFF_V:
                qkext_a[CONV_HALO:CONV_HALO + ts, c0 - OFF_Q:c0 - OFF_Q + PC] = y
            elif c0 < OFF_O:
                v_a[:, c0 - OFF_V:c0 - OFF_V + PC] = y.astype(BF16)
            elif c0 < OFF_GATE:
                og_a[:, c0 - OFF_O:c0 - OFF_O + PC] = y
            else:
                g_a[...] = y + bias_ref[...]
        return run

    pieces = [out_piece(j) for j in range(D_MODEL // PC)] + [proj_piece(c0) for c0 in range(0, IN_COLS_PAD, PC)]
    pieces.reverse()

    def issue(n=1):
        for _ in range(n):
            if pieces:
                pieces.pop()()

    def pool_window():
        d_mains = []
        for blk in range(ts // POOL_BLOCK):
            r0 = blk * POOL_BLOCK
            for g in range(POOL_GROUPS):
                cols = slice(g * POOL_GROUP_DIM, (g + 1) * POOL_GROUP_DIM)
                ublk = uext_b[POOL_HALO + r0:POOL_HALO + r0 + POOL_BLOCK, cols]
                d_mains.append(_dot(pmat_ref[g], ublk.astype(BF16)))
        return d_mains

    def pool_mix(d_mains):
        for blk in range(ts // POOL_BLOCK):
            r0 = blk * POOL_BLOCK
            pos = tb * ts + r0 + lax.broadcasted_iota(jnp.int32, (POOL_HALO, 1), 0)
            for g, win in enumerate(POOL_WINDOWS):
                cols = slice(g * POOL_GROUP_DIM, (g + 1) * POOL_GROUP_DIM)
                cur = uext_b[POOL_HALO + r0:POOL_HALO + r0 + POOL_HALO, cols]
                acc = cur
                for k in range(1, win):
                    acc = acc + uext_b[POOL_HALO + r0 - k:POOL_HALO + r0 - k + POOL_HALO, cols]
                count = jnp.minimum(pos + 1, win).astype(F32)
                d_head = acc / count - cur
                d = jnp.concatenate([d_head, d_mains[blk * POOL_GROUPS + g][POOL_HALO:, :]], axis=0)
                y = _dot(d.astype(BF16), wpool_ref[g]) * pscale_ref[:, cols]
                mix_b[r0:r0 + POOL_BLOCK, cols] = y.astype(BF16)

    ri = lax.broadcasted_iota(jnp.int32, (L, L), 0)
    ci = lax.broadcasted_iota(jnp.int32, (L, L), 1)
    causal = ci <= ri
    lane_x = lax.broadcasted_iota(jnp.int32, (L, V7X_LANES), 1)
    sub8 = lax.broadcasted_iota(jnp.int32, (V7X_SUBLANES, L), 0)
    lane8 = lax.broadcasted_iota(jnp.int32, (V7X_SUBLANES, L), 1)
    zpad = jnp.zeros((V7X_LANES - V7X_SUBLANES, L), F32)
    onehot = [(lane_x == h).astype(BF16) for h in range(H)]
    cw = conv_ref[...]

    cs = [jnp.where(first_b, 0.0, c_ref[h]) for h in range(H)]
    m_row = jnp.where(first_b, 0.0, m_ref[0:1, :])

    def conv_silu(col0, r0):
        acc = None
        for j in range(CONV_WIDTH):
            off = CONV_HALO + r0 - (CONV_WIDTH - 1) + j
            term = cw[j:j + 1, col0:col0 + DH] * qkext_b[off:off + L, col0:col0 + DH]
            acc = term if acc is None else acc + term
        return acc * _sigmoid(acc)

    def lane_scan(x, op, fill):
        sh = 1
        while sh < L:
            x = op(x, jnp.where(lane8 >= sh, pltpu.roll(x, sh, axis=1), fill))
            sh *= 2
        return x

    def gate_prep(r0, m_row):
        G = g_b[r0:r0 + L, :]
        R8 = G.T[0:V7X_SUBLANES, :]
        low = sub8 < H
        B8 = lane_scan(jnp.where(low, _log_sigmoid(R8), 0.0), jnp.add, 0.0)
        C8 = jnp.where(low, pltpu.roll(R8, H, axis=0) - B8, 0.0)
        M8 = lane_scan(C8, jnp.maximum, -jnp.inf)
        T8 = jnp.where(low, B8, pltpu.roll(M8, H, axis=0))
        TX = jnp.concatenate([T8, zpad], axis=0).T
        valid = lane_x < H
        bX = jnp.where(valid, TX, 0.0)
        cmX = jnp.where(valid, pltpu.roll(TX, V7X_LANES - H, axis=1), 0.0)
        igX = jnp.where(valid, pltpu.roll(G, V7X_LANES - H, axis=1), 0.0)
        mmaxX = jnp.maximum(cmX, m_row)
        bL = bX[L - 1:L, :]
        gkX = bL - bX + igX
        m_new = jnp.maximum(bL + m_row, jnp.max(gkX, axis=0, keepdims=True))
        return dict(C8=C8, mmaxX=mmaxX, m_new=m_new,
                    w_interX=jnp.exp(m_row - mmaxX),
                    enegX=jnp.exp(-(bX + mmaxX)),
                    decay=jnp.exp(bL + m_row - m_new),
                    wkX=jnp.exp(gkX - m_new))

    issue()
    gps = [gate_prep(0, m_row)]
    issue()
    d_mains = pool_window()
    gps.append(gate_prep(L, gps[0]["m_new"]))
    issue()
    pool_mix(d_mains)
    issue()

    for c in range(ts // L):
        r0 = c * L
        C8, mmaxX, w_interX, enegX, decay, wkX = (gps[c][n] for n in
                                                 ("C8", "mmaxX", "w_interX", "enegX", "decay", "wkX"))
        per_head = []
        rsX = None
        qnX = None
        for h in range(H):
            hc = h * DH
            q = conv_silu(hc, r0) * (DH ** -0.5)
            k = conv_silu(MLSTM_WIDTH + hc, r0)
            v_aug = jnp.concatenate([v_b[r0:r0 + L, hc:hc + DH], onehot[h]], axis=1)
            qb = q.astype(BF16)
            S = _dot_nt(qb, k.astype(BF16))
            issue()
            E = jnp.exp(jnp.where(causal, C8[h:h + 1, :] - mmaxX[:, h:h + 1], -jnp.inf))
            av = _dot((S * E).astype(BF16), v_aug)
            qc = _dot(qb, cs[h].astype(BF16))
            rsX = av[:, DH:] if rsX is None else rsX + av[:, DH:]
            qnX = qc[:, DH:] if qnX is None else qnX + qc[:, DH:]
            per_head.append((k, v_aug, av[:, :DH], qc[:, :DH]))
            if c > 0 and h % 2 == 1:
                issue()

        denX = rsX + w_interX * qnX
        rX = 1.0 / jnp.maximum(jnp.abs(denX), enegX)

        for h in range(H):
            hc = h * DH
            k, v_aug, av, qc = per_head[h]
            hout = (av + w_interX[:, h:h + 1] * qc) * rX[:, h:h + 1]
            hn = hout * lax.rsqrt(jnp.mean(hout * hout, axis=1, keepdims=True) + EPS)
            og = _sigmoid(og_b[r0:r0 + L, hc:hc + DH])
            mout = hn * gnorm_ref[:, hc:hc + DH] * og
            mix_b[r0:r0 + L, POOL_WIDTH + hc:POOL_WIDTH + hc + DH] = mout.astype(BF16)

            kw = wkX[:, h:h + 1] * k
            cs[h] = decay[:, h:h + 1] * cs[h] + _dot_tn(kw.astype(BF16), v_aug)
            if c == 0 and h == 1:
                issue()
    m_row = gps[-1]["m_new"]
    assert not pieces, "every projection piece must have an issue point"

    uext_a[0:POOL_HALO, :] = jnp.where(first_a, 0.0, uext_b[ts:ts + POOL_HALO, :])
    qkext_a[0:CONV_HALO, :] = jnp.where(first_a, 0.0, qkext_b[ts:ts + CONV_HALO, :])

    for h in range(H):
        c_ref[h] = cs[h]
    m_ref[0:1, :] = m_row


def _mixer(x2, gpre, win, bias, conv, pmat, wpool, pscale, gnorm, wout, gpost, seq):
    T = x2.shape[0]
    ts = TS_MIX
    nt = seq // ts
    ntot = T // ts
    return pl.pallas_call(
        functools.partial(_mixer_kernel, nt=nt),
        grid=(ntot + 3,),
        in_specs=[
            pl.BlockSpec((ts, D_MODEL), lambda s: (jnp.minimum(s, ntot - 1), 0)),
            pl.BlockSpec((ts, D_MODEL), lambda s: (jnp.maximum(s - 3, 0), 0)),
            _const_spec((1, D_MODEL)),
            _const_spec((D_MODEL, IN_COLS_PAD)),
            _const_spec((1, V7X_LANES)),
            _const_spec((CONV_WIDTH, 2 * MLSTM_WIDTH)),
            _const_spec((POOL_GROUPS, POOL_BLOCK, POOL_BLOCK)),
            _const_spec((POOL_GROUPS, POOL_GROUP_DIM, POOL_GROUP_DIM)),
            _const_spec((1, POOL_WIDTH)),
            _const_spec((1, MLSTM_WIDTH)),
            _const_spec((D_MODEL, D_MODEL)),
            _const_spec((1, D_MODEL)),
        ],
        out_specs=pl.BlockSpec((ts, D_MODEL), lambda s: (jnp.maximum(s - 3, 0), 0)),
        out_shape=jax.ShapeDtypeStruct((T, D_MODEL), F32),
        scratch_shapes=[
            *[pltpu.VMEM((ts, D_MODEL), BF16)] * 2,
            *[pltpu.VMEM((ts + POOL_HALO, POOL_WIDTH), F32)] * 2,
            *[pltpu.VMEM((ts + CONV_HALO, 2 * MLSTM_WIDTH), F32)] * 2,
            *[pltpu.VMEM((ts, MLSTM_WIDTH), BF16)] * 2,
            *[pltpu.VMEM((ts, MLSTM_WIDTH), F32)] * 2,
            *[pltpu.VMEM((ts, V7X_LANES), F32)] * 2,
            *[pltpu.VMEM((ts, D_MODEL), BF16)] * 2,
            pltpu.VMEM((MLSTM_HEADS, MLSTM_HEAD_DIM, 2 * MLSTM_HEAD_DIM), F32),
            pltpu.VMEM((V7X_SUBLANES, V7X_LANES), F32),
        ],
        compiler_params=pltpu.CompilerParams(
            dimension_semantics=("arbitrary",), vmem_limit_bytes=V7X_VMEM_LIMIT_BYTES),
        name="mixer",
    )(x2, x2, gpre, win, bias, conv, pmat, wpool, pscale, gnorm, wout, gpost)


def _memkv_kernel(mem_ref, g_ref, wkv_ref, k_ref, v_ref):
    mb = _rmsnorm(mem_ref[0], g_ref[...]).astype(BF16)
    kv = _dot(mb, wkv_ref[...])
    k_ref[0] = kv[:, :D_MODEL].astype(BF16)
    v_ref[0] = kv[:, D_MODEL:].astype(BF16)


def _memkv(mem, g, wkv):
    B = mem.shape[0]
    blk = pl.BlockSpec((1, N_MEM, D_MODEL), lambda b: (b, 0, 0))
    return pl.pallas_call(
        _memkv_kernel,
        grid=(B,),
        in_specs=[blk, _const_spec((1, D_MODEL)), _const_spec((D_MODEL, 2 * D_MODEL))],
        out_specs=[blk, blk],
        out_shape=[jax.ShapeDtypeStruct((B, N_MEM, D_MODEL), BF16)] * 2,
        compiler_params=pltpu.CompilerParams(
            dimension_semantics=("arbitrary",), vmem_limit_bytes=V7X_VMEM_LIMIT_BYTES),
        name="memkv",
    )(mem, g, wkv)


def _xattn_kernel(x_ref, k_ref, v_ref, gpre_ref, wq_ref, wo_ref, gpost_ref, o_ref, att_ref):
    for r0 in range(0, x_ref.shape[0], SUB_ROWS):
        rows = slice(r0, r0 + SUB_ROWS)
        x = x_ref[rows, :]
        hb = _rmsnorm(x, gpre_ref[...]).astype(BF16)
        qb = (_dot(hb, wq_ref[...]) * (XATTN_HEAD_DIM ** -0.5)).astype(BF16)
        for h in range(XATTN_HEADS):
            hc = h * XATTN_HEAD_DIM
            s = _dot_nt(qb[:, hc:hc + XATTN_HEAD_DIM], k_ref[0, :, hc:hc + XATTN_HEAD_DIM])
            p = jnp.exp(s - jnp.max(s, axis=1, keepdims=True))
            o = _dot(p.astype(BF16), v_ref[0, :, hc:hc + XATTN_HEAD_DIM]) / jnp.sum(p, axis=1, keepdims=True)
            att_ref[rows, hc:hc + XATTN_HEAD_DIM] = o.astype(BF16)
        y = _dot(att_ref[rows, :], wo_ref[...])
        o_ref[rows, :] = x + _rmsnorm(y, gpost_ref[...])


def _xattn(x2, k, v, gpre, wq, wo, gpost, batch, seq):
    T = x2.shape[0]
    ts = TS_ATT
    nt = seq // ts
    tok = lambda b, t: (b * nt + t, 0)
    kvb = pl.BlockSpec((1, N_MEM, D_MODEL), lambda b, t: (b, 0, 0))
    return pl.pallas_call(
        _xattn_kernel,
        grid=(batch, nt),
        in_specs=[
            pl.BlockSpec((ts, D_MODEL), tok), kvb, kvb,
            _const_spec((1, D_MODEL)),
            _const_spec((D_MODEL, D_MODEL)),
            _const_spec((D_MODEL, D_MODEL)),
            _const_spec((1, D_MODEL)),
        ],
        out_specs=pl.BlockSpec((ts, D_MODEL), tok),
        out_shape=jax.ShapeDtypeStruct((T, D_MODEL), F32),
        scratch_shapes=[pltpu.VMEM((ts, D_MODEL), BF16)],
        compiler_params=pltpu.CompilerParams(
            dimension_semantics=("arbitrary", "arbitrary"), vmem_limit_bytes=V7X_VMEM_LIMIT_BYTES),
        name="xattn",
    )(x2, k, v, gpre, wq, wo, gpost)


def _ffn_kernel(x_ref, gpre_ref, wg_ref, wu_ref, wd_ref, gpost_ref, o_ref):
    tf = TF_FFN
    for r0 in range(0, x_ref.shape[0], SUB_ROWS):
        rows = slice(r0, r0 + SUB_ROWS)
        x = x_ref[rows, :]
        hb = _rmsnorm(x, gpre_ref[...]).astype(BF16)
        acc = None
        for c in range(wg_ref.shape[1] // tf):
            cols = slice(c * tf, (c + 1) * tf)
            gate = _dot(hb, wg_ref[:, cols])
            a = gate * _sigmoid(gate) * _dot(hb, wu_ref[:, cols])
            contrib = _dot(a.astype(BF16), wd_ref[cols, :])
            acc = contrib if acc is None else acc + contrib
        o_ref[rows, :] = x + _rmsnorm(acc, gpost_ref[...])


def _ffn(x2, gpre, wg, wu, wd, gpost):
    T = x2.shape[0]
    tm = TM_FFN
    d_ff = wg.shape[1]
    return pl.pallas_call(
        _ffn_kernel,
        grid=(T // tm,),
        in_specs=[
            pl.BlockSpec((tm, D_MODEL), lambda i: (i, 0)),
            _const_spec((1, D_MODEL)),
            _const_spec((D_MODEL, d_ff)),
            _const_spec((D_MODEL, d_ff)),
            _const_spec((d_ff, D_MODEL)),
            _const_spec((1, D_MODEL)),
        ],
        out_specs=pl.BlockSpec((tm, D_MODEL), lambda i: (i, 0)),
        out_shape=jax.ShapeDtypeStruct((T, D_MODEL), F32),
        compiler_params=pltpu.CompilerParams(
            dimension_semantics=("arbitrary",), vmem_limit_bytes=V7X_VMEM_LIMIT_BYTES),
        name="ffn",
    )(x2, gpre, wg, wu, wd, gpost)


def kernel(x, mem, w_in, b_gate, conv_qk, w_pool, pool_scale, mlstm_norm_g, w_out, g_mix_pre, g_mix_post, g_mem, g_xattn_pre, g_xattn_post, wq_x, wk_x, wv_x, wo_x, g_ffn_pre, g_ffn_post, w_gate, w_up, w_down):
    B, S, D = x.shape
    depth = w_in.shape[0]
    H = MLSTM_HEADS
    x2 = x.reshape(B * S, D)
    row = lambda v: v.reshape(1, -1).astype(F32)
    pmat = _pool_matrices()

    for l in range(depth):
        win = jnp.concatenate(
            [w_in[l, :, :OFF_GATE], w_in[l, :, OFF_GATE + H:], w_in[l, :, OFF_GATE:OFF_GATE + H],
             jnp.zeros((D, V7X_LANES - 2 * H), w_in.dtype)], axis=1).astype(BF16)
        bias = jnp.concatenate(
            [b_gate[l, H:], b_gate[l, :H], jnp.zeros((V7X_LANES - 2 * H,), b_gate.dtype)]).reshape(1, V7X_LANES)
        x2 = _mixer(x2, row(g_mix_pre[l]), win, bias.astype(F32), conv_qk[l].astype(F32), pmat,
                    w_pool[l].astype(BF16), row(pool_scale[l]), row(mlstm_norm_g[l]),
                    w_out[l].astype(BF16), row(g_mix_post[l]), S)

        wkv = jnp.concatenate([wk_x[l], wv_x[l]], axis=1).astype(BF16)
        k, v = _memkv(mem, row(g_mem[l]), wkv)
        x2 = _xattn(x2, k, v, row(g_xattn_pre[l]), wq_x[l].astype(BF16), wo_x[l].astype(BF16),
                    row(g_xattn_post[l]), B, S)

        x2 = _ffn(x2, row(g_ffn_pre[l]), w_gate[l].astype(BF16), w_up[l].astype(BF16),
                  w_down[l].astype(BF16), row(g_ffn_post[l]))

    return x2.reshape(B, S, D)
```

```python
import functools

import numpy as np
import jax
import jax.numpy as jnp
from jax import lax
from jax.experimental import pallas as pl
from jax.experimental.pallas import tpu as pltpu

F32 = jnp.float32
BF16 = jnp.bfloat16

EPS = 1e-6
D_MODEL = 1024
N_MEM = 256
POOL_WIDTH = 512
POOL_GROUPS = 4
POOL_GROUP_DIM = 128
POOL_WINDOWS = (2, 4, 8, 16)
MLSTM_WIDTH = 512
MLSTM_HEADS = 4
MLSTM_HEAD_DIM = 128
CONV_WIDTH = 4
XATTN_HEADS = 4
XATTN_HEAD_DIM = 256
OFF_Q = POOL_WIDTH
OFF_K = OFF_Q + MLSTM_WIDTH
OFF_V = OFF_K + MLSTM_WIDTH
OFF_O = OFF_V + MLSTM_WIDTH
OFF_GATE = OFF_O + MLSTM_WIDTH

V7X_LANES = 128
V7X_SUBLANES = 8
V7X_VMEM_LIMIT_BYTES = 56 * 1024 * 1024

IN_COLS_PAD = OFF_GATE + V7X_LANES
POOL_HALO = 16
CONV_HALO = 8
POOL_BLOCK = 256
PIECE_COLS = 256

TS_MIX = 512
L_CHUNK = 256
TS_ATT = 1024
TM_FFN = 1024
SUB_ROWS = 512
TF_FFN = 256


def _rmsnorm(x, g):
    return x * lax.rsqrt(jnp.mean(x * x, axis=-1, keepdims=True) + EPS) * g


def _sigmoid(x):
    return 1.0 / (1.0 + jnp.exp(-x))


def _log_sigmoid(x):
    return jnp.minimum(x, 0.0) - jnp.log(1.0 + jnp.exp(-jnp.abs(x)))


def _dot(a, b):
    return jnp.dot(a, b, preferred_element_type=F32)


def _dot_nt(a, b):
    return lax.dot_general(a, b, (((1,), (1,)), ((), ())), preferred_element_type=F32)


def _dot_tn(a, b):
    return lax.dot_general(a, b, (((0,), (0,)), ((), ())), preferred_element_type=F32)


def _const_spec(shape):
    nd = len(shape)
    return pl.BlockSpec(shape, lambda *_: (0,) * nd, pipeline_mode=pl.Buffered(1))


def _layer_spec(shape, l):
    nd = len(shape)
    return pl.BlockSpec((None,) + tuple(shape), lambda *_: (l,) + (0,) * nd, pipeline_mode=pl.Buffered(1))


def _pool_matrices():
    i = np.arange(POOL_BLOCK)[:, None]
    j = np.arange(POOL_BLOCK)[None, :]
    mats = []
    for win in POOL_WINDOWS:
        inside = (j <= i) & (j > i - win)
        mats.append(inside.astype(np.float32) / win - (i == j).astype(np.float32))
    return jnp.asarray(np.stack(mats), dtype=BF16)


def _mixer_kernel(*refs, nt):
    parity = lax.rem(pl.program_id(0), 2)
    pl.when(parity == 0)(functools.partial(_mixer_step, 1, 0, *refs, nt=nt))
    pl.when(parity == 1)(functools.partial(_mixer_step, 0, 1, *refs, nt=nt))


def _mixer_step(a, b, xn_ref, xb_ref, gpre_ref, win_ref, wgate_ref, bias_ref, conv_ref, pmat_ref, wpool_ref,
                pscale_ref, gnorm_ref, wout_ref, gpost_ref, o_ref,
                hb0_ref, hb1_ref, u0_ref, u1_ref, qk0_ref, qk1_ref, v0_ref, v1_ref, og0_ref, og1_ref,
                g0_ref, g1_ref, mix0_ref, mix1_ref, c_ref, m_ref, *, nt):
    hb_a, hb_b = (hb0_ref, hb1_ref)[a], (hb0_ref, hb1_ref)[b]
    uext_a, uext_b = (u0_ref, u1_ref)[a], (u0_ref, u1_ref)[b]
    qkext_a, qkext_b = (qk0_ref, qk1_ref)[a], (qk0_ref, qk1_ref)[b]
    v_a, v_b = (v0_ref, v1_ref)[a], (v0_ref, v1_ref)[b]
    og_a, og_b = (og0_ref, og1_ref)[a], (og0_ref, og1_ref)[b]
    g_a, g_b = (g0_ref, g1_ref)[a], (g0_ref, g1_ref)[b]
    mix_a, mix_b = (mix0_ref, mix1_ref)[a], (mix0_ref, mix1_ref)[b]
    ts = xn_ref.shape[0]
    L = L_CHUNK
    H = MLSTM_HEADS
    DH = MLSTM_HEAD_DIM
    PC = PIECE_COLS
    s = pl.program_id(0)
    first_a = lax.rem(s + nt - 1, nt) == 0
    tb = lax.rem(s + 2 * nt - 2, nt)
    first_b = tb == 0

    @pl.when(s == 0)
    def _():
        for ref in (hb1_ref, u0_ref, qk0_ref, v0_ref, og0_ref, g0_ref, mix1_ref, c_ref, m_ref):
            ref[...] = jnp.zeros_like(ref)

    hb_b[...] = _rmsnorm(xn_ref[...], gpre_ref[...]).astype(BF16)

    out_parts = []

    def out_piece(j):
        def run():
            out_parts.append(_dot(mix_a[...], wout_ref[:, j * PC:(j + 1) * PC]))
            if len(out_parts) == D_MODEL // PC:
                ss = sum(jnp.sum(y * y, axis=1, keepdims=True) for y in out_parts)
                rs = lax.rsqrt(ss * (1.0 / D_MODEL) + EPS)
                for jj, y in enumerate(out_parts):
                    cols = slice(jj * PC, (jj + 1) * PC)
                    o_ref[:, cols] = xb_ref[:, cols] + y * rs * gpost_ref[:, cols]
        return run

    def proj_piece(c0):
        def run():
            w = win_ref[:, c0:c0 + PC] if c0 < OFF_GATE else wgate_ref[...]
            y = _dot(hb_a[...], w)
            if c0 < OFF_Q:
                uext_a[POOL_HALO:POOL_HALO + ts, c0:c0 + PC] = y
            elif c0 < OFF_V:
                qkext_a[CONV_HALO:CONV_HALO + ts, c0 - OFF_Q:c0 - OFF_Q + PC] = y
            elif c0 < OFF_O:
                v_a[:, c0 - OFF_V:c0 - OFF_V + PC] = y.astype(BF16)
            elif c0 < OFF_GATE:
                og_a[:, c0 - OFF_O:c0 - OFF_O + PC] = y
            else:
                g_a[...] = y + bias_ref[...]
        return run

    pieces = [out_piece(j) for j in range(D_MODEL // PC)] + [proj_piece(c0) for c0 in range(0, IN_COLS_PAD, PC)]
    pieces.reverse()

    def issue(n=1):
        for _ in range(n):
            if pieces:
                pieces.pop()()

    def pool_window():
        d_mains = []
        for blk in range(ts // POOL_BLOCK):
            r0 = blk * POOL_BLOCK
            for g in range(POOL_GROUPS):
                cols = slice(g * POOL_GROUP_DIM, (g + 1) * POOL_GROUP_DIM)
                ublk = uext_b[POOL_HALO + r0:POOL_HALO + r0 + POOL_BLOCK, cols]
                d_mains.append(_dot(pmat_ref[g], ublk.astype(BF16)))
        return d_mains

    def pool_mix(d_mains):
        for blk in range(ts // POOL_BLOCK):
            r0 = blk * POOL_BLOCK
            pos = tb * ts + r0 + lax.broadcasted_iota(jnp.int32, (POOL_HALO, 1), 0)
            for g, win in enumerate(POOL_WINDOWS):
                cols = slice(g * POOL_GROUP_DIM, (g + 1) * POOL_GROUP_DIM)
                cur = uext_b[POOL_HALO + r0:POOL_HALO + r0 + POOL_HALO, cols]
                acc = cur
                for k in range(1, win):
                    acc = acc + uext_b[POOL_HALO + r0 - k:POOL_HALO + r0 - k + POOL_HALO, cols]
                count = jnp.minimum(pos + 1, win).astype(F32)
                d_head = acc / count - cur
                d = jnp.concatenate([d_head, d_mains[blk * POOL_GROUPS + g][POOL_HALO:, :]], axis=0)
                y = _dot(d.astype(BF16), wpool_ref[g]) * pscale_ref[:, cols]
                mix_b[r0:r0 + POOL_BLOCK, cols] = y.astype(BF16)

    ri = lax.broadcasted_iota(jnp.int32, (L, L), 0)
    ci = lax.broadcasted_iota(jnp.int32, (L, L), 1)
    causal = ci <= ri
    lane_x = lax.broadcasted_iota(jnp.int32, (L, V7X_LANES), 1)
    sub8 = lax.broadcasted_iota(jnp.int32, (V7X_SUBLANES, L), 0)
    lane8 = lax.broadcasted_iota(jnp.int32, (V7X_SUBLANES, L), 1)
    zpad = jnp.zeros((V7X_LANES - V7X_SUBLANES, L), F32)
    onehot = [(lane_x == h).astype(BF16) for h in range(H)]
    cw = conv_ref[...]

    cs = [jnp.where(first_b, 0.0, c_ref[h]) for h in range(H)]
    m_row = jnp.where(first_b, 0.0, m_ref[0:1, :])

    def conv_silu(col0, r0):
        acc = None
        for j in range(CONV_WIDTH):
            off = CONV_HALO + r0 - (CONV_WIDTH - 1) + j
            term = cw[j:j + 1, col0:col0 + DH] * qkext_b[off:off + L, col0:col0 + DH]
            acc = term if acc is None else acc + term
        return acc * _sigmoid(acc)

    def lane_scan(x, op, fill):
        sh = 1
        while sh < L:
            x = op(x, jnp.where(lane8 >= sh, pltpu.roll(x, sh, axis=1), fill))
            sh *= 2
        return x

    def gate_prep(r0, m_row):
        G = g_b[r0:r0 + L, :]
        R8 = G.T[0:V7X_SUBLANES, :]
        low = sub8 < H
        B8 = lane_scan(jnp.where(low, _log_sigmoid(R8), 0.0), jnp.add, 0.0)
        C8 = jnp.where(low, pltpu.roll(R8, H, axis=0) - B8, 0.0)
        M8 = lane_scan(C8, jnp.maximum, -jnp.inf)
        T8 = jnp.where(low, B8, pltpu.roll(M8, H, axis=0))
        TX = jnp.concatenate([T8, zpad], axis=0).T
        valid = lane_x < H
        bX = jnp.where(valid, TX, 0.0)
        cmX = jnp.where(valid, pltpu.roll(TX, V7X_LANES - H, axis=1), 0.0)
        igX = jnp.where(valid, pltpu.roll(G, V7X_LANES - H, axis=1), 0.0)
        mmaxX = jnp.maximum(cmX, m_row)
        bL = bX[L - 1:L, :]
        gkX = bL - bX + igX
        m_new = jnp.maximum(bL + m_row, jnp.max(gkX, axis=0, keepdims=True))
        return dict(C8=C8, mmaxX=mmaxX, m_new=m_new,
                    w_interX=jnp.exp(m_row - mmaxX),
                    enegX=jnp.exp(-(bX + mmaxX)),
                    decay=jnp.exp(bL + m_row - m_new),
                    wkX=jnp.exp(gkX - m_new))

    issue()
    gps = [gate_prep(0, m_row)]
    issue()
    d_mains = pool_window()
    gps.append(gate_prep(L, gps[0]["m_new"]))
    issue()
    pool_mix(d_mains)
    issue()

    for c in range(ts // L):
        r0 = c * L
        C8, mmaxX, w_interX, enegX, decay, wkX = (gps[c][n] for n in
                                                 ("C8", "mmaxX", "w_interX", "enegX", "decay", "wkX"))
        per_head = []
        rsX = None
        qnX = None
        for h in range(H):
            hc = h * DH
            q = conv_silu(hc, r0) * (DH ** -0.5)
            k = conv_silu(MLSTM_WIDTH + hc, r0)
            v_aug = jnp.concatenate([v_b[r0:r0 + L, hc:hc + DH], onehot[h]], axis=1)
            qb = q.astype(BF16)
            S = _dot_nt(qb, k.astype(BF16))
            issue()
            E = jnp.exp(jnp.where(causal, C8[h:h + 1, :] - mmaxX[:, h:h + 1], -jnp.inf))
            av = _dot((S * E).astype(BF16), v_aug)
            qc = _dot(qb, cs[h].astype(BF16))
            rsX = av[:, DH:] if rsX is None else rsX + av[:, DH:]
            qnX = qc[:, DH:] if qnX is None else qnX + qc[:, DH:]
            per_head.append((k, v_aug, av[:, :DH], qc[:, :DH]))
            if c > 0 and h % 2 == 1:
                issue()

        denX = rsX + w_interX * qnX
        rX = 1.0 / jnp.maximum(jnp.abs(denX), enegX)

        for h in range(H):
            hc = h * DH
            k, v_aug, av, qc = per_head[h]
            hout = (av + w_interX[:, h:h + 1] * qc) * rX[:, h:h + 1]
            hn = hout * lax.rsqrt(jnp.mean(hout * hout, axis=1, keepdims=True) + EPS)
            og = _sigmoid(og_b[r0:r0 + L, hc:hc + DH])
            mout = hn * gnorm_ref[:, hc:hc + DH] * og
            mix_b[r0:r0 + L, POOL_WIDTH + hc:POOL_WIDTH + hc + DH] = mout.astype(BF16)

            kw = wkX[:, h:h + 1] * k
            cs[h] = decay[:, h:h + 1] * cs[h] + _dot_tn(kw.astype(BF16), v_aug)
            if c == 0 and h == 1:
                issue()
    m_row = gps[-1]["m_new"]
    assert not pieces, "every projection piece must have an issue point"

    uext_a[0:POOL_HALO, :] = jnp.where(first_a, 0.0, uext_b[ts:ts + POOL_HALO, :])
    qkext_a[0:CONV_HALO, :] = jnp.where(first_a, 0.0, qkext_b[ts:ts + CONV_HALO, :])

    for h in range(H):
        c_ref[h] = cs[h]
    m_ref[0:1, :] = m_row


def _mixer(l, x2, gpre, win, wgate, bias, conv, pmat, wpool, pscale, gnorm, wout, gpost, seq):
    T = x2.shape[0]
    ts = TS_MIX
    nt = seq // ts
    ntot = T // ts
    return pl.pallas_call(
        functools.partial(_mixer_kernel, nt=nt),
        grid=(ntot + 3,),
        in_specs=[
            pl.BlockSpec((ts, D_MODEL), lambda s: (jnp.minimum(s, ntot - 1), 0)),
            pl.BlockSpec((ts, D_MODEL), lambda s: (jnp.maximum(s - 3, 0), 0)),
            _layer_spec((1, D_MODEL), l),
            _layer_spec((D_MODEL, OFF_GATE), l),
            _layer_spec((D_MODEL, V7X_LANES), l),
            _layer_spec((1, V7X_LANES), l),
            _layer_spec((CONV_WIDTH, 2 * MLSTM_WIDTH), l),
            _const_spec((POOL_GROUPS, POOL_BLOCK, POOL_BLOCK)),
            _layer_spec((POOL_GROUPS, POOL_GROUP_DIM, POOL_GROUP_DIM), l),
            _layer_spec((1, POOL_WIDTH), l),
            _layer_spec((1, MLSTM_WIDTH), l),
            _layer_spec((D_MODEL, D_MODEL), l),
            _layer_spec((1, D_MODEL), l),
        ],
        out_specs=pl.BlockSpec((ts, D_MODEL), lambda s: (jnp.maximum(s - 3, 0), 0)),
        out_shape=jax.ShapeDtypeStruct((T, D_MODEL), F32),
        scratch_shapes=[
            *[pltpu.VMEM((ts, D_MODEL), BF16)] * 2,
            *[pltpu.VMEM((ts + POOL_HALO, POOL_WIDTH), F32)] * 2,
            *[pltpu.VMEM((ts + CONV_HALO, 2 * MLSTM_WIDTH), F32)] * 2,
            *[pltpu.VMEM((ts, MLSTM_WIDTH), BF16)] * 2,
            *[pltpu.VMEM((ts, MLSTM_WIDTH), F32)] * 2,
            *[pltpu.VMEM((ts, V7X_LANES), F32)] * 2,
            *[pltpu.VMEM((ts, D_MODEL), BF16)] * 2,
            pltpu.VMEM((MLSTM_HEADS, MLSTM_HEAD_DIM, 2 * MLSTM_HEAD_DIM), F32),
            pltpu.VMEM((V7X_SUBLANES, V7X_LANES), F32),
        ],
        compiler_params=pltpu.CompilerParams(
            dimension_semantics=("arbitrary",), vmem_limit_bytes=V7X_VMEM_LIMIT_BYTES),
        name="mixer",
    )(x2, x2, gpre, win, wgate, bias, conv, pmat, wpool, pscale, gnorm, wout, gpost)


def _memkv_kernel(mem_ref, g_ref, wk_ref, wv_ref, k_ref, v_ref):
    mb = _rmsnorm(mem_ref[0], g_ref[...]).astype(BF16)
    k_ref[0] = _dot(mb, wk_ref[...]).astype(BF16)
    v_ref[0] = _dot(mb, wv_ref[...]).astype(BF16)


def _memkv(l, mem, g, wk, wv):
    B = mem.shape[0]
    blk = pl.BlockSpec((1, N_MEM, D_MODEL), lambda b: (b, 0, 0))
    return pl.pallas_call(
        _memkv_kernel,
        grid=(B,),
        in_specs=[blk, _layer_spec((1, D_MODEL), l), _layer_spec((D_MODEL, D_MODEL), l),
                  _layer_spec((D_MODEL, D_MODEL), l)],
        out_specs=[blk, blk],
        out_shape=[jax.ShapeDtypeStruct((B, N_MEM, D_MODEL), BF16)] * 2,
        compiler_params=pltpu.CompilerParams(
            dimension_semantics=("arbitrary",), vmem_limit_bytes=V7X_VMEM_LIMIT_BYTES),
        name="memkv",
    )(mem, g, wk, wv)


def _xattn_kernel(x_ref, k_ref, v_ref, gpre_ref, wq_ref, wo_ref, gpost_ref, o_ref, att_ref):
    for r0 in range(0, x_ref.shape[0], SUB_ROWS):
        rows = slice(r0, r0 + SUB_ROWS)
        x = x_ref[rows, :]
        hb = _rmsnorm(x, gpre_ref[...]).astype(BF16)
        qb = (_dot(hb, wq_ref[...]) * (XATTN_HEAD_DIM ** -0.5)).astype(BF16)
        for h in range(XATTN_HEADS):
            hc = h * XATTN_HEAD_DIM
            s = _dot_nt(qb[:, hc:hc + XATTN_HEAD_DIM], k_ref[0, :, hc:hc + XATTN_HEAD_DIM])
            p = jnp.exp(s - jnp.max(s, axis=1, keepdims=True))
            o = _dot(p.astype(BF16), v_ref[0, :, hc:hc + XATTN_HEAD_DIM]) / jnp.sum(p, axis=1, keepdims=True)
            att_ref[rows, hc:hc + XATTN_HEAD_DIM] = o.astype(BF16)
        y = _dot(att_ref[rows, :], wo_ref[...])
        o_ref[rows, :] = x + _rmsnorm(y, gpost_ref[...])


def _xattn(l, x2, k, v, gpre, wq, wo, gpost, batch, seq):
    T = x2.shape[0]
    ts = TS_ATT
    nt = seq // ts
    tok = lambda b, t: (b * nt + t, 0)
    kvb = pl.BlockSpec((1, N_MEM, D_MODEL), lambda b, t: (b, 0, 0))
    return pl.pallas_call(
        _xattn_kernel,
        grid=(batch, nt),
        in_specs=[
            pl.BlockSpec((ts, D_MODEL), tok), kvb, kvb,
            _layer_spec((1, D_MODEL), l),
            _layer_spec((D_MODEL, D_MODEL), l),
            _layer_spec((D_MODEL, D_MODEL), l),
            _layer_spec((1, D_MODEL), l),
        ],
        out_specs=pl.BlockSpec((ts, D_MODEL), tok),
        out_shape=jax.ShapeDtypeStruct((T, D_MODEL), F32),
        scratch_shapes=[pltpu.VMEM((ts, D_MODEL), BF16)],
        compiler_params=pltpu.CompilerParams(
            dimension_semantics=("arbitrary", "arbitrary"), vmem_limit_bytes=V7X_VMEM_LIMIT_BYTES),
        name="xattn",
    )(x2, k, v, gpre, wq, wo, gpost)


def _ffn_kernel(x_ref, gpre_ref, wg_ref, wu_ref, wd_ref, gpost_ref, o_ref):
    tf = TF_FFN
    for r0 in range(0, x_ref.shape[0], SUB_ROWS):
        rows = slice(r0, r0 + SUB_ROWS)
        x = x_ref[rows, :]
        hb = _rmsnorm(x, gpre_ref[...]).astype(BF16)
        acc = None
        for c in range(wg_ref.shape[1] // tf):
            cols = slice(c * tf, (c + 1) * tf)
            gate = _dot(hb, wg_ref[:, cols])
            a = gate * _sigmoid(gate) * _dot(hb, wu_ref[:, cols])
            contrib = _dot(a.astype(BF16), wd_ref[cols, :])
            acc = contrib if acc is None else acc + contrib
        o_ref[rows, :] = x + _rmsnorm(acc, gpost_ref[...])


def _ffn(l, x2, gpre, wg, wu, wd, gpost):
    T = x2.shape[0]
    tm = TM_FFN
    d_ff = wg.shape[-1]
    return pl.pallas_call(
        _ffn_kernel,
        grid=(T // tm,),
        in_specs=[
            pl.BlockSpec((tm, D_MODEL), lambda i: (i, 0)),
            _layer_spec((1, D_MODEL), l),
            _layer_spec((D_MODEL, d_ff), l),
            _layer_spec((D_MODEL, d_ff), l),
            _layer_spec((d_ff, D_MODEL), l),
            _layer_spec((1, D_MODEL), l),
        ],
        out_specs=pl.BlockSpec((tm, D_MODEL), lambda i: (i, 0)),
        out_shape=jax.ShapeDtypeStruct((T, D_MODEL), F32),
        compiler_params=pltpu.CompilerParams(
            dimension_semantics=("arbitrary",), vmem_limit_bytes=V7X_VMEM_LIMIT_BYTES),
        name="ffn",
    )(x2, gpre, wg, wu, wd, gpost)


def kernel(x, mem, w_in, b_gate, conv_qk, w_pool, pool_scale, mlstm_norm_g, w_out, g_mix_pre, g_mix_post, g_mem, g_xattn_pre, g_xattn_post, wq_x, wk_x, wv_x, wo_x, g_ffn_pre, g_ffn_post, w_gate, w_up, w_down):
    B, S, D = x.shape
    depth = w_in.shape[0]
    H = MLSTM_HEADS
    x2 = x.reshape(B * S, D)
    pmat = _pool_matrices()
    rows = lambda v: v.reshape(depth, 1, -1).astype(F32)
    bf = lambda w: w.astype(BF16)
    w_in_b = bf(w_in)
    wgate = jnp.concatenate(
        [w_in_b[:, :, OFF_GATE + H:], w_in_b[:, :, OFF_GATE:OFF_GATE + H],
         jnp.zeros((depth, D, V7X_LANES - 2 * H), BF16)], axis=2)
    bias = jnp.concatenate(
        [b_gate[:, H:], b_gate[:, :H], jnp.zeros((depth, V7X_LANES - 2 * H), b_gate.dtype)], axis=1)
    mixer_params = (rows(g_mix_pre), w_in_b, wgate, rows(bias), conv_qk.astype(F32), pmat, bf(w_pool),
                    rows(pool_scale), rows(mlstm_norm_g), bf(w_out), rows(g_mix_post))
    memkv_params = (rows(g_mem), bf(wk_x), bf(wv_x))
    xattn_params = (rows(g_xattn_pre), bf(wq_x), bf(wo_x), rows(g_xattn_post))
    ffn_params = (rows(g_ffn_pre), bf(w_gate), bf(w_up), bf(w_down), rows(g_ffn_post))

    for l in range(depth):
        x2 = _mixer(l, x2, *mixer_params, S)
        k, v = _memkv(l, mem, *memkv_params)
        x2 = _xattn(l, x2, k, v, *xattn_params, B, S)
        x2 = _ffn(l, x2, *ffn_params)

    return x2.reshape(B, S, D)
```

```python
import functools

import numpy as np
import jax
import jax.numpy as jnp
from jax import lax
from jax.experimental import pallas as pl
from jax.experimental.pallas import tpu as pltpu

F32 = jnp.float32
BF16 = jnp.bfloat16

EPS = 1e-6
D_MODEL = 1024
N_MEM = 256
POOL_WIDTH = 512
POOL_GROUPS = 4
POOL_GROUP_DIM = 128
POOL_WINDOWS = (2, 4, 8, 16)
MLSTM_WIDTH = 512
MLSTM_HEADS = 4
MLSTM_HEAD_DIM = 128
CONV_WIDTH = 4
XATTN_HEADS = 4
XATTN_HEAD_DIM = 256
OFF_Q = POOL_WIDTH
OFF_K = OFF_Q + MLSTM_WIDTH
OFF_V = OFF_K + MLSTM_WIDTH
OFF_O = OFF_V + MLSTM_WIDTH
OFF_GATE = OFF_O + MLSTM_WIDTH

V7X_LANES = 128
V7X_SUBLANES = 8
V7X_VMEM_LIMIT_BYTES = 56 * 1024 * 1024

G_MIX_PRE, G_MIX_POST, G_MEM, G_XATTN_PRE, G_XATTN_POST, G_FFN_PRE, G_FFN_POST = range(7)
V_POOL_SCALE, V_MLSTM_NORM = range(2)

IN_COLS_PAD = OFF_GATE + V7X_LANES
POOL_HALO = 16
CONV_HALO = 8
POOL_BLOCK = 256
PIECE_COLS = 256

TS_MIX = 512
L_CHUNK = 256
TS_ATT = 2048
TM_FFN = 1024
SUB_ROWS = 512
TF_FFN = 256


def _rmsnorm(x, g):
    return x * lax.rsqrt(jnp.mean(x * x, axis=-1, keepdims=True) + EPS) * g


def _sigmoid(x):
    return 1.0 / (1.0 + jnp.exp(-x))


def _log_sigmoid(x):
    return jnp.minimum(x, 0.0) - jnp.log(1.0 + jnp.exp(-jnp.abs(x)))


def _dot(a, b):
    return jnp.dot(a, b, preferred_element_type=F32)


def _dot_nt(a, b):
    return lax.dot_general(a, b, (((1,), (1,)), ((), ())), preferred_element_type=F32)


def _dot_tn(a, b):
    return lax.dot_general(a, b, (((0,), (0,)), ((), ())), preferred_element_type=F32)


def _const_spec(shape):
    nd = len(shape)
    return pl.BlockSpec(shape, lambda *_: (0,) * nd, pipeline_mode=pl.Buffered(1))


def _vector_spec(n, l, k):
    return pl.BlockSpec((None, None, 1, n), lambda *_: (l, k, 0, 0), pipeline_mode=pl.Buffered(1))


def _layer_spec(shape, l):
    nd = len(shape)
    return pl.BlockSpec((None,) + tuple(shape), lambda *_: (l,) + (0,) * nd, pipeline_mode=pl.Buffered(1))


def _pool_matrices():
    i = np.arange(POOL_BLOCK)[:, None]
    j = np.arange(POOL_BLOCK)[None, :]
    mats = []
    for win in POOL_WINDOWS:
        inside = (j <= i) & (j > i - win)
        mats.append(inside.astype(np.float32) / win - (i == j).astype(np.float32))
    return jnp.asarray(np.stack(mats), dtype=BF16)


def _mixer_kernel(*refs, nt):
    parity = lax.rem(pl.program_id(0), 2)
    pl.when(parity == 0)(functools.partial(_mixer_step, 0, 1, *refs, nt=nt))
    pl.when(parity == 1)(functools.partial(_mixer_step, 1, 0, *refs, nt=nt))


def _mixer_step(a, b, xn_ref, xb_ref, gpre_ref, win_ref, wgate_ref, bias_ref, conv_ref, pmat_ref, wpool_ref,
                pscale_ref, gnorm_ref, wout_ref, gpost_ref, o_ref,
                u0_ref, u1_ref, qk0_ref, qk1_ref, v0_ref, v1_ref, og0_ref, og1_ref,
                g0_ref, g1_ref, mix0_ref, mix1_ref, c_ref, m_ref, *, nt):
    uext_a, uext_b = (u0_ref, u1_ref)[a], (u0_ref, u1_ref)[b]
    qkext_a, qkext_b = (qk0_ref, qk1_ref)[a], (qk0_ref, qk1_ref)[b]
    v_a, v_b = (v0_ref, v1_ref)[a], (v0_ref, v1_ref)[b]
    og_a, og_b = (og0_ref, og1_ref)[a], (og0_ref, og1_ref)[b]
    g_a, g_b = (g0_ref, g1_ref)[a], (g0_ref, g1_ref)[b]
    mix_a, mix_b = (mix0_ref, mix1_ref)[a], (mix0_ref, mix1_ref)[b]
    ts = xn_ref.shape[0]
    L = L_CHUNK
    H = MLSTM_HEADS
    DH = MLSTM_HEAD_DIM
    PC = PIECE_COLS
    s = pl.program_id(0)
    first_a = lax.rem(s, nt) == 0
    tb = lax.rem(s + nt - 1, nt)
    first_b = tb == 0

    @pl.when(s == 0)
    def _():
        for ref in (u1_ref, qk1_ref, v1_ref, og1_ref, g1_ref, mix0_ref, c_ref, m_ref):
            ref[...] = jnp.zeros_like(ref)

    hb = _rmsnorm(xn_ref[...], gpre_ref[...]).astype(BF16)

    out_parts = []

    def out_piece(j):
        def run():
            out_parts.append(_dot(mix_a[...], wout_ref[:, j * PC:(j + 1) * PC]))
            if len(out_parts) == D_MODEL // PC:
                ss = sum(jnp.sum(y * y, axis=1, keepdims=True) for y in out_parts)
                rs = lax.rsqrt(ss * (1.0 / D_MODEL) + EPS)
                for jj, y in enumerate(out_parts):
                    cols = slice(jj * PC, (jj + 1) * PC)
                    o_ref[:, cols] = xb_ref[:, cols] + y * rs * gpost_ref[:, cols]
        return run

    def proj_piece(c0):
        def run():
            w = win_ref[:, c0:c0 + PC] if c0 < OFF_GATE else wgate_ref[...]
            y = _dot(hb, w)
            if c0 < OFF_Q:
                uext_a[POOL_HALO:POOL_HALO + ts, c0:c0 + PC] = y
            elif c0 < OFF_V:
                qkext_a[CONV_HALO:CONV_HALO + ts, c0 - OFF_Q:c0 - OFF_Q + PC] = y
            elif c0 < OFF_O:
                v_a[:, c0 - OFF_V:c0 - OFF_V + PC] = y.astype(BF16)
            elif c0 < OFF_GATE:
                og_a[:, c0 - OFF_O:c0 - OFF_O + PC] = y
            else:
                g_a[...] = y + bias_ref[...]
        return run

    pieces = [out_piece(j) for j in range(D_MODEL // PC)] + [proj_piece(c0) for c0 in range(0, IN_COLS_PAD, PC)]
    pieces.reverse()

    def issue(n=1):
        for _ in range(n):
            if pieces:
                pieces.pop()()

    def pool_window():
        d_mains = []
        for blk in range(ts // POOL_BLOCK):
            r0 = blk * POOL_BLOCK
            for g in range(POOL_GROUPS):
                cols = slice(g * POOL_GROUP_DIM, (g + 1) * POOL_GROUP_DIM)
                ublk = uext_b[POOL_HALO + r0:POOL_HALO + r0 + POOL_BLOCK, cols]
                d_mains.append(_dot(pmat_ref[g], ublk.astype(BF16)))
        return d_mains

    def pool_mix(d_mains):
        for blk in range(ts // POOL_BLOCK):
            r0 = blk * POOL_BLOCK
            pos = tb * ts + r0 + lax.broadcasted_iota(jnp.int32, (POOL_HALO, 1), 0)
            for g, win in enumerate(POOL_WINDOWS):
                cols = slice(g * POOL_GROUP_DIM, (g + 1) * POOL_GROUP_DIM)
                cur = uext_b[POOL_HALO + r0:POOL_HALO + r0 + POOL_HALO, cols]
                acc = cur
                for k in range(1, win):
                    acc = acc + uext_b[POOL_HALO + r0 - k:POOL_HALO + r0 - k + POOL_HALO, cols]
                count = jnp.minimum(pos + 1, win).astype(F32)
                d_head = acc / count - cur
                d = jnp.concatenate([d_head, d_mains[blk * POOL_GROUPS + g][POOL_HALO:, :]], axis=0)
                y = _dot(d.astype(BF16), wpool_ref[g]) * pscale_ref[:, cols]
                mix_b[r0:r0 + POOL_BLOCK, cols] = y.astype(BF16)

    ri = lax.broadcasted_iota(jnp.int32, (L, L), 0)
    ci = lax.broadcasted_iota(jnp.int32, (L, L), 1)
    causal = ci <= ri
    lane_x = lax.broadcasted_iota(jnp.int32, (L, V7X_LANES), 1)
    sub8 = lax.broadcasted_iota(jnp.int32, (V7X_SUBLANES, L), 0)
    lane8 = lax.broadcasted_iota(jnp.int32, (V7X_SUBLANES, L), 1)
    zpad = jnp.zeros((V7X_LANES - V7X_SUBLANES, L), F32)
    onehot = [(lane_x == h).astype(BF16) for h in range(H)]
    cw = conv_ref[...]

    cs = [jnp.where(first_b, 0.0, c_ref[h]) for h in range(H)]
    m_row = jnp.where(first_b, 0.0, m_ref[0:1, :])

    def conv_silu(col0, r0):
        acc = None
        for j in range(CONV_WIDTH):
            off = CONV_HALO + r0 - (CONV_WIDTH - 1) + j
            term = cw[j:j + 1, col0:col0 + DH] * qkext_b[off:off + L, col0:col0 + DH]
            acc = term if acc is None else acc + term
        return acc * _sigmoid(acc)

    def lane_scan(x, op, fill):
        sh = 1
        while sh < L:
            x = op(x, jnp.where(lane8 >= sh, pltpu.roll(x, sh, axis=1), fill))
            sh *= 2
        return x

    def gate_prep(r0, m_row):
        G = g_b[r0:r0 + L, :]
        R8 = G.T[0:V7X_SUBLANES, :]
        low = sub8 < H
        B8 = lane_scan(jnp.where(low, _log_sigmoid(R8), 0.0), jnp.add, 0.0)
        C8 = jnp.where(low, pltpu.roll(R8, H, axis=0) - B8, 0.0)
        M8 = lane_scan(C8, jnp.maximum, -jnp.inf)
        T8 = jnp.where(low, B8, pltpu.roll(M8, H, axis=0))
        TX = jnp.concatenate([T8, zpad], axis=0).T
        valid = lane_x < H
        bX = jnp.where(valid, TX, 0.0)
        cmX = jnp.where(valid, pltpu.roll(TX, V7X_LANES - H, axis=1), 0.0)
        igX = jnp.where(valid, pltpu.roll(G, V7X_LANES - H, axis=1), 0.0)
        mmaxX = jnp.maximum(cmX, m_row)
        bL = bX[L - 1:L, :]
        gkX = bL - bX + igX
        m_new = jnp.maximum(bL + m_row, jnp.max(gkX, axis=0, keepdims=True))
        return dict(C8=C8, mmaxX=mmaxX, m_new=m_new,
                    w_interX=jnp.exp(m_row - mmaxX),
                    enegX=jnp.exp(-(bX + mmaxX)),
                    decay=jnp.exp(bL + m_row - m_new),
                    wkX=jnp.exp(gkX - m_new))

    issue()
    gps = [gate_prep(0, m_row)]
    issue()
    d_mains = pool_window()
    gps.append(gate_prep(L, gps[0]["m_new"]))
    issue()
    pool_mix(d_mains)
    issue()

    for c in range(ts // L):
        r0 = c * L
        C8, mmaxX, w_interX, enegX, decay, wkX = (gps[c][n] for n in
                                                 ("C8", "mmaxX", "w_interX", "enegX", "decay", "wkX"))
        per_head = []
        rsX = None
        qnX = None
        for h in range(H):
            hc = h * DH
            q = conv_silu(hc, r0) * (DH ** -0.5)
            k = conv_silu(MLSTM_WIDTH + hc, r0)
            v_aug = jnp.concatenate([v_b[r0:r0 + L, hc:hc + DH], onehot[h]], axis=1)
            qb = q.astype(BF16)
            S = _dot_nt(qb, k.astype(BF16))
            issue()
            E = jnp.exp(jnp.where(causal, C8[h:h + 1, :] - mmaxX[:, h:h + 1], -jnp.inf))
            av = _dot((S * E).astype(BF16), v_aug)
            qc = _dot(qb, cs[h].astype(BF16))
            rsX = av[:, DH:] if rsX is None else rsX + av[:, DH:]
            qnX = qc[:, DH:] if qnX is None else qnX + qc[:, DH:]
            per_head.append((k, v_aug, av[:, :DH], qc[:, :DH]))
            if c > 0 and h % 2 == 1:
                issue()

        denX = rsX + w_interX * qnX
        rX = 1.0 / jnp.maximum(jnp.abs(denX), enegX)

        for h in range(H):
            hc = h * DH
            k, v_aug, av, qc = per_head[h]
            hout = (av + w_interX[:, h:h + 1] * qc) * rX[:, h:h + 1]
            hn = hout * lax.rsqrt(jnp.mean(hout * hout, axis=1, keepdims=True) + EPS)
            og = _sigmoid(og_b[r0:r0 + L, hc:hc + DH])
            mout = hn * gnorm_ref[:, hc:hc + DH] * og
            mix_b[r0:r0 + L, POOL_WIDTH + hc:POOL_WIDTH + hc + DH] = mout.astype(BF16)

            kw = wkX[:, h:h + 1] * k
            cs[h] = decay[:, h:h + 1] * cs[h] + _dot_tn(kw.astype(BF16), v_aug)
            if c == 0 and h == 1:
                issue()
    m_row = gps[-1]["m_new"]
    assert not pieces, "every projection piece must have an issue point"

    uext_a[0:POOL_HALO, :] = jnp.where(first_a, 0.0, uext_b[ts:ts + POOL_HALO, :])
    qkext_a[0:CONV_HALO, :] = jnp.where(first_a, 0.0, qkext_b[ts:ts + CONV_HALO, :])

    for h in range(H):
        c_ref[h] = cs[h]
    m_ref[0:1, :] = m_row


def _mixer(l, x2, gpre, win, wgate, bias, conv, pmat, wpool, pscale, gnorm, wout, gpost, seq):
    T = x2.shape[0]
    ts = TS_MIX
    nt = seq // ts
    ntot = T // ts
    return pl.pallas_call(
        functools.partial(_mixer_kernel, nt=nt),
        grid=(ntot + 2,),
        in_specs=[
            pl.BlockSpec((ts, D_MODEL), lambda s: (jnp.minimum(s, ntot - 1), 0)),
            pl.BlockSpec((ts, D_MODEL), lambda s: (jnp.maximum(s - 2, 0), 0)),
            _vector_spec(D_MODEL, l, G_MIX_PRE),
            _layer_spec((D_MODEL, OFF_GATE), l),
            _layer_spec((D_MODEL, V7X_LANES), l),
            _layer_spec((1, V7X_LANES), l),
            _layer_spec((CONV_WIDTH, 2 * MLSTM_WIDTH), l),
            _const_spec((POOL_GROUPS, POOL_BLOCK, POOL_BLOCK)),
            _layer_spec((POOL_GROUPS, POOL_GROUP_DIM, POOL_GROUP_DIM), l),
            _vector_spec(POOL_WIDTH, l, V_POOL_SCALE),
            _vector_spec(MLSTM_WIDTH, l, V_MLSTM_NORM),
            _layer_spec((D_MODEL, D_MODEL), l),
            _vector_spec(D_MODEL, l, G_MIX_POST),
        ],
        out_specs=pl.BlockSpec((ts, D_MODEL), lambda s: (jnp.maximum(s - 2, 0), 0)),
        out_shape=jax.ShapeDtypeStruct((T, D_MODEL), F32),
        scratch_shapes=[
            *[pltpu.VMEM((ts + POOL_HALO, POOL_WIDTH), F32)] * 2,
            *[pltpu.VMEM((ts + CONV_HALO, 2 * MLSTM_WIDTH), F32)] * 2,
            *[pltpu.VMEM((ts, MLSTM_WIDTH), BF16)] * 2,
            *[pltpu.VMEM((ts, MLSTM_WIDTH), F32)] * 2,
            *[pltpu.VMEM((ts, V7X_LANES), F32)] * 2,
            *[pltpu.VMEM((ts, D_MODEL), BF16)] * 2,
            pltpu.VMEM((MLSTM_HEADS, MLSTM_HEAD_DIM, 2 * MLSTM_HEAD_DIM), F32),
            pltpu.VMEM((V7X_SUBLANES, V7X_LANES), F32),
        ],
        compiler_params=pltpu.CompilerParams(
            dimension_semantics=("arbitrary",), vmem_limit_bytes=V7X_VMEM_LIMIT_BYTES),
        name="mixer",
    )(x2, x2, gpre, win, wgate, bias, conv, pmat, wpool, pscale, gnorm, wout, gpost)


def _memkv_kernel(mem_ref, g_ref, wk_ref, wv_ref, k_ref, v_ref):
    mb = _rmsnorm(mem_ref[0], g_ref[...]).astype(BF16)
    k_ref[0] = _dot(mb, wk_ref[...]).astype(BF16)
    v_ref[0] = _dot(mb, wv_ref[...]).astype(BF16)


def _memkv(l, mem, g, wk, wv):
    B = mem.shape[0]
    blk = pl.BlockSpec((1, N_MEM, D_MODEL), lambda b: (b, 0, 0))
    return pl.pallas_call(
        _memkv_kernel,
        grid=(B,),
        in_specs=[blk, _vector_spec(D_MODEL, l, G_MEM), _layer_spec((D_MODEL, D_MODEL), l),
                  _layer_spec((D_MODEL, D_MODEL), l)],
        out_specs=[blk, blk],
        out_shape=[jax.ShapeDtypeStruct((B, N_MEM, D_MODEL), BF16)] * 2,
        compiler_params=pltpu.CompilerParams(
            dimension_semantics=("arbitrary",), vmem_limit_bytes=V7X_VMEM_LIMIT_BYTES),
        name="memkv",
    )(mem, g, wk, wv)


def _xattn_kernel(x_ref, k_ref, v_ref, gpre_ref, wq_ref, wo_ref, gpost_ref, o_ref, att_ref):
    halves = [slice(r0, r0 + SUB_ROWS) for r0 in range(0, x_ref.shape[0], SUB_ROWS)]
    qbs = []
    for rows in halves:
        hb = _rmsnorm(x_ref[rows, :], gpre_ref[...]).astype(BF16)
        qbs.append((_dot(hb, wq_ref[...]) * (XATTN_HEAD_DIM ** -0.5)).astype(BF16))
    for h in range(XATTN_HEADS):
        cols = slice(h * XATTN_HEAD_DIM, (h + 1) * XATTN_HEAD_DIM)
        scores = [_dot_nt(qb[:, cols], k_ref[0, :, cols]) for qb in qbs]
        for rows, s in zip(halves, scores):
            p = jnp.exp(s - jnp.max(s, axis=1, keepdims=True))
            o = _dot(p.astype(BF16), v_ref[0, :, cols]) / jnp.sum(p, axis=1, keepdims=True)
            att_ref[rows, cols] = o.astype(BF16)
    for rows in halves:
        y = _dot(att_ref[rows, :], wo_ref[...])
        o_ref[rows, :] = x_ref[rows, :] + _rmsnorm(y, gpost_ref[...])


def _xattn(l, x2, k, v, gpre, wq, wo, gpost, batch, seq):
    T = x2.shape[0]
    ts = TS_ATT
    nt = seq // ts
    tok = lambda b, t: (b * nt + t, 0)
    kvb = pl.BlockSpec((1, N_MEM, D_MODEL), lambda b, t: (b, 0, 0))
    return pl.pallas_call(
        _xattn_kernel,
        grid=(batch, nt),
        in_specs=[
            pl.BlockSpec((ts, D_MODEL), tok), kvb, kvb,
            _vector_spec(D_MODEL, l, G_XATTN_PRE),
            _layer_spec((D_MODEL, D_MODEL), l),
            _layer_spec((D_MODEL, D_MODEL), l),
            _vector_spec(D_MODEL, l, G_XATTN_POST),
        ],
        out_specs=pl.BlockSpec((ts, D_MODEL), tok),
        out_shape=jax.ShapeDtypeStruct((T, D_MODEL), F32),
        scratch_shapes=[pltpu.VMEM((ts, D_MODEL), BF16)],
        compiler_params=pltpu.CompilerParams(
            dimension_semantics=("arbitrary", "arbitrary"), vmem_limit_bytes=V7X_VMEM_LIMIT_BYTES),
        name="xattn",
    )(x2, k, v, gpre, wq, wo, gpost)


def _ffn_kernel(x_ref, gpre_ref, wg_ref, wu_ref, wd_ref, gpost_ref, o_ref):
    tf = TF_FFN
    halves = [slice(r0, r0 + SUB_ROWS) for r0 in range(0, x_ref.shape[0], SUB_ROWS)]
    hbs = [_rmsnorm(x_ref[rows, :], gpre_ref[...]).astype(BF16) for rows in halves]
    accs = [None] * len(halves)

    def down(i, a, cols):
        contrib = _dot(a, wd_ref[cols, :])
        accs[i] = contrib if accs[i] is None else accs[i] + contrib

    pending = None
    for c in range(wg_ref.shape[1] // tf):
        cols = slice(c * tf, (c + 1) * tf)
        for i, hb in enumerate(hbs):
            gate = _dot(hb, wg_ref[:, cols])
            a = (gate * _sigmoid(gate) * _dot(hb, wu_ref[:, cols])).astype(BF16)
            if pending is not None:
                down(*pending)
            pending = (i, a, cols)
    down(*pending)
    for rows, acc in zip(halves, accs):
        o_ref[rows, :] = x_ref[rows, :] + _rmsnorm(acc, gpost_ref[...])


def _ffn(l, x2, gpre, wg, wu, wd, gpost):
    T = x2.shape[0]
    tm = TM_FFN
    d_ff = wg.shape[-1]
    return pl.pallas_call(
        _ffn_kernel,
        grid=(T // tm,),
        in_specs=[
            pl.BlockSpec((tm, D_MODEL), lambda i: (i, 0)),
            _vector_spec(D_MODEL, l, G_FFN_PRE),
            _layer_spec((D_MODEL, d_ff), l),
            _layer_spec((D_MODEL, d_ff), l),
            _layer_spec((d_ff, D_MODEL), l),
            _vector_spec(D_MODEL, l, G_FFN_POST),
        ],
        out_specs=pl.BlockSpec((tm, D_MODEL), lambda i: (i, 0)),
        out_shape=jax.ShapeDtypeStruct((T, D_MODEL), F32),
        compiler_params=pltpu.CompilerParams(
            dimension_semantics=("arbitrary",), vmem_limit_bytes=V7X_VMEM_LIMIT_BYTES),
        name="ffn",
    )(x2, gpre, wg, wu, wd, gpost)


def kernel(x, mem, w_in, b_gate, conv_qk, w_pool, pool_scale, mlstm_norm_g, w_out, g_mix_pre, g_mix_post, g_mem, g_xattn_pre, g_xattn_post, wq_x, wk_x, wv_x, wo_x, g_ffn_pre, g_ffn_post, w_gate, w_up, w_down):
    B, S, D = x.shape
    depth = w_in.shape[0]
    H = MLSTM_HEADS
    x2 = x.reshape(B * S, D)
    pmat = _pool_matrices()
    rows = lambda v: v.reshape(depth, 1, -1).astype(F32)
    table = lambda vs: jnp.stack([v.astype(F32) for v in vs], axis=1)[:, :, None, :]
    gains = table([g_mix_pre, g_mix_post, g_mem, g_xattn_pre, g_xattn_post, g_ffn_pre, g_ffn_post])
    mixvecs = table([pool_scale, mlstm_norm_g])
    bf = lambda w: w.astype(BF16)
    w_in_b = bf(w_in)
    wgate = jnp.concatenate(
        [w_in_b[:, :, OFF_GATE + H:], w_in_b[:, :, OFF_GATE:OFF_GATE + H],
         jnp.zeros((depth, D, V7X_LANES - 2 * H), BF16)], axis=2)
    bias = jnp.concatenate(
        [b_gate[:, H:], b_gate[:, :H], jnp.zeros((depth, V7X_LANES - 2 * H), b_gate.dtype)], axis=1)
    mixer_params = (gains, w_in_b, wgate, rows(bias), conv_qk.astype(F32), pmat, bf(w_pool),
                    mixvecs, mixvecs, bf(w_out), gains)
    memkv_params = (gains, bf(wk_x), bf(wv_x))
    xattn_params = (gains, bf(wq_x), bf(wo_x), gains)
    ffn_params = (gains, bf(w_gate), bf(w_up), bf(w_down), gains)

    for l in range(depth):
        x2 = _mixer(l, x2, *mixer_params, S)
        k, v = _memkv(l, mem, *memkv_params)
        x2 = _xattn(l, x2, k, v, *xattn_params, B, S)
        x2 = _ffn(l, x2, *ffn_params)

    return x2.reshape(B, S, D)
```

```python
import functools

import numpy as np
import jax
import jax.numpy as jnp
from jax import lax
from jax.experimental import pallas as pl
from jax.experimental.pallas import tpu as pltpu

F32 = jnp.float32
BF16 = jnp.bfloat16

EPS = 1e-6
D_MODEL = 1024
N_MEM = 256
POOL_WIDTH = 512
POOL_GROUPS = 4
POOL_GROUP_DIM = 128
POOL_WINDOWS = (2, 4, 8, 16)
MLSTM_WIDTH = 512
MLSTM_HEADS = 4
MLSTM_HEAD_DIM = 128
CONV_WIDTH = 4
XATTN_HEADS = 4
XATTN_HEAD_DIM = 256
OFF_Q = POOL_WIDTH
OFF_K = OFF_Q + MLSTM_WIDTH
OFF_V = OFF_K + MLSTM_WIDTH
OFF_O = OFF_V + MLSTM_WIDTH
OFF_GATE = OFF_O + MLSTM_WIDTH

V7X_LANES = 128
V7X_SUBLANES = 8
V7X_VMEM_LIMIT_BYTES = 56 * 1024 * 1024

G_MIX_PRE, G_MIX_POST, G_MEM, G_XATTN_PRE, G_XATTN_POST, G_FFN_PRE, G_FFN_POST = range(7)
V_POOL_SCALE, V_MLSTM_NORM = range(2)

IN_COLS_PAD = OFF_GATE + V7X_LANES
POOL_HALO = 16
CONV_HALO = 8
POOL_BLOCK = 256
PIECE_COLS = 256

TS_MIX = 512
L_CHUNK = 256
TS_ATT = 2048
TM_FFN = 1024
SUB_ROWS = 512
TF_FFN = 256


def _rmsnorm(x, g):
    return x * lax.rsqrt(jnp.mean(x * x, axis=-1, keepdims=True) + EPS) * g


def _sigmoid(x):
    return 1.0 / (1.0 + jnp.exp(-x))


def _log_sigmoid(x):
    return jnp.minimum(x, 0.0) - jnp.log(1.0 + jnp.exp(-jnp.abs(x)))


def _dot(a, b):
    return jnp.dot(a, b, preferred_element_type=F32)


def _dot_nt(a, b):
    return lax.dot_general(a, b, (((1,), (1,)), ((), ())), preferred_element_type=F32)


def _dot_tn(a, b):
    return lax.dot_general(a, b, (((0,), (0,)), ((), ())), preferred_element_type=F32)


def _const_spec(shape):
    nd = len(shape)
    return pl.BlockSpec(shape, lambda *_: (0,) * nd, pipeline_mode=pl.Buffered(1))


def _vector_spec(n, l, k):
    return pl.BlockSpec((None, None, 1, n), lambda *_: (l, k, 0, 0), pipeline_mode=pl.Buffered(1))


def _layer_spec(shape, l):
    nd = len(shape)
    return pl.BlockSpec((None,) + tuple(shape), lambda *_: (l,) + (0,) * nd, pipeline_mode=pl.Buffered(1))


def _pool_matrices():
    i = np.arange(POOL_BLOCK)[:, None]
    j = np.arange(POOL_BLOCK)[None, :]
    mats = []
    for win in POOL_WINDOWS:
        inside = (j <= i) & (j > i - win)
        mats.append(inside.astype(np.float32) / win - (i == j).astype(np.float32))
    return jnp.asarray(np.stack(mats), dtype=BF16)


def _mixer_kernel(*refs, nt, ntot):
    s = pl.program_id(0)

    def variant(step, stages):
        return functools.partial(_mixer_step, step % 2, 1 - step % 2, *refs, nt=nt, stages=stages)

    steady = jnp.logical_and(s >= 2, s < ntot)
    pl.when(s == 0)(variant(0, ("proj",)))
    pl.when(s == 1)(variant(1, ("proj", "mix")))
    pl.when(jnp.logical_and(steady, lax.rem(s, 2) == 0))(variant(0, ("proj", "mix", "out")))
    pl.when(jnp.logical_and(steady, lax.rem(s, 2) == 1))(variant(1, ("proj", "mix", "out")))
    pl.when(s == ntot)(variant(ntot, ("mix", "out")))
    pl.when(s == ntot + 1)(variant(ntot + 1, ("out",)))


def _mixer_step(a, b, xn_ref, xb_ref, gpre_ref, win_ref, wgate_ref, bias_ref, conv_ref, pmat_ref, wpool_ref,
                pscale_ref, gnorm_ref, wout_ref, gpost_ref, o_ref,
                u0_ref, u1_ref, qk0_ref, qk1_ref, v0_ref, v1_ref, og0_ref, og1_ref,
                g0_ref, g1_ref, mix0_ref, mix1_ref, c_ref, m_ref, *, nt, stages):
    uext_a, uext_b = (u0_ref, u1_ref)[a], (u0_ref, u1_ref)[b]
    qkext_a, qkext_b = (qk0_ref, qk1_ref)[a], (qk0_ref, qk1_ref)[b]
    v_a, v_b = (v0_ref, v1_ref)[a], (v0_ref, v1_ref)[b]
    og_a, og_b = (og0_ref, og1_ref)[a], (og0_ref, og1_ref)[b]
    g_a, g_b = (g0_ref, g1_ref)[a], (g0_ref, g1_ref)[b]
    mix_a, mix_b = (mix0_ref, mix1_ref)[a], (mix0_ref, mix1_ref)[b]
    ts = xn_ref.shape[0]
    L = L_CHUNK
    H = MLSTM_HEADS
    DH = MLSTM_HEAD_DIM
    PC = PIECE_COLS
    s = pl.program_id(0)
    first_a = lax.rem(s, nt) == 0
    tb = lax.rem(s + nt - 1, nt)
    first_b = tb == 0

    hb = _rmsnorm(xn_ref[...], gpre_ref[...]).astype(BF16) if "proj" in stages else None

    out_parts = []

    def out_piece(j):
        def run():
            out_parts.append(_dot(mix_a[...], wout_ref[:, j * PC:(j + 1) * PC]))
            if len(out_parts) == D_MODEL // PC:
                ss = sum(jnp.sum(y * y, axis=1, keepdims=True) for y in out_parts)
                rs = lax.rsqrt(ss * (1.0 / D_MODEL) + EPS)
                for jj, y in enumerate(out_parts):
                    cols = slice(jj * PC, (jj + 1) * PC)
                    o_ref[:, cols] = xb_ref[:, cols] + y * rs * gpost_ref[:, cols]
        return run

    def proj_piece(c0):
        def run():
            w = win_ref[:, c0:c0 + PC] if c0 < OFF_GATE else wgate_ref[...]
            y = _dot(hb, w)
            if c0 < OFF_Q:
                uext_a[POOL_HALO:POOL_HALO + ts, c0:c0 + PC] = y
            elif c0 < OFF_V:
                qkext_a[CONV_HALO:CONV_HALO + ts, c0 - OFF_Q:c0 - OFF_Q + PC] = y
            elif c0 < OFF_O:
                v_a[:, c0 - OFF_V:c0 - OFF_V + PC] = y.astype(BF16)
            elif c0 < OFF_GATE:
                og_a[:, c0 - OFF_O:c0 - OFF_O + PC] = y
            else:
                g_a[...] = y + bias_ref[...]
        return run

    pieces = []
    if "out" in stages:
        pieces += [out_piece(j) for j in range(D_MODEL // PC)]
    if "proj" in stages:
        pieces += [proj_piece(c0) for c0 in range(0, IN_COLS_PAD, PC)]
    pieces.reverse()

    def issue(n=1):
        for _ in range(n):
            if pieces:
                pieces.pop()()

    def history_rows():
        if "mix" in stages:
            uext_a[0:POOL_HALO, :] = jnp.where(first_a, 0.0, uext_b[ts:ts + POOL_HALO, :])
            qkext_a[0:CONV_HALO, :] = jnp.where(first_a, 0.0, qkext_b[ts:ts + CONV_HALO, :])
        else:
            uext_a[0:POOL_HALO, :] = jnp.zeros((POOL_HALO, POOL_WIDTH), F32)
            qkext_a[0:CONV_HALO, :] = jnp.zeros((CONV_HALO, 2 * MLSTM_WIDTH), F32)

    if "mix" not in stages:
        issue(len(pieces))
        if "proj" in stages:
            history_rows()
            c_ref[...] = jnp.zeros_like(c_ref)
            m_ref[...] = jnp.zeros_like(m_ref)
        return

    def pool_window():
        d_mains = []
        for blk in range(ts // POOL_BLOCK):
            r0 = blk * POOL_BLOCK
            for g in range(POOL_GROUPS):
                cols = slice(g * POOL_GROUP_DIM, (g + 1) * POOL_GROUP_DIM)
                ublk = uext_b[POOL_HALO + r0:POOL_HALO + r0 + POOL_BLOCK, cols]
                d_mains.append(_dot(pmat_ref[g], ublk.astype(BF16)))
        return d_mains

    def pool_mix(d_mains):
        for blk in range(ts // POOL_BLOCK):
            r0 = blk * POOL_BLOCK
            pos = tb * ts + r0 + lax.broadcasted_iota(jnp.int32, (POOL_HALO, 1), 0)
            for g, win in enumerate(POOL_WINDOWS):
                cols = slice(g * POOL_GROUP_DIM, (g + 1) * POOL_GROUP_DIM)
                cur = uext_b[POOL_HALO + r0:POOL_HALO + r0 + POOL_HALO, cols]
                acc = cur
                for k in range(1, win):
                    acc = acc + uext_b[POOL_HALO + r0 - k:POOL_HALO + r0 - k + POOL_HALO, cols]
                count = jnp.minimum(pos + 1, win).astype(F32)
                d_head = acc / count - cur
                d = jnp.concatenate([d_head, d_mains[blk * POOL_GROUPS + g][POOL_HALO:, :]], axis=0)
                y = _dot(d.astype(BF16), wpool_ref[g]) * pscale_ref[:, cols]
                mix_b[r0:r0 + POOL_BLOCK, cols] = y.astype(BF16)

    ri = lax.broadcasted_iota(jnp.int32, (L, L), 0)
    ci = lax.broadcasted_iota(jnp.int32, (L, L), 1)
    causal = ci <= ri
    lane_x = lax.broadcasted_iota(jnp.int32, (L, V7X_LANES), 1)
    sub8 = lax.broadcasted_iota(jnp.int32, (V7X_SUBLANES, L), 0)
    lane8 = lax.broadcasted_iota(jnp.int32, (V7X_SUBLANES, L), 1)
    zpad = jnp.zeros((V7X_LANES - V7X_SUBLANES, L), F32)
    onehot = [(lane_x == h).astype(BF16) for h in range(H)]
    cw = conv_ref[...]

    cs = [jnp.where(first_b, 0.0, c_ref[h]) for h in range(H)]
    m_row = jnp.where(first_b, 0.0, m_ref[0:1, :])

    def conv_silu(col0, r0):
        acc = None
        for j in range(CONV_WIDTH):
            off = CONV_HALO + r0 - (CONV_WIDTH - 1) + j
            term = cw[j:j + 1, col0:col0 + DH] * qkext_b[off:off + L, col0:col0 + DH]
            acc = term if acc is None else acc + term
        return acc * _sigmoid(acc)

    def lane_scan(x, op, fill):
        sh = 1
        while sh < L:
            x = op(x, jnp.where(lane8 >= sh, pltpu.roll(x, sh, axis=1), fill))
            sh *= 2
        return x

    def gate_prep(r0, m_row):
        G = g_b[r0:r0 + L, :]
        R8 = G.T[0:V7X_SUBLANES, :]
        low = sub8 < H
        B8 = lane_scan(jnp.where(low, _log_sigmoid(R8), 0.0), jnp.add, 0.0)
        C8 = jnp.where(low, pltpu.roll(R8, H, axis=0) - B8, 0.0)
        M8 = lane_scan(C8, jnp.maximum, -jnp.inf)
        T8 = jnp.where(low, B8, pltpu.roll(M8, H, axis=0))
        TX = jnp.concatenate([T8, zpad], axis=0).T
        valid = lane_x < H
        bX = jnp.where(valid, TX, 0.0)
        cmX = jnp.where(valid, pltpu.roll(TX, V7X_LANES - H, axis=1), 0.0)
        igX = jnp.where(valid, pltpu.roll(G, V7X_LANES - H, axis=1), 0.0)
        mmaxX = jnp.maximum(cmX, m_row)
        bL = bX[L - 1:L, :]
        gkX = bL - bX + igX
        m_new = jnp.maximum(bL + m_row, jnp.max(gkX, axis=0, keepdims=True))
        return dict(C8=C8, mmaxX=mmaxX, m_new=m_new,
                    w_interX=jnp.exp(m_row - mmaxX),
                    enegX=jnp.exp(-(bX + mmaxX)),
                    decay=jnp.exp(bL + m_row - m_new),
                    wkX=jnp.exp(gkX - m_new))

    issue()
    gps = [gate_prep(0, m_row)]
    issue()
    d_mains = pool_window()
    gps.append(gate_prep(L, gps[0]["m_new"]))
    issue()
    pool_mix(d_mains)
    issue()

    for c in range(ts // L):
        r0 = c * L
        C8, mmaxX, w_interX, enegX, decay, wkX = (gps[c][n] for n in
                                                 ("C8", "mmaxX", "w_interX", "enegX", "decay", "wkX"))
        per_head = []
        rsX = None
        qnX = None
        for h in range(H):
            hc = h * DH
            q = conv_silu(hc, r0) * (DH ** -0.5)
            k = conv_silu(MLSTM_WIDTH + hc, r0)
            v_aug = jnp.concatenate([v_b[r0:r0 + L, hc:hc + DH], onehot[h]], axis=1)
            qb = q.astype(BF16)
            S = _dot_nt(qb, k.astype(BF16))
            issue()
            E = jnp.exp(jnp.where(causal, C8[h:h + 1, :] - mmaxX[:, h:h + 1], -jnp.inf))
            av = _dot((S * E).astype(BF16), v_aug)
            qc = _dot(qb, cs[h].astype(BF16))
            rsX = av[:, DH:] if rsX is None else rsX + av[:, DH:]
            qnX = qc[:, DH:] if qnX is None else qnX + qc[:, DH:]
            per_head.append((k, v_aug, av[:, :DH], qc[:, :DH]))
            if c > 0 and h % 2 == 1:
                issue()

        denX = rsX + w_interX * qnX
        rX = 1.0 / jnp.maximum(jnp.abs(denX), enegX)

        for h in range(H):
            hc = h * DH
            k, v_aug, av, qc = per_head[h]
            hout = (av + w_interX[:, h:h + 1] * qc) * rX[:, h:h + 1]
            hn = hout * lax.rsqrt(jnp.mean(hout * hout, axis=1, keepdims=True) + EPS)
            og = _sigmoid(og_b[r0:r0 + L, hc:hc + DH])
            mout = hn * gnorm_ref[:, hc:hc + DH] * og
            mix_b[r0:r0 + L, POOL_WIDTH + hc:POOL_WIDTH + hc + DH] = mout.astype(BF16)

            kw = wkX[:, h:h + 1] * k
            cs[h] = decay[:, h:h + 1] * cs[h] + _dot_tn(kw.astype(BF16), v_aug)
            if c == 0 and h == 1:
                issue()
    m_row = gps[-1]["m_new"]
    assert not pieces, "every projection piece must have an issue point"
    if "proj" in stages:
        history_rows()

    for h in range(H):
        c_ref[h] = cs[h]
    m_ref[0:1, :] = m_row


def _mixer(l, x2, gpre, win, wgate, bias, conv, pmat, wpool, pscale, gnorm, wout, gpost, seq):
    T = x2.shape[0]
    ts = TS_MIX
    nt = seq // ts
    ntot = T // ts
    return pl.pallas_call(
        functools.partial(_mixer_kernel, nt=nt, ntot=ntot),
        grid=(ntot + 2,),
        in_specs=[
            pl.BlockSpec((ts, D_MODEL), lambda s: (jnp.minimum(s, ntot - 1), 0)),
            pl.BlockSpec((ts, D_MODEL), lambda s: (jnp.maximum(s - 2, 0), 0)),
            _vector_spec(D_MODEL, l, G_MIX_PRE),
            _layer_spec((D_MODEL, OFF_GATE), l),
            _layer_spec((D_MODEL, V7X_LANES), l),
            _layer_spec((1, V7X_LANES), l),
            _layer_spec((CONV_WIDTH, 2 * MLSTM_WIDTH), l),
            _const_spec((POOL_GROUPS, POOL_BLOCK, POOL_BLOCK)),
            _layer_spec((POOL_GROUPS, POOL_GROUP_DIM, POOL_GROUP_DIM), l),
            _vector_spec(POOL_WIDTH, l, V_POOL_SCALE),
            _vector_spec(MLSTM_WIDTH, l, V_MLSTM_NORM),
            _layer_spec((D_MODEL, D_MODEL), l),
            _vector_spec(D_MODEL, l, G_MIX_POST),
        ],
        out_specs=pl.BlockSpec((ts, D_MODEL), lambda s: (jnp.maximum(s - 2, 0), 0)),
        out_shape=jax.ShapeDtypeStruct((T, D_MODEL), F32),
        scratch_shapes=[
            *[pltpu.VMEM((ts + POOL_HALO, POOL_WIDTH), F32)] * 2,
            *[pltpu.VMEM((ts + CONV_HALO, 2 * MLSTM_WIDTH), F32)] * 2,
            *[pltpu.VMEM((ts, MLSTM_WIDTH), BF16)] * 2,
            *[pltpu.VMEM((ts, MLSTM_WIDTH), F32)] * 2,
            *[pltpu.VMEM((ts, V7X_LANES), F32)] * 2,
            *[pltpu.VMEM((ts, D_MODEL), BF16)] * 2,
            pltpu.VMEM((MLSTM_HEADS, MLSTM_HEAD_DIM, 2 * MLSTM_HEAD_DIM), F32),
            pltpu.VMEM((V7X_SUBLANES, V7X_LANES), F32),
        ],
        compiler_params=pltpu.CompilerParams(
            dimension_semantics=("arbitrary",), vmem_limit_bytes=V7X_VMEM_LIMIT_BYTES),
        name="mixer",
    )(x2, x2, gpre, win, wgate, bias, conv, pmat, wpool, pscale, gnorm, wout, gpost)


def _memkv_kernel(mem_ref, g_ref, wk_ref, wv_ref, k_ref, v_ref):
    mb = _rmsnorm(mem_ref[0], g_ref[...]).astype(BF16)
    k_ref[0] = _dot(mb, wk_ref[...]).astype(BF16)
    v_ref[0] = _dot(mb, wv_ref[...]).astype(BF16)


def _memkv(l, mem, g, wk, wv):
    B = mem.shape[0]
    blk = pl.BlockSpec((1, N_MEM, D_MODEL), lambda b: (b, 0, 0))
    return pl.pallas_call(
        _memkv_kernel,
        grid=(B,),
        in_specs=[blk, _vector_spec(D_MODEL, l, G_MEM), _layer_spec((D_MODEL, D_MODEL), l),
                  _layer_spec((D_MODEL, D_MODEL), l)],
        out_specs=[blk, blk],
        out_shape=[jax.ShapeDtypeStruct((B, N_MEM, D_MODEL), BF16)] * 2,
        compiler_params=pltpu.CompilerParams(
            dimension_semantics=("arbitrary",), vmem_limit_bytes=V7X_VMEM_LIMIT_BYTES),
        name="memkv",
    )(mem, g, wk, wv)


def _xattn_kernel(x_ref, k_ref, v_ref, gpre_ref, wq_ref, wo_ref, gpost_ref, o_ref, att_ref):
    halves = [slice(r0, r0 + SUB_ROWS) for r0 in range(0, x_ref.shape[0], SUB_ROWS)]
    qbs = []
    for rows in halves:
        hb = _rmsnorm(x_ref[rows, :], gpre_ref[...]).astype(BF16)
        qbs.append((_dot(hb, wq_ref[...]) * (XATTN_HEAD_DIM ** -0.5)).astype(BF16))
    for h in range(XATTN_HEADS):
        cols = slice(h * XATTN_HEAD_DIM, (h + 1) * XATTN_HEAD_DIM)
        scores = [_dot_nt(qb[:, cols], k_ref[0, :, cols]) for qb in qbs]
        for rows, s in zip(halves, scores):
            p = jnp.exp(s - jnp.max(s, axis=1, keepdims=True))
            o = _dot(p.astype(BF16), v_ref[0, :, cols]) / jnp.sum(p, axis=1, keepdims=True)
            att_ref[rows, cols] = o.astype(BF16)
    for rows in halves:
        y = _dot(att_ref[rows, :], wo_ref[...])
        o_ref[rows, :] = x_ref[rows, :] + _rmsnorm(y, gpost_ref[...])


def _xattn(l, x2, k, v, gpre, wq, wo, gpost, batch, seq):
    T = x2.shape[0]
    ts = TS_ATT
    nt = seq // ts
    tok = lambda b, t: (b * nt + t, 0)
    kvb = pl.BlockSpec((1, N_MEM, D_MODEL), lambda b, t: (b, 0, 0))
    return pl.pallas_call(
        _xattn_kernel,
        grid=(batch, nt),
        in_specs=[
            pl.BlockSpec((ts, D_MODEL), tok), kvb, kvb,
            _vector_spec(D_MODEL, l, G_XATTN_PRE),
            _layer_spec((D_MODEL, D_MODEL), l),
            _layer_spec((D_MODEL, D_MODEL), l),
            _vector_spec(D_MODEL, l, G_XATTN_POST),
        ],
        out_specs=pl.BlockSpec((ts, D_MODEL), tok),
        out_shape=jax.ShapeDtypeStruct((T, D_MODEL), F32),
        scratch_shapes=[pltpu.VMEM((ts, D_MODEL), BF16)],
        compiler_params=pltpu.CompilerParams(
            dimension_semantics=("arbitrary", "arbitrary"), vmem_limit_bytes=V7X_VMEM_LIMIT_BYTES),
        name="xattn",
    )(x2, k, v, gpre, wq, wo, gpost)


def _ffn_kernel(x_ref, gpre_ref, wg_ref, wu_ref, wd_ref, gpost_ref, o_ref):
    tf = TF_FFN
    halves = [slice(r0, r0 + SUB_ROWS) for r0 in range(0, x_ref.shape[0], SUB_ROWS)]
    hbs = [_rmsnorm(x_ref[rows, :], gpre_ref[...]).astype(BF16) for rows in halves]
    accs = [None] * len(halves)

    def down(i, a, cols):
        contrib = _dot(a, wd_ref[cols, :])
        accs[i] = contrib if accs[i] is None else accs[i] + contrib

    pending = None
    for c in range(wg_ref.shape[1] // tf):
        cols = slice(c * tf, (c + 1) * tf)
        for i, hb in enumerate(hbs):
            gate = _dot(hb, wg_ref[:, cols])
            a = (gate * _sigmoid(gate) * _dot(hb, wu_ref[:, cols])).astype(BF16)
            if pending is not None:
                down(*pending)
            pending = (i, a, cols)
    down(*pending)
    for rows, acc in zip(halves, accs):
        o_ref[rows, :] = x_ref[rows, :] + _rmsnorm(acc, gpost_ref[...])


def _ffn(l, x2, gpre, wg, wu, wd, gpost):
    T = x2.shape[0]
    tm = TM_FFN
    d_ff = wg.shape[-1]
    return pl.pallas_call(
        _ffn_kernel,
        grid=(T // tm,),
        in_specs=[
            pl.BlockSpec((tm, D_MODEL), lambda i: (i, 0)),
            _vector_spec(D_MODEL, l, G_FFN_PRE),
            _layer_spec((D_MODEL, d_ff), l),
            _layer_spec((D_MODEL, d_ff), l),
            _layer_spec((d_ff, D_MODEL), l),
            _vector_spec(D_MODEL, l, G_FFN_POST),
        ],
        out_specs=pl.BlockSpec((tm, D_MODEL), lambda i: (i, 0)),
        out_shape=jax.ShapeDtypeStruct((T, D_MODEL), F32),
        compiler_params=pltpu.CompilerParams(
            dimension_semantics=("arbitrary",), vmem_limit_bytes=V7X_VMEM_LIMIT_BYTES),
        name="ffn",
    )(x2, gpre, wg, wu, wd, gpost)


def kernel(x, mem, w_in, b_gate, conv_qk, w_pool, pool_scale, mlstm_norm_g, w_out, g_mix_pre, g_mix_post, g_mem, g_xattn_pre, g_xattn_post, wq_x, wk_x, wv_x, wo_x, g_ffn_pre, g_ffn_post, w_gate, w_up, w_down):
    B, S, D = x.shape
    depth = w_in.shape[0]
    H = MLSTM_HEADS
    x2 = x.reshape(B * S, D)
    pmat = _pool_matrices()
    rows = lambda v: v.reshape(depth, 1, -1).astype(F32)
    table = lambda vs: jnp.stack([v.astype(F32) for v in vs], axis=1)[:, :, None, :]
    gains = table([g_mix_pre, g_mix_post, g_mem, g_xattn_pre, g_xattn_post, g_ffn_pre, g_ffn_post])
    mixvecs = table([pool_scale, mlstm_norm_g])
    bf = lambda w: w.astype(BF16)
    w_in_b = bf(w_in)
    wgate = jnp.concatenate(
        [w_in_b[:, :, OFF_GATE + H:], w_in_b[:, :, OFF_GATE:OFF_GATE + H],
         jnp.zeros((depth, D, V7X_LANES - 2 * H), BF16)], axis=2)
    bias = jnp.concatenate(
        [b_gate[:, H:], b_gate[:, :H], jnp.zeros((depth, V7X_LANES - 2 * H), b_gate.dtype)], axis=1)
    mixer_params = (gains, w_in_b, wgate, rows(bias), conv_qk.astype(F32), pmat, bf(w_pool),
                    mixvecs, mixvecs, bf(w_out), gains)
    memkv_params = (gains, bf(wk_x), bf(wv_x))
    xattn_params = (gains, bf(wq_x), bf(wo_x), gains)
    ffn_params = (gains, bf(w_gate), bf(w_up), bf(w_down), gains)

    for l in range(depth):
        x2 = _mixer(l, x2, *mixer_params, S)
        k, v = _memkv(l, mem, *memkv_params)
        x2 = _xattn(l, x2, k, v, *xattn_params, B, S)
        x2 = _ffn(l, x2, *ffn_params)

    return x2.reshape(B, S, D)
```

```python
import functools

import numpy as np
import jax
import jax.numpy as jnp
from jax import lax
from jax.experimental import pallas as pl
from jax.experimental.pallas import tpu as pltpu

F32 = jnp.float32
BF16 = jnp.bfloat16

EPS = 1e-6
D_MODEL = 1024
N_MEM = 256
POOL_WIDTH = 512
POOL_GROUPS = 4
POOL_GROUP_DIM = 128
POOL_WINDOWS = (2, 4, 8, 16)
MLSTM_WIDTH = 512
MLSTM_HEADS = 4
MLSTM_HEAD_DIM = 128
CONV_WIDTH = 4
XATTN_HEADS = 4
XATTN_HEAD_DIM = 256
OFF_Q = POOL_WIDTH
OFF_K = OFF_Q + MLSTM_WIDTH
OFF_V = OFF_K + MLSTM_WIDTH
OFF_O = OFF_V + MLSTM_WIDTH
OFF_GATE = OFF_O + MLSTM_WIDTH

V7X_LANES = 128
V7X_SUBLANES = 8
V7X_VMEM_LIMIT_BYTES = 56 * 1024 * 1024

G_MIX_PRE, G_MIX_POST, G_MEM, G_XATTN_PRE, G_XATTN_POST, G_FFN_PRE, G_FFN_POST = range(7)
V_POOL_SCALE, V_MLSTM_NORM = range(2)

IN_COLS_PAD = OFF_GATE + V7X_LANES
POOL_HALO = 16
CONV_HALO = 8
POOL_BLOCK = 256
PIECE_COLS = 256

TS_MIX = 512
L_CHUNK = 256
TS_ATT = 2048
TM_FFN = 1024
SUB_ROWS = 512
TF_FFN = 256


def _rmsnorm(x, g):
    return x * lax.rsqrt(jnp.mean(x * x, axis=-1, keepdims=True) + EPS) * g


def _sigmoid(x):
    return 1.0 / (1.0 + jnp.exp(-x))


def _log_sigmoid(x):
    return jnp.minimum(x, 0.0) - jnp.log(1.0 + jnp.exp(-jnp.abs(x)))


def _dot(a, b):
    return jnp.dot(a, b, preferred_element_type=F32)


def _dot_nt(a, b):
    return lax.dot_general(a, b, (((1,), (1,)), ((), ())), preferred_element_type=F32)


def _dot_tn(a, b):
    return lax.dot_general(a, b, (((0,), (0,)), ((), ())), preferred_element_type=F32)


def _const_spec(shape):
    nd = len(shape)
    return pl.BlockSpec(shape, lambda *_: (0,) * nd, pipeline_mode=pl.Buffered(1))


def _vector_spec(n, l, k):
    return pl.BlockSpec((None, None, 1, n), lambda *_: (l, k, 0, 0), pipeline_mode=pl.Buffered(1))


def _layer_spec(shape, l):
    nd = len(shape)
    return pl.BlockSpec((None,) + tuple(shape), lambda *_: (l,) + (0,) * nd, pipeline_mode=pl.Buffered(1))


def _pool_matrices():
    i = np.arange(POOL_BLOCK)[:, None]
    j = np.arange(POOL_BLOCK)[None, :]
    mats = []
    for win in POOL_WINDOWS:
        inside = (j <= i) & (j > i - win)
        mats.append(inside.astype(np.float32) / win - (i == j).astype(np.float32))
    return jnp.asarray(np.stack(mats), dtype=BF16)


def _mixer_kernel(*refs, nt, ntot):
    s = pl.program_id(0)

    def variant(step, stages):
        return functools.partial(_mixer_step, step % 2, 1 - step % 2, *refs, nt=nt, stages=stages)

    steady = jnp.logical_and(s >= 1, s <= ntot)
    pl.when(s == 0)(variant(0, ("proj",)))
    pl.when(jnp.logical_and(steady, lax.rem(s, 2) == 0))(variant(0, ("proj", "mix", "out")))
    pl.when(jnp.logical_and(steady, lax.rem(s, 2) == 1))(variant(1, ("proj", "mix", "out")))
    pl.when(s == ntot + 1)(variant(ntot + 1, ("out",)))


def _mixer_step(a, b, xn_ref, xb_ref, gpre_ref, win_ref, wgate_ref, bias_ref, conv_ref, pmat_ref, wpool_ref,
                pscale_ref, gnorm_ref, wout_ref, gpost_ref, o_ref,
                u0_ref, u1_ref, qk0_ref, qk1_ref, v0_ref, v1_ref, og0_ref, og1_ref,
                g0_ref, g1_ref, mix0_ref, mix1_ref, c_ref, m_ref, *, nt, stages):
    uext_a, uext_b = (u0_ref, u1_ref)[a], (u0_ref, u1_ref)[b]
    qkext_a, qkext_b = (qk0_ref, qk1_ref)[a], (qk0_ref, qk1_ref)[b]
    v_a, v_b = (v0_ref, v1_ref)[a], (v0_ref, v1_ref)[b]
    og_a, og_b = (og0_ref, og1_ref)[a], (og0_ref, og1_ref)[b]
    g_a, g_b = (g0_ref, g1_ref)[a], (g0_ref, g1_ref)[b]
    mix_a, mix_b = (mix0_ref, mix1_ref)[a], (mix0_ref, mix1_ref)[b]
    ts = xn_ref.shape[0]
    L = L_CHUNK
    H = MLSTM_HEADS
    DH = MLSTM_HEAD_DIM
    PC = PIECE_COLS
    s = pl.program_id(0)
    first_a = lax.rem(s, nt) == 0
    tb = lax.rem(s + nt - 1, nt)
    first_b = tb == 0

    hb = _rmsnorm(xn_ref[...], gpre_ref[...]).astype(BF16) if "proj" in stages else None

    out_parts = []

    def out_piece(j):
        def run():
            out_parts.append(_dot(mix_a[...], wout_ref[:, j * PC:(j + 1) * PC]))
            if len(out_parts) == D_MODEL // PC:
                ss = sum(jnp.sum(y * y, axis=1, keepdims=True) for y in out_parts)
                rs = lax.rsqrt(ss * (1.0 / D_MODEL) + EPS)
                for jj, y in enumerate(out_parts):
                    cols = slice(jj * PC, (jj + 1) * PC)
                    o_ref[:, cols] = xb_ref[:, cols] + y * rs * gpost_ref[:, cols]
        return run

    def proj_piece(c0):
        def run():
            w = win_ref[:, c0:c0 + PC] if c0 < OFF_GATE else wgate_ref[...]
            y = _dot(hb, w)
            if c0 < OFF_Q:
                uext_a[POOL_HALO:POOL_HALO + ts, c0:c0 + PC] = y
            elif c0 < OFF_V:
                qkext_a[CONV_HALO:CONV_HALO + ts, c0 - OFF_Q:c0 - OFF_Q + PC] = y
            elif c0 < OFF_O:
                v_a[:, c0 - OFF_V:c0 - OFF_V + PC] = y.astype(BF16)
            elif c0 < OFF_GATE:
                og_a[:, c0 - OFF_O:c0 - OFF_O + PC] = y
            else:
                g_a[...] = y + bias_ref[...]
        return run

    pieces = []
    if "out" in stages:
        pieces += [out_piece(j) for j in range(D_MODEL // PC)]
    if "proj" in stages:
        pieces += [proj_piece(c0) for c0 in range(0, IN_COLS_PAD, PC)]
    pieces.reverse()

    def issue(n=1):
        for _ in range(n):
            if pieces:
                pieces.pop()()

    def history_rows():
        if "mix" in stages:
            uext_a[0:POOL_HALO, :] = jnp.where(first_a, 0.0, uext_b[ts:ts + POOL_HALO, :])
            qkext_a[0:CONV_HALO, :] = jnp.where(first_a, 0.0, qkext_b[ts:ts + CONV_HALO, :])
        else:
            uext_a[0:POOL_HALO, :] = jnp.zeros((POOL_HALO, POOL_WIDTH), F32)
            qkext_a[0:CONV_HALO, :] = jnp.zeros((CONV_HALO, 2 * MLSTM_WIDTH), F32)

    if "mix" not in stages:
        issue(len(pieces))
        if "proj" in stages:
            history_rows()
            for ref in (mix_b, c_ref, m_ref):
                ref[...] = jnp.zeros_like(ref)
        return

    def pool_window():
        d_mains = []
        for blk in range(ts // POOL_BLOCK):
            r0 = blk * POOL_BLOCK
            for g in range(POOL_GROUPS):
                cols = slice(g * POOL_GROUP_DIM, (g + 1) * POOL_GROUP_DIM)
                ublk = uext_b[POOL_HALO + r0:POOL_HALO + r0 + POOL_BLOCK, cols]
                d_mains.append(_dot(pmat_ref[g], ublk.astype(BF16)))
        return d_mains

    def pool_mix(d_mains):
        for blk in range(ts // POOL_BLOCK):
            r0 = blk * POOL_BLOCK
            pos = tb * ts + r0 + lax.broadcasted_iota(jnp.int32, (POOL_HALO, 1), 0)
            for g, win in enumerate(POOL_WINDOWS):
                cols = slice(g * POOL_GROUP_DIM, (g + 1) * POOL_GROUP_DIM)
                cur = uext_b[POOL_HALO + r0:POOL_HALO + r0 + POOL_HALO, cols]
                acc = cur
                for k in range(1, win):
                    acc = acc + uext_b[POOL_HALO + r0 - k:POOL_HALO + r0 - k + POOL_HALO, cols]
                count = jnp.minimum(pos + 1, win).astype(F32)
                d_head = acc / count - cur
                d = jnp.concatenate([d_head, d_mains[blk * POOL_GROUPS + g][POOL_HALO:, :]], axis=0)
                y = _dot(d.astype(BF16), wpool_ref[g]) * pscale_ref[:, cols]
                mix_b[r0:r0 + POOL_BLOCK, cols] = y.astype(BF16)

    ri = lax.broadcasted_iota(jnp.int32, (L, L), 0)
    ci = lax.broadcasted_iota(jnp.int32, (L, L), 1)
    causal = ci <= ri
    lane_x = lax.broadcasted_iota(jnp.int32, (L, V7X_LANES), 1)
    sub8 = lax.broadcasted_iota(jnp.int32, (V7X_SUBLANES, L), 0)
    lane8 = lax.broadcasted_iota(jnp.int32, (V7X_SUBLANES, L), 1)
    zpad = jnp.zeros((V7X_LANES - V7X_SUBLANES, L), F32)
    onehot = [(lane_x == h).astype(BF16) for h in range(H)]
    cw = conv_ref[...]

    cs = [jnp.where(first_b, 0.0, c_ref[h]) for h in range(H)]
    m_row = jnp.where(first_b, 0.0, m_ref[0:1, :])

    def conv_silu(col0, r0):
        acc = None
        for j in range(CONV_WIDTH):
            off = CONV_HALO + r0 - (CONV_WIDTH - 1) + j
            term = cw[j:j + 1, col0:col0 + DH] * qkext_b[off:off + L, col0:col0 + DH]
            acc = term if acc is None else acc + term
        return acc * _sigmoid(acc)

    def lane_scan(x, op, fill):
        sh = 1
        while sh < L:
            x = op(x, jnp.where(lane8 >= sh, pltpu.roll(x, sh, axis=1), fill))
            sh *= 2
        return x

    def gate_prep(r0, m_row):
        G = g_b[r0:r0 + L, :]
        R8 = G.T[0:V7X_SUBLANES, :]
        low = sub8 < H
        B8 = lane_scan(jnp.where(low, _log_sigmoid(R8), 0.0), jnp.add, 0.0)
        C8 = jnp.where(low, pltpu.roll(R8, H, axis=0) - B8, 0.0)
        M8 = lane_scan(C8, jnp.maximum, -jnp.inf)
        T8 = jnp.where(low, B8, pltpu.roll(M8, H, axis=0))
        TX = jnp.concatenate([T8, zpad], axis=0).T
        valid = lane_x < H
        bX = jnp.where(valid, TX, 0.0)
        cmX = jnp.where(valid, pltpu.roll(TX, V7X_LANES - H, axis=1), 0.0)
        igX = jnp.where(valid, pltpu.roll(G, V7X_LANES - H, axis=1), 0.0)
        mmaxX = jnp.maximum(cmX, m_row)
        bL = bX[L - 1:L, :]
        gkX = bL - bX + igX
        m_new = jnp.maximum(bL + m_row, jnp.max(gkX, axis=0, keepdims=True))
        return dict(C8=C8, mmaxX=mmaxX, m_new=m_new,
                    w_interX=jnp.exp(m_row - mmaxX),
                    enegX=jnp.exp(-(bX + mmaxX)),
                    decay=jnp.exp(bL + m_row - m_new),
                    wkX=jnp.exp(gkX - m_new))

    issue()
    gps = [gate_prep(0, m_row)]
    issue()
    d_mains = pool_window()
    gps.append(gate_prep(L, gps[0]["m_new"]))
    issue()
    pool_mix(d_mains)
    issue()

    for c in range(ts // L):
        r0 = c * L
        C8, mmaxX, w_interX, enegX, decay, wkX = (gps[c][n] for n in
                                                 ("C8", "mmaxX", "w_interX", "enegX", "decay", "wkX"))
        per_head = []
        rsX = None
        qnX = None
        for h in range(H):
            hc = h * DH
            q = conv_silu(hc, r0) * (DH ** -0.5)
            k = conv_silu(MLSTM_WIDTH + hc, r0)
            v_aug = jnp.concatenate([v_b[r0:r0 + L, hc:hc + DH], onehot[h]], axis=1)
            qb = q.astype(BF16)
            S = _dot_nt(qb, k.astype(BF16))
            issue()
            E = jnp.exp(jnp.where(causal, C8[h:h + 1, :] - mmaxX[:, h:h + 1], -jnp.inf))
            av = _dot((S * E).astype(BF16), v_aug)
            qc = _dot(qb, cs[h].astype(BF16))
            rsX = av[:, DH:] if rsX is None else rsX + av[:, DH:]
            qnX = qc[:, DH:] if qnX is None else qnX + qc[:, DH:]
            per_head.append((k, v_aug, av[:, :DH], qc[:, :DH]))
            if c > 0 and h % 2 == 1:
                issue()

        denX = rsX + w_interX * qnX
        rX = 1.0 / jnp.maximum(jnp.abs(denX), enegX)

        for h in range(H):
            hc = h * DH
            k, v_aug, av, qc = per_head[h]
            hout = (av + w_interX[:, h:h + 1] * qc) * rX[:, h:h + 1]
            hn = hout * lax.rsqrt(jnp.mean(hout * hout, axis=1, keepdims=True) + EPS)
            og = _sigmoid(og_b[r0:r0 + L, hc:hc + DH])
            mout = hn * gnorm_ref[:, hc:hc + DH] * og
            mix_b[r0:r0 + L, POOL_WIDTH + hc:POOL_WIDTH + hc + DH] = mout.astype(BF16)

            kw = wkX[:, h:h + 1] * k
            cs[h] = decay[:, h:h + 1] * cs[h] + _dot_tn(kw.astype(BF16), v_aug)
            if c == 0 and h == 1:
                issue()
    m_row = gps[-1]["m_new"]
    assert not pieces, "every projection piece must have an issue point"
    if "proj" in stages:
        history_rows()

    for h in range(H):
        c_ref[h] = cs[h]
    m_ref[0:1, :] = m_row


def _mixer(l, x2, gpre, win, wgate, bias, conv, pmat, wpool, pscale, gnorm, wout, gpost, seq):
    T = x2.shape[0]
    ts = TS_MIX
    nt = seq // ts
    ntot = T // ts
    return pl.pallas_call(
        functools.partial(_mixer_kernel, nt=nt, ntot=ntot),
        grid=(ntot + 2,),
        in_specs=[
            pl.BlockSpec((ts, D_MODEL), lambda s: (jnp.minimum(s, ntot - 1), 0)),
            pl.BlockSpec((ts, D_MODEL), lambda s: (jnp.maximum(s - 2, 0), 0)),
            _vector_spec(D_MODEL, l, G_MIX_PRE),
            _layer_spec((D_MODEL, OFF_GATE), l),
            _layer_spec((D_MODEL, V7X_LANES), l),
            _layer_spec((1, V7X_LANES), l),
            _layer_spec((CONV_WIDTH, 2 * MLSTM_WIDTH), l),
            _const_spec((POOL_GROUPS, POOL_BLOCK, POOL_BLOCK)),
            _layer_spec((POOL_GROUPS, POOL_GROUP_DIM, POOL_GROUP_DIM), l),
            _vector_spec(POOL_WIDTH, l, V_POOL_SCALE),
            _vector_spec(MLSTM_WIDTH, l, V_MLSTM_NORM),
            _layer_spec((D_MODEL, D_MODEL), l),
            _vector_spec(D_MODEL, l, G_MIX_POST),
        ],
        out_specs=pl.BlockSpec((ts, D_MODEL), lambda s: (jnp.maximum(s - 2, 0), 0)),
        out_shape=jax.ShapeDtypeStruct((T, D_MODEL), F32),
        scratch_shapes=[
            *[pltpu.VMEM((ts + POOL_HALO, POOL_WIDTH), F32)] * 2,
            *[pltpu.VMEM((ts + CONV_HALO, 2 * MLSTM_WIDTH), F32)] * 2,
            *[pltpu.VMEM((ts, MLSTM_WIDTH), BF16)] * 2,
            *[pltpu.VMEM((ts, MLSTM_WIDTH), F32)] * 2,
            *[pltpu.VMEM((ts, V7X_LANES), F32)] * 2,
            *[pltpu.VMEM((ts, D_MODEL), BF16)] * 2,
            pltpu.VMEM((MLSTM_HEADS, MLSTM_HEAD_DIM, 2 * MLSTM_HEAD_DIM), F32),
            pltpu.VMEM((V7X_SUBLANES, V7X_LANES), F32),
        ],
        compiler_params=pltpu.CompilerParams(
            dimension_semantics=("arbitrary",), vmem_limit_bytes=V7X_VMEM_LIMIT_BYTES),
        name="mixer",
    )(x2, x2, gpre, win, wgate, bias, conv, pmat, wpool, pscale, gnorm, wout, gpost)


def _memkv_kernel(mem_ref, g_ref, wk_ref, wv_ref, k_ref, v_ref):
    mb = _rmsnorm(mem_ref[0], g_ref[...]).astype(BF16)
    k_ref[0] = _dot(mb, wk_ref[...]).astype(BF16)
    v_ref[0] = _dot(mb, wv_ref[...]).astype(BF16)


def _memkv(l, mem, g, wk, wv):
    B = mem.shape[0]
    blk = pl.BlockSpec((1, N_MEM, D_MODEL), lambda b: (b, 0, 0))
    return pl.pallas_call(
        _memkv_kernel,
        grid=(B,),
        in_specs=[blk, _vector_spec(D_MODEL, l, G_MEM), _layer_spec((D_MODEL, D_MODEL), l),
                  _layer_spec((D_MODEL, D_MODEL), l)],
        out_specs=[blk, blk],
        out_shape=[jax.ShapeDtypeStruct((B, N_MEM, D_MODEL), BF16)] * 2,
        compiler_params=pltpu.CompilerParams(
            dimension_semantics=("arbitrary",), vmem_limit_bytes=V7X_VMEM_LIMIT_BYTES),
        name="memkv",
    )(mem, g, wk, wv)


def _xattn_kernel(x_ref, k_ref, v_ref, gpre_ref, wq_ref, wo_ref, gpost_ref, o_ref, att_ref):
    halves = [slice(r0, r0 + SUB_ROWS) for r0 in range(0, x_ref.shape[0], SUB_ROWS)]
    qbs = []
    for rows in halves:
        hb = _rmsnorm(x_ref[rows, :], gpre_ref[...]).astype(BF16)
        qbs.append((_dot(hb, wq_ref[...]) * (XATTN_HEAD_DIM ** -0.5)).astype(BF16))
    for h in range(XATTN_HEADS):
        cols = slice(h * XATTN_HEAD_DIM, (h + 1) * XATTN_HEAD_DIM)
        scores = [_dot_nt(qb[:, cols], k_ref[0, :, cols]) for qb in qbs]
        for rows, s in zip(halves, scores):
            p = jnp.exp(s - jnp.max(s, axis=1, keepdims=True))
            o = _dot(p.astype(BF16), v_ref[0, :, cols]) / jnp.sum(p, axis=1, keepdims=True)
            att_ref[rows, cols] = o.astype(BF16)
    for rows in halves:
        y = _dot(att_ref[rows, :], wo_ref[...])
        o_ref[rows, :] = x_ref[rows, :] + _rmsnorm(y, gpost_ref[...])


def _xattn(l, x2, k, v, gpre, wq, wo, gpost, batch, seq):
    T = x2.shape[0]
    ts = TS_ATT
    nt = seq // ts
    tok = lambda b, t: (b * nt + t, 0)
    kvb = pl.BlockSpec((1, N_MEM, D_MODEL), lambda b, t: (b, 0, 0))
    return pl.pallas_call(
        _xattn_kernel,
        grid=(batch, nt),
        in_specs=[
            pl.BlockSpec((ts, D_MODEL), tok), kvb, kvb,
            _vector_spec(D_MODEL, l, G_XATTN_PRE),
            _layer_spec((D_MODEL, D_MODEL), l),
            _layer_spec((D_MODEL, D_MODEL), l),
            _vector_spec(D_MODEL, l, G_XATTN_POST),
        ],
        out_specs=pl.BlockSpec((ts, D_MODEL), tok),
        out_shape=jax.ShapeDtypeStruct((T, D_MODEL), F32),
        scratch_shapes=[pltpu.VMEM((ts, D_MODEL), BF16)],
        compiler_params=pltpu.CompilerParams(
            dimension_semantics=("arbitrary", "arbitrary"), vmem_limit_bytes=V7X_VMEM_LIMIT_BYTES),
        name="xattn",
    )(x2, k, v, gpre, wq, wo, gpost)


def _ffn_kernel(x_ref, gpre_ref, wg_ref, wu_ref, wd_ref, gpost_ref, o_ref):
    tf = TF_FFN
    halves = [slice(r0, r0 + SUB_ROWS) for r0 in range(0, x_ref.shape[0], SUB_ROWS)]
    hbs = [_rmsnorm(x_ref[rows, :], gpre_ref[...]).astype(BF16) for rows in halves]
    accs = [None] * len(halves)

    def down(i, a, cols):
        contrib = _dot(a, wd_ref[cols, :])
        accs[i] = contrib if accs[i] is None else accs[i] + contrib

    pending = None
    for c in range(wg_ref.shape[1] // tf):
        cols = slice(c * tf, (c + 1) * tf)
        for i, hb in enumerate(hbs):
            gate = _dot(hb, wg_ref[:, cols])
            a = (gate * _sigmoid(gate) * _dot(hb, wu_ref[:, cols])).astype(BF16)
            if pending is not None:
                down(*pending)
            pending = (i, a, cols)
    down(*pending)
    for rows, acc in zip(halves, accs):
        o_ref[rows, :] = x_ref[rows, :] + _rmsnorm(acc, gpost_ref[...])


def _ffn(l, x2, gpre, wg, wu, wd, gpost):
    T = x2.shape[0]
    tm = TM_FFN
    d_ff = wg.shape[-1]
    return pl.pallas_call(
        _ffn_kernel,
        grid=(T // tm,),
        in_specs=[
            pl.BlockSpec((tm, D_MODEL), lambda i: (i, 0)),
            _vector_spec(D_MODEL, l, G_FFN_PRE),
            _layer_spec((D_MODEL, d_ff), l),
            _layer_spec((D_MODEL, d_ff), l),
            _layer_spec((d_ff, D_MODEL), l),
            _vector_spec(D_MODEL, l, G_FFN_POST),
        ],
        out_specs=pl.BlockSpec((tm, D_MODEL), lambda i: (i, 0)),
        out_shape=jax.ShapeDtypeStruct((T, D_MODEL), F32),
        compiler_params=pltpu.CompilerParams(
            dimension_semantics=("arbitrary",), vmem_limit_bytes=V7X_VMEM_LIMIT_BYTES),
        name="ffn",
    )(x2, gpre, wg, wu, wd, gpost)


def kernel(x, mem, w_in, b_gate, conv_qk, w_pool, pool_scale, mlstm_norm_g, w_out, g_mix_pre, g_mix_post, g_mem, g_xattn_pre, g_xattn_post, wq_x, wk_x, wv_x, wo_x, g_ffn_pre, g_ffn_post, w_gate, w_up, w_down):
    B, S, D = x.shape
    depth = w_in.shape[0]
    H = MLSTM_HEADS
    x2 = x.reshape(B * S, D)
    pmat = _pool_matrices()
    rows = lambda v: v.reshape(depth, 1, -1).astype(F32)
    table = lambda vs: jnp.stack([v.astype(F32) for v in vs], axis=1)[:, :, None, :]
    gains = table([g_mix_pre, g_mix_post, g_mem, g_xattn_pre, g_xattn_post, g_ffn_pre, g_ffn_post])
    mixvecs = table([pool_scale, mlstm_norm_g])
    bf = lambda w: w.astype(BF16)
    w_in_b = bf(w_in)
    wgate = jnp.concatenate(
        [w_in_b[:, :, OFF_GATE + H:], w_in_b[:, :, OFF_GATE:OFF_GATE + H],
         jnp.zeros((depth, D, V7X_LANES - 2 * H), BF16)], axis=2)
    bias = jnp.concatenate(
        [b_gate[:, H:], b_gate[:, :H], jnp.zeros((depth, V7X_LANES - 2 * H), b_gate.dtype)], axis=1)
    mixer_params = (gains, w_in_b, wgate, rows(bias), conv_qk.astype(F32), pmat, bf(w_pool),
                    mixvecs, mixvecs, bf(w_out), gains)
    memkv_params = (gains, bf(wk_x), bf(wv_x))
    xattn_params = (gains, bf(wq_x), bf(wo_x), gains)
    ffn_params = (gains, bf(w_gate), bf(w_up), bf(w_down), gains)

    for l in range(depth):
        x2 = _mixer(l, x2, *mixer_params, S)
        k, v = _memkv(l, mem, *memkv_params)
        x2 = _xattn(l, x2, k, v, *xattn_params, B, S)
        x2 = _ffn(l, x2, *ffn_params)

    return x2.reshape(B, S, D)
```

```python
import functools

import numpy as np
import jax
import jax.numpy as jnp
from jax import lax
from jax.experimental import pallas as pl
from jax.experimental.pallas import tpu as pltpu

F32 = jnp.float32
BF16 = jnp.bfloat16

EPS = 1e-6
D_MODEL = 1024
N_MEM = 256
POOL_WIDTH = 512
POOL_GROUPS = 4
POOL_GROUP_DIM = 128
POOL_WINDOWS = (2, 4, 8, 16)
MLSTM_WIDTH = 512
MLSTM_HEADS = 4
MLSTM_HEAD_DIM = 128
CONV_WIDTH = 4
XATTN_HEADS = 4
XATTN_HEAD_DIM = 256
OFF_Q = POOL_WIDTH
OFF_K = OFF_Q + MLSTM_WIDTH
OFF_V = OFF_K + MLSTM_WIDTH
OFF_O = OFF_V + MLSTM_WIDTH
OFF_GATE = OFF_O + MLSTM_WIDTH

V7X_LANES = 128
V7X_SUBLANES = 8
V7X_VMEM_LIMIT_BYTES = 56 * 1024 * 1024

G_MIX_PRE, G_MIX_POST, G_MEM, G_XATTN_PRE, G_XATTN_POST, G_FFN_PRE, G_FFN_POST = range(7)
V_POOL_SCALE, V_MLSTM_NORM = range(2)

IN_COLS_PAD = OFF_GATE + V7X_LANES
POOL_HALO = 16
CONV_HALO = 8
POOL_BLOCK = 256
PIECE_COLS = 256

TS_MIX = 512
L_CHUNK = 256
TS_ATT = 2048
TM_FFN = 1024
SUB_ROWS = 512
TF_FFN = 256


def _rmsnorm(x, g):
    return x * lax.rsqrt(jnp.mean(x * x, axis=-1, keepdims=True) + EPS) * g


def _sigmoid(x):
    return 1.0 / (1.0 + jnp.exp(-x))


def _log_sigmoid(x):
    return jnp.minimum(x, 0.0) - jnp.log(1.0 + jnp.exp(-jnp.abs(x)))


def _dot(a, b):
    return jnp.dot(a, b, preferred_element_type=F32)


def _dot_nt(a, b):
    return lax.dot_general(a, b, (((1,), (1,)), ((), ())), preferred_element_type=F32)


def _dot_tn(a, b):
    return lax.dot_general(a, b, (((0,), (0,)), ((), ())), preferred_element_type=F32)


def _const_spec(shape):
    nd = len(shape)
    return pl.BlockSpec(shape, lambda *_: (0,) * nd, pipeline_mode=pl.Buffered(1))


def _vector_spec(n, l, k):
    return pl.BlockSpec((None, None, 1, n), lambda *_: (l, k, 0, 0), pipeline_mode=pl.Buffered(1))


def _layer_spec(shape, l):
    nd = len(shape)
    return pl.BlockSpec((None,) + tuple(shape), lambda *_: (l,) + (0,) * nd, pipeline_mode=pl.Buffered(1))


def _pool_matrices():
    i = np.arange(POOL_BLOCK)[:, None]
    j = np.arange(POOL_BLOCK)[None, :]
    mats = []
    for win in POOL_WINDOWS:
        inside = (j <= i) & (j > i - win)
        mats.append(inside.astype(np.float32) / win - (i == j).astype(np.float32))
    return jnp.asarray(np.stack(mats), dtype=BF16)


def _mixer_kernel(*refs, nt, ntot):
    s = pl.program_id(0)

    def variant(step, stages):
        return functools.partial(_mixer_step, step % 2, 1 - step % 2, *refs, nt=nt, stages=stages)

    steady = jnp.logical_and(s >= 1, s <= ntot)
    pl.when(s == 0)(variant(0, ("proj",)))
    pl.when(jnp.logical_and(steady, lax.rem(s, 2) == 0))(variant(0, ("proj", "mix", "out")))
    pl.when(jnp.logical_and(steady, lax.rem(s, 2) == 1))(variant(1, ("proj", "mix", "out")))
    pl.when(s == ntot + 1)(variant(ntot + 1, ("out",)))


def _mixer_step(a, b, xn_ref, xb_ref, gpre_ref, win_ref, wgate_ref, bias_ref, conv_ref, pmat_ref, wpool_ref,
                pscale_ref, gnorm_ref, wout_ref, gpost_ref, o_ref,
                u0_ref, u1_ref, qk0_ref, qk1_ref, v0_ref, v1_ref, og0_ref, og1_ref,
                g0_ref, g1_ref, mix0_ref, mix1_ref, c_ref, m_ref, *, nt, stages):
    uext_a, uext_b = (u0_ref, u1_ref)[a], (u0_ref, u1_ref)[b]
    qkext_a, qkext_b = (qk0_ref, qk1_ref)[a], (qk0_ref, qk1_ref)[b]
    v_a, v_b = (v0_ref, v1_ref)[a], (v0_ref, v1_ref)[b]
    og_a, og_b = (og0_ref, og1_ref)[a], (og0_ref, og1_ref)[b]
    g_a, g_b = (g0_ref, g1_ref)[a], (g0_ref, g1_ref)[b]
    mix_a, mix_b = (mix0_ref, mix1_ref)[a], (mix0_ref, mix1_ref)[b]
    ts = xn_ref.shape[0]
    L = L_CHUNK
    H = MLSTM_HEADS
    DH = MLSTM_HEAD_DIM
    PC = PIECE_COLS
    s = pl.program_id(0)
    first_a = lax.rem(s, nt) == 0
    tb = lax.rem(s + nt - 1, nt)
    first_b = tb == 0

    hb = _rmsnorm(xn_ref[...], gpre_ref[...]).astype(BF16) if "proj" in stages else None

    out_parts = []

    def out_piece(j):
        def run():
            out_parts.append(_dot(mix_a[...], wout_ref[:, j * PC:(j + 1) * PC]))
            if len(out_parts) == D_MODEL // PC:
                ss = sum(jnp.sum(y * y, axis=1, keepdims=True) for y in out_parts)
                rs = lax.rsqrt(ss * (1.0 / D_MODEL) + EPS)
                for jj, y in enumerate(out_parts):
                    cols = slice(jj * PC, (jj + 1) * PC)
                    o_ref[:, cols] = xb_ref[:, cols] + y * rs * gpost_ref[:, cols]
        return run

    def proj_piece(c0):
        def run():
            w = win_ref[:, c0:c0 + PC] if c0 < OFF_GATE else wgate_ref[...]
            y = _dot(hb, w)
            if c0 < OFF_Q:
                uext_a[POOL_HALO:POOL_HALO + ts, c0:c0 + PC] = y
            elif c0 < OFF_V:
                qkext_a[CONV_HALO:CONV_HALO + ts, c0 - OFF_Q:c0 - OFF_Q + PC] = y
            elif c0 < OFF_O:
                v_a[:, c0 - OFF_V:c0 - OFF_V + PC] = y.astype(BF16)
            elif c0 < OFF_GATE:
                og_a[:, c0 - OFF_O:c0 - OFF_O + PC] = y
            else:
                g_a[...] = y + bias_ref[...]
        return run

    pieces = []
    if "out" in stages:
        pieces += [out_piece(j) for j in range(D_MODEL // PC)]
    if "proj" in stages:
        pieces += [proj_piece(c0) for c0 in range(0, IN_COLS_PAD, PC)]
    pieces.reverse()

    def issue(n=1):
        for _ in range(n):
            if pieces:
                pieces.pop()()

    def history_rows():
        if "mix" in stages:
            uext_a[0:POOL_HALO, :] = jnp.where(first_a, 0.0, uext_b[ts:ts + POOL_HALO, :])
            qkext_a[0:CONV_HALO, :] = jnp.where(first_a, 0.0, qkext_b[ts:ts + CONV_HALO, :])
        else:
            uext_a[0:POOL_HALO, :] = jnp.zeros((POOL_HALO, POOL_WIDTH), F32)
            qkext_a[0:CONV_HALO, :] = jnp.zeros((CONV_HALO, 2 * MLSTM_WIDTH), F32)

    if "mix" not in stages:
        issue(len(pieces))
        if "proj" in stages:
            history_rows()
            for ref in (mix_b, c_ref, m_ref):
                ref[...] = jnp.zeros_like(ref)
        return

    def pool_window():
        d_mains = []
        for blk in range(ts // POOL_BLOCK):
            r0 = blk * POOL_BLOCK
            for g in range(POOL_GROUPS):
                cols = slice(g * POOL_GROUP_DIM, (g + 1) * POOL_GROUP_DIM)
                ublk = uext_b[POOL_HALO + r0:POOL_HALO + r0 + POOL_BLOCK, cols]
                d_mains.append(_dot(pmat_ref[g], ublk.astype(BF16)))
        return d_mains

    def pool_mix(d_mains):
        for blk in range(ts // POOL_BLOCK):
            r0 = blk * POOL_BLOCK
            pos = tb * ts + r0 + lax.broadcasted_iota(jnp.int32, (POOL_HALO, 1), 0)
            for g, win in enumerate(POOL_WINDOWS):
                cols = slice(g * POOL_GROUP_DIM, (g + 1) * POOL_GROUP_DIM)
                cur = uext_b[POOL_HALO + r0:POOL_HALO + r0 + POOL_HALO, cols]
                acc = cur
                for k in range(1, win):
                    acc = acc + uext_b[POOL_HALO + r0 - k:POOL_HALO + r0 - k + POOL_HALO, cols]
                count = jnp.minimum(pos + 1, win).astype(F32)
                d_head = acc / count - cur
                d = jnp.concatenate([d_head, d_mains[blk * POOL_GROUPS + g][POOL_HALO:, :]], axis=0)
                y = _dot(d.astype(BF16), wpool_ref[g]) * pscale_ref[:, cols]
                mix_b[r0:r0 + POOL_BLOCK, cols] = y.astype(BF16)

    ri = lax.broadcasted_iota(jnp.int32, (L, L), 0)
    ci = lax.broadcasted_iota(jnp.int32, (L, L), 1)
    causal = ci <= ri
    lane_x = lax.broadcasted_iota(jnp.int32, (L, V7X_LANES), 1)
    sub8 = lax.broadcasted_iota(jnp.int32, (V7X_SUBLANES, L), 0)
    lane8 = lax.broadcasted_iota(jnp.int32, (V7X_SUBLANES, L), 1)
    zpad = jnp.zeros((V7X_LANES - V7X_SUBLANES, L), F32)
    onehot = [(lane_x == h).astype(BF16) for h in range(H)]
    cw = conv_ref[...]

    cs = [jnp.where(first_b, 0.0, c_ref[h]) for h in range(H)]
    m_row = jnp.where(first_b, 0.0, m_ref[0:1, :])

    def conv_silu(col0, r0):
        acc = None
        for j in range(CONV_WIDTH):
            off = CONV_HALO + r0 - (CONV_WIDTH - 1) + j
            term = cw[j:j + 1, col0:col0 + DH] * qkext_b[off:off + L, col0:col0 + DH]
            acc = term if acc is None else acc + term
        return acc * _sigmoid(acc)

    def lane_scan(x, op, fill):
        sh = 1
        while sh < L:
            x = op(x, jnp.where(lane8 >= sh, pltpu.roll(x, sh, axis=1), fill))
            sh *= 2
        return x

    def gate_prep(r0, m_row):
        G = g_b[r0:r0 + L, :]
        R8 = G.T[0:V7X_SUBLANES, :]
        low = sub8 < H
        B8 = lane_scan(jnp.where(low, _log_sigmoid(R8), 0.0), jnp.add, 0.0)
        C8 = jnp.where(low, pltpu.roll(R8, H, axis=0) - B8, 0.0)
        M8 = lane_scan(C8, jnp.maximum, -jnp.inf)
        T8 = jnp.where(low, B8, pltpu.roll(M8, H, axis=0))
        TX = jnp.concatenate([T8, zpad], axis=0).T
        valid = lane_x < H
        bX = jnp.where(valid, TX, 0.0)
        cmX = jnp.where(valid, pltpu.roll(TX, V7X_LANES - H, axis=1), 0.0)
        igX = jnp.where(valid, pltpu.roll(G, V7X_LANES - H, axis=1), 0.0)
        mmaxX = jnp.maximum(cmX, m_row)
        bL = bX[L - 1:L, :]
        gkX = bL - bX + igX
        m_new = jnp.maximum(bL + m_row, jnp.max(gkX, axis=0, keepdims=True))
        return dict(C8=C8, mmaxX=mmaxX, m_new=m_new,
                    w_interX=jnp.exp(m_row - mmaxX),
                    enegX=jnp.exp(-(bX + mmaxX)),
                    decay=jnp.exp(bL + m_row - m_new),
                    wkX=jnp.exp(gkX - m_new))

    issue()
    gps = [gate_prep(0, m_row)]
    issue()
    d_mains = pool_window()
    gps.append(gate_prep(L, gps[0]["m_new"]))
    issue()
    pool_mix(d_mains)
    issue()

    for c in range(ts // L):
        r0 = c * L
        C8, mmaxX, w_interX, enegX, decay, wkX = (gps[c][n] for n in
                                                 ("C8", "mmaxX", "w_interX", "enegX", "decay", "wkX"))
        per_head = []
        rsX = None
        qnX = None
        for h in range(H):
            hc = h * DH
            q = conv_silu(hc, r0) * (DH ** -0.5)
            k = conv_silu(MLSTM_WIDTH + hc, r0)
            v_aug = jnp.concatenate([v_b[r0:r0 + L, hc:hc + DH], onehot[h]], axis=1)
            qb = q.astype(BF16)
            S = _dot_nt(qb, k.astype(BF16))
            issue()
            E = jnp.exp(jnp.where(causal, C8[h:h + 1, :] - mmaxX[:, h:h + 1], -jnp.inf))
            av = _dot((S * E).astype(BF16), v_aug)
            qc = _dot(qb, cs[h].astype(BF16))
            rsX = av[:, DH:] if rsX is None else rsX + av[:, DH:]
            qnX = qc[:, DH:] if qnX is None else qnX + qc[:, DH:]
            per_head.append(av[:, :DH] + w_interX[:, h:h + 1] * qc[:, :DH])
            kw = wkX[:, h:h + 1] * k
            cs[h] = decay[:, h:h + 1] * cs[h] + _dot_tn(kw.astype(BF16), v_aug)
            if c > 0 and h % 2 == 1:
                issue()

        denX = rsX + w_interX * qnX
        rX = 1.0 / jnp.maximum(jnp.abs(denX), enegX)

        for h in range(H):
            hc = h * DH
            hout = per_head[h] * rX[:, h:h + 1]
            hn = hout * lax.rsqrt(jnp.mean(hout * hout, axis=1, keepdims=True) + EPS)
            og = _sigmoid(og_b[r0:r0 + L, hc:hc + DH])
            mout = hn * gnorm_ref[:, hc:hc + DH] * og
            mix_b[r0:r0 + L, POOL_WIDTH + hc:POOL_WIDTH + hc + DH] = mout.astype(BF16)
            if c == 0 and h == 1:
                issue()
    m_row = gps[-1]["m_new"]
    assert not pieces, "every projection piece must have an issue point"
    if "proj" in stages:
        history_rows()

    for h in range(H):
        c_ref[h] = cs[h]
    m_ref[0:1, :] = m_row


def _mixer(l, x2, gpre, win, wgate, bias, conv, pmat, wpool, pscale, gnorm, wout, gpost, seq):
    T = x2.shape[0]
    ts = TS_MIX
    nt = seq // ts
    ntot = T // ts
    return pl.pallas_call(
        functools.partial(_mixer_kernel, nt=nt, ntot=ntot),
        grid=(ntot + 2,),
        in_specs=[
            pl.BlockSpec((ts, D_MODEL), lambda s: (jnp.minimum(s, ntot - 1), 0)),
            pl.BlockSpec((ts, D_MODEL), lambda s: (jnp.maximum(s - 2, 0), 0)),
            _vector_spec(D_MODEL, l, G_MIX_PRE),
            _layer_spec((D_MODEL, OFF_GATE), l),
            _layer_spec((D_MODEL, V7X_LANES), l),
            _layer_spec((1, V7X_LANES), l),
            _layer_spec((CONV_WIDTH, 2 * MLSTM_WIDTH), l),
            _const_spec((POOL_GROUPS, POOL_BLOCK, POOL_BLOCK)),
            _layer_spec((POOL_GROUPS, POOL_GROUP_DIM, POOL_GROUP_DIM), l),
            _vector_spec(POOL_WIDTH, l, V_POOL_SCALE),
            _vector_spec(MLSTM_WIDTH, l, V_MLSTM_NORM),
            _layer_spec((D_MODEL, D_MODEL), l),
            _vector_spec(D_MODEL, l, G_MIX_POST),
        ],
        out_specs=pl.BlockSpec((ts, D_MODEL), lambda s: (jnp.maximum(s - 2, 0), 0)),
        out_shape=jax.ShapeDtypeStruct((T, D_MODEL), F32),
        scratch_shapes=[
            *[pltpu.VMEM((ts + POOL_HALO, POOL_WIDTH), F32)] * 2,
            *[pltpu.VMEM((ts + CONV_HALO, 2 * MLSTM_WIDTH), F32)] * 2,
            *[pltpu.VMEM((ts, MLSTM_WIDTH), BF16)] * 2,
            *[pltpu.VMEM((ts, MLSTM_WIDTH), F32)] * 2,
            *[pltpu.VMEM((ts, V7X_LANES), F32)] * 2,
            *[pltpu.VMEM((ts, D_MODEL), BF16)] * 2,
            pltpu.VMEM((MLSTM_HEADS, MLSTM_HEAD_DIM, 2 * MLSTM_HEAD_DIM), F32),
            pltpu.VMEM((V7X_SUBLANES, V7X_LANES), F32),
        ],
        compiler_params=pltpu.CompilerParams(
            dimension_semantics=("arbitrary",), vmem_limit_bytes=V7X_VMEM_LIMIT_BYTES),
        name="mixer",
    )(x2, x2, gpre, win, wgate, bias, conv, pmat, wpool, pscale, gnorm, wout, gpost)


def _memkv_kernel(mem_ref, g_ref, wk_ref, wv_ref, k_ref, v_ref):
    mb = _rmsnorm(mem_ref[0], g_ref[...]).astype(BF16)
    k_ref[0] = _dot(mb, wk_ref[...]).astype(BF16)
    v_ref[0] = _dot(mb, wv_ref[...]).astype(BF16)


def _memkv(l, mem, g, wk, wv):
    B = mem.shape[0]
    blk = pl.BlockSpec((1, N_MEM, D_MODEL), lambda b: (b, 0, 0))
    return pl.pallas_call(
        _memkv_kernel,
        grid=(B,),
        in_specs=[blk, _vector_spec(D_MODEL, l, G_MEM), _layer_spec((D_MODEL, D_MODEL), l),
                  _layer_spec((D_MODEL, D_MODEL), l)],
        out_specs=[blk, blk],
        out_shape=[jax.ShapeDtypeStruct((B, N_MEM, D_MODEL), BF16)] * 2,
        compiler_params=pltpu.CompilerParams(
            dimension_semantics=("arbitrary",), vmem_limit_bytes=V7X_VMEM_LIMIT_BYTES),
        name="memkv",
    )(mem, g, wk, wv)


def _xattn_kernel(x_ref, k_ref, v_ref, gpre_ref, wq_ref, wo_ref, gpost_ref, o_ref, att_ref):
    halves = [slice(r0, r0 + SUB_ROWS) for r0 in range(0, x_ref.shape[0], SUB_ROWS)]
    qbs = []
    for rows in halves:
        hb = _rmsnorm(x_ref[rows, :], gpre_ref[...]).astype(BF16)
        qbs.append((_dot(hb, wq_ref[...]) * (XATTN_HEAD_DIM ** -0.5)).astype(BF16))
    for h in range(XATTN_HEADS):
        cols = slice(h * XATTN_HEAD_DIM, (h + 1) * XATTN_HEAD_DIM)
        scores = [_dot_nt(qb[:, cols], k_ref[0, :, cols]) for qb in qbs]
        for rows, s in zip(halves, scores):
            p = jnp.exp(s - jnp.max(s, axis=1, keepdims=True))
            o = _dot(p.astype(BF16), v_ref[0, :, cols]) / jnp.sum(p, axis=1, keepdims=True)
            att_ref[rows, cols] = o.astype(BF16)
    for rows in halves:
        y = _dot(att_ref[rows, :], wo_ref[...])
        o_ref[rows, :] = x_ref[rows, :] + _rmsnorm(y, gpost_ref[...])


def _xattn(l, x2, k, v, gpre, wq, wo, gpost, batch, seq):
    T = x2.shape[0]
    ts = TS_ATT
    nt = seq // ts
    tok = lambda b, t: (b * nt + t, 0)
    kvb = pl.BlockSpec((1, N_MEM, D_MODEL), lambda b, t: (b, 0, 0))
    return pl.pallas_call(
        _xattn_kernel,
        grid=(batch, nt),
        in_specs=[
            pl.BlockSpec((ts, D_MODEL), tok), kvb, kvb,
            _vector_spec(D_MODEL, l, G_XATTN_PRE),
            _layer_spec((D_MODEL, D_MODEL), l),
            _layer_spec((D_MODEL, D_MODEL), l),
            _vector_spec(D_MODEL, l, G_XATTN_POST),
        ],
        out_specs=pl.BlockSpec((ts, D_MODEL), tok),
        out_shape=jax.ShapeDtypeStruct((T, D_MODEL), F32),
        scratch_shapes=[pltpu.VMEM((ts, D_MODEL), BF16)],
        compiler_params=pltpu.CompilerParams(
            dimension_semantics=("arbitrary", "arbitrary"), vmem_limit_bytes=V7X_VMEM_LIMIT_BYTES),
        name="xattn",
    )(x2, k, v, gpre, wq, wo, gpost)


def _ffn_kernel(x_ref, gpre_ref, wg_ref, wu_ref, wd_ref, gpost_ref, o_ref):
    tf = TF_FFN
    halves = [slice(r0, r0 + SUB_ROWS) for r0 in range(0, x_ref.shape[0], SUB_ROWS)]
    hbs = [_rmsnorm(x_ref[rows, :], gpre_ref[...]).astype(BF16) for rows in halves]
    accs = [None] * len(halves)

    def down(i, a, cols):
        contrib = _dot(a, wd_ref[cols, :])
        accs[i] = contrib if accs[i] is None else accs[i] + contrib

    pending = None
    for c in range(wg_ref.shape[1] // tf):
        cols = slice(c * tf, (c + 1) * tf)
        for i, hb in enumerate(hbs):
            gate = _dot(hb, wg_ref[:, cols])
            a = (gate * _sigmoid(gate) * _dot(hb, wu_ref[:, cols])).astype(BF16)
            if pending is not None:
                down(*pending)
            pending = (i, a, cols)
    down(*pending)
    for rows, acc in zip(halves, accs):
        o_ref[rows, :] = x_ref[rows, :] + _rmsnorm(acc, gpost_ref[...])


def _ffn(l, x2, gpre, wg, wu, wd, gpost):
    T = x2.shape[0]
    tm = TM_FFN
    d_ff = wg.shape[-1]
    return pl.pallas_call(
        _ffn_kernel,
        grid=(T // tm,),
        in_specs=[
            pl.BlockSpec((tm, D_MODEL), lambda i: (i, 0)),
            _vector_spec(D_MODEL, l, G_FFN_PRE),
            _layer_spec((D_MODEL, d_ff), l),
            _layer_spec((D_MODEL, d_ff), l),
            _layer_spec((d_ff, D_MODEL), l),
            _vector_spec(D_MODEL, l, G_FFN_POST),
        ],
        out_specs=pl.BlockSpec((tm, D_MODEL), lambda i: (i, 0)),
        out_shape=jax.ShapeDtypeStruct((T, D_MODEL), F32),
        compiler_params=pltpu.CompilerParams(
            dimension_semantics=("arbitrary",), vmem_limit_bytes=V7X_VMEM_LIMIT_BYTES),
        name="ffn",
    )(x2, gpre, wg, wu, wd, gpost)


def kernel(x, mem, w_in, b_gate, conv_qk, w_pool, pool_scale, mlstm_norm_g, w_out, g_mix_pre, g_mix_post, g_mem, g_xattn_pre, g_xattn_post, wq_x, wk_x, wv_x, wo_x, g_ffn_pre, g_ffn_post, w_gate, w_up, w_down):
    B, S, D = x.shape
    depth = w_in.shape[0]
    H = MLSTM_HEADS
    x2 = x.reshape(B * S, D)
    pmat = _pool_matrices()
    rows = lambda v: v.reshape(depth, 1, -1).astype(F32)
    table = lambda vs: jnp.stack([v.astype(F32) for v in vs], axis=1)[:, :, None, :]
    gains = table([g_mix_pre, g_mix_post, g_mem, g_xattn_pre, g_xattn_post, g_ffn_pre, g_ffn_post])
    mixvecs = table([pool_scale, mlstm_norm_g])
    bf = lambda w: w.astype(BF16)
    w_in_b = bf(w_in)
    wgate = jnp.concatenate(
        [w_in_b[:, :, OFF_GATE + H:], w_in_b[:, :, OFF_GATE:OFF_GATE + H],
         jnp.zeros((depth, D, V7X_LANES - 2 * H), BF16)], axis=2)
    bias = jnp.concatenate(
        [b_gate[:, H:], b_gate[:, :H], jnp.zeros((depth, V7X_LANES - 2 * H), b_gate.dtype)], axis=1)
    mixer_params = (gains, w_in_b, wgate, rows(bias), conv_qk.astype(F32), pmat, bf(w_pool),
                    mixvecs, mixvecs, bf(w_out), gains)
    memkv_params = (gains, bf(wk_x), bf(wv_x))
    xattn_params = (gains, bf(wq_x), bf(wo_x), gains)
    ffn_params = (gains, bf(w_gate), bf(w_up), bf(w_down), gains)

    for l in range(depth):
        x2 = _mixer(l, x2, *mixer_params, S)
        k, v = _memkv(l, mem, *memkv_params)
        x2 = _xattn(l, x2, k, v, *xattn_params, B, S)
        x2 = _ffn(l, x2, *ffn_params)

    return x2.reshape(B, S, D)
```

```python
import functools

import numpy as np
import jax
import jax.numpy as jnp
from jax import lax
from jax.experimental import pallas as pl
from jax.experimental.pallas import tpu as pltpu

F32 = jnp.float32
BF16 = jnp.bfloat16

EPS = 1e-6
D_MODEL = 1024
N_MEM = 256
POOL_WIDTH = 512
POOL_GROUPS = 4
POOL_GROUP_DIM = 128
POOL_WINDOWS = (2, 4, 8, 16)
MLSTM_WIDTH = 512
MLSTM_HEADS = 4
MLSTM_HEAD_DIM = 128
CONV_WIDTH = 4
XATTN_HEADS = 4
XATTN_HEAD_DIM = 256
OFF_Q = POOL_WIDTH
OFF_K = OFF_Q + MLSTM_WIDTH
OFF_V = OFF_K + MLSTM_WIDTH
OFF_O = OFF_V + MLSTM_WIDTH
OFF_GATE = OFF_O + MLSTM_WIDTH

V7X_LANES = 128
V7X_SUBLANES = 8
V7X_VMEM_LIMIT_BYTES = 56 * 1024 * 1024

G_MIX_PRE, G_MIX_POST, G_MEM, G_XATTN_PRE, G_XATTN_POST, G_FFN_PRE, G_FFN_POST = range(7)
V_POOL_SCALE, V_MLSTM_NORM = range(2)

IN_COLS_PAD = OFF_GATE + V7X_LANES
POOL_HALO = 16
CONV_HALO = 8
POOL_BLOCK = 256
PIECE_COLS = 256

TS_MIX = 512
L_CHUNK = 256
TS_ATT = 2048
TM_FFN = 1024
SUB_ROWS = 512
TF_FFN = 256


def _rmsnorm(x, g):
    return x * lax.rsqrt(jnp.mean(x * x, axis=-1, keepdims=True) + EPS) * g


def _sigmoid(x):
    return 1.0 / (1.0 + jnp.exp(-x))


def _log_sigmoid(x):
    return jnp.minimum(x, 0.0) - jnp.log(1.0 + jnp.exp(-jnp.abs(x)))


def _dot(a, b):
    return jnp.dot(a, b, preferred_element_type=F32)


def _dot_w(a, w):
    return lax.dot_general(a, w, (((1,), (0,)), ((), ())), preferred_element_type=F32)


def _dot_nt(a, b):
    return lax.dot_general(a, b, (((1,), (1,)), ((), ())), preferred_element_type=F32)


def _dot_tn(a, b):
    return lax.dot_general(a, b, (((0,), (0,)), ((), ())), preferred_element_type=F32)


def _const_spec(shape):
    nd = len(shape)
    return pl.BlockSpec(shape, lambda *_: (0,) * nd, pipeline_mode=pl.Buffered(1))


def _vector_spec(n, l, k):
    return pl.BlockSpec((None, None, 1, n), lambda *_: (l, k, 0, 0), pipeline_mode=pl.Buffered(1))


def _layer_spec(shape, l):
    nd = len(shape)
    return pl.BlockSpec((None,) + tuple(shape), lambda *_: (l,) + (0,) * nd, pipeline_mode=pl.Buffered(1))


def _pool_matrices():
    i = np.arange(POOL_BLOCK)[:, None]
    j = np.arange(POOL_BLOCK)[None, :]
    mats = []
    for win in POOL_WINDOWS:
        inside = (j <= i) & (j > i - win)
        mats.append(inside.astype(np.float32) / win - (i == j).astype(np.float32))
    return jnp.asarray(np.stack(mats), dtype=BF16)


def _mixer_kernel(*refs, nt, ntot):
    s = pl.program_id(0)

    def variant(step, stages):
        return functools.partial(_mixer_step, step % 2, 1 - step % 2, *refs, nt=nt, stages=stages)

    steady = jnp.logical_and(s >= 1, s <= ntot)
    pl.when(s == 0)(variant(0, ("proj",)))
    pl.when(jnp.logical_and(steady, lax.rem(s, 2) == 0))(variant(0, ("proj", "mix", "out")))
    pl.when(jnp.logical_and(steady, lax.rem(s, 2) == 1))(variant(1, ("proj", "mix", "out")))
    pl.when(s == ntot + 1)(variant(ntot + 1, ("out",)))


def _mixer_step(a, b, xn_ref, xb_ref, gpre_ref, win_ref, wgate_ref, bias_ref, conv_ref, pmat_ref, wpool_ref,
                pscale_ref, gnorm_ref, wout_ref, gpost_ref, o_ref,
                u0_ref, u1_ref, qk0_ref, qk1_ref, v0_ref, v1_ref, og0_ref, og1_ref,
                g0_ref, g1_ref, mix0_ref, mix1_ref, c_ref, m_ref, *, nt, stages):
    uext_a, uext_b = (u0_ref, u1_ref)[a], (u0_ref, u1_ref)[b]
    qkext_a, qkext_b = (qk0_ref, qk1_ref)[a], (qk0_ref, qk1_ref)[b]
    v_a, v_b = (v0_ref, v1_ref)[a], (v0_ref, v1_ref)[b]
    og_a, og_b = (og0_ref, og1_ref)[a], (og0_ref, og1_ref)[b]
    g_a, g_b = (g0_ref, g1_ref)[a], (g0_ref, g1_ref)[b]
    mix_a, mix_b = (mix0_ref, mix1_ref)[a], (mix0_ref, mix1_ref)[b]
    ts = xn_ref.shape[0]
    L = L_CHUNK
    H = MLSTM_HEADS
    DH = MLSTM_HEAD_DIM
    PC = PIECE_COLS
    s = pl.program_id(0)
    first_a = lax.rem(s, nt) == 0
    tb = lax.rem(s + nt - 1, nt)
    first_b = tb == 0

    hb = _rmsnorm(xn_ref[...], gpre_ref[...]).astype(BF16) if "proj" in stages else None

    out_parts = []

    def out_piece(j):
        def run():
            out_parts.append(_dot(mix_a[...], wout_ref[:, j * PC:(j + 1) * PC]))
            if len(out_parts) == D_MODEL // PC:
                ss = sum(jnp.sum(y * y, axis=1, keepdims=True) for y in out_parts)
                rs = lax.rsqrt(ss * (1.0 / D_MODEL) + EPS)
                for jj, y in enumerate(out_parts):
                    cols = slice(jj * PC, (jj + 1) * PC)
                    o_ref[:, cols] = xb_ref[:, cols] + y * rs * gpost_ref[:, cols]
        return run

    def proj_piece(c0):
        def run():
            w = win_ref[:, c0:c0 + PC] if c0 < OFF_GATE else wgate_ref[...]
            y = _dot(hb, w)
            if c0 < OFF_Q:
                uext_a[POOL_HALO:POOL_HALO + ts, c0:c0 + PC] = y
            elif c0 < OFF_V:
                qkext_a[CONV_HALO:CONV_HALO + ts, c0 - OFF_Q:c0 - OFF_Q + PC] = y
            elif c0 < OFF_O:
                v_a[:, c0 - OFF_V:c0 - OFF_V + PC] = y.astype(BF16)
            elif c0 < OFF_GATE:
                og_a[:, c0 - OFF_O:c0 - OFF_O + PC] = y
            else:
                g_a[...] = y + bias_ref[...]
        return run

    pieces = []
    if "out" in stages:
        pieces += [out_piece(j) for j in range(D_MODEL // PC)]
    if "proj" in stages:
        pieces += [proj_piece(c0) for c0 in range(0, IN_COLS_PAD, PC)]
    pieces.reverse()

    def issue(n=1):
        for _ in range(n):
            if pieces:
                pieces.pop()()

    def history_rows():
        if "mix" in stages:
            uext_a[0:POOL_HALO, :] = jnp.where(first_a, 0.0, uext_b[ts:ts + POOL_HALO, :])
            qkext_a[0:CONV_HALO, :] = jnp.where(first_a, 0.0, qkext_b[ts:ts + CONV_HALO, :])
        else:
            uext_a[0:POOL_HALO, :] = jnp.zeros((POOL_HALO, POOL_WIDTH), F32)
            qkext_a[0:CONV_HALO, :] = jnp.zeros((CONV_HALO, 2 * MLSTM_WIDTH), F32)

    if "mix" not in stages:
        issue(len(pieces))
        if "proj" in stages:
            history_rows()
            for ref in (mix_b, c_ref, m_ref):
                ref[...] = jnp.zeros_like(ref)
        return

    def pool_window():
        d_mains = []
        for blk in range(ts // POOL_BLOCK):
            r0 = blk * POOL_BLOCK
            for g in range(POOL_GROUPS):
                cols = slice(g * POOL_GROUP_DIM, (g + 1) * POOL_GROUP_DIM)
                ublk = uext_b[POOL_HALO + r0:POOL_HALO + r0 + POOL_BLOCK, cols]
                d_mains.append(_dot(pmat_ref[g], ublk.astype(BF16)))
        return d_mains

    def pool_mix(d_mains):
        for blk in range(ts // POOL_BLOCK):
            r0 = blk * POOL_BLOCK
            pos = tb * ts + r0 + lax.broadcasted_iota(jnp.int32, (POOL_HALO, 1), 0)
            for g, win in enumerate(POOL_WINDOWS):
                cols = slice(g * POOL_GROUP_DIM, (g + 1) * POOL_GROUP_DIM)
                cur = uext_b[POOL_HALO + r0:POOL_HALO + r0 + POOL_HALO, cols]
                acc = cur
                for k in range(1, win):
                    acc = acc + uext_b[POOL_HALO + r0 - k:POOL_HALO + r0 - k + POOL_HALO, cols]
                count = jnp.minimum(pos + 1, win).astype(F32)
                d_head = acc / count - cur
                d = jnp.concatenate([d_head, d_mains[blk * POOL_GROUPS + g][POOL_HALO:, :]], axis=0)
                y = _dot(d.astype(BF16), wpool_ref[g]) * pscale_ref[:, cols]
                mix_b[r0:r0 + POOL_BLOCK, cols] = y.astype(BF16)

    ri = lax.broadcasted_iota(jnp.int32, (L, L), 0)
    ci = lax.broadcasted_iota(jnp.int32, (L, L), 1)
    causal = ci <= ri
    lane_x = lax.broadcasted_iota(jnp.int32, (L, V7X_LANES), 1)
    sub8 = lax.broadcasted_iota(jnp.int32, (V7X_SUBLANES, L), 0)
    lane8 = lax.broadcasted_iota(jnp.int32, (V7X_SUBLANES, L), 1)
    zpad = jnp.zeros((V7X_LANES - V7X_SUBLANES, L), F32)
    onehot = [(lane_x == h).astype(BF16) for h in range(H)]
    cw = conv_ref[...]

    cs = [jnp.where(first_b, 0.0, c_ref[h]) for h in range(H)]
    m_row = jnp.where(first_b, 0.0, m_ref[0:1, :])

    def conv_silu(col0, r0):
        acc = None
        for j in range(CONV_WIDTH):
            off = CONV_HALO + r0 - (CONV_WIDTH - 1) + j
            term = cw[j:j + 1, col0:col0 + DH] * qkext_b[off:off + L, col0:col0 + DH]
            acc = term if acc is None else acc + term
        return acc * _sigmoid(acc)

    def lane_scan(x, op, fill):
        sh = 1
        while sh < L:
            x = op(x, jnp.where(lane8 >= sh, pltpu.roll(x, sh, axis=1), fill))
            sh *= 2
        return x

    def gate_prep(r0, m_row):
        G = g_b[r0:r0 + L, :]
        R8 = G.T[0:V7X_SUBLANES, :]
        low = sub8 < H
        B8 = lane_scan(jnp.where(low, _log_sigmoid(R8), 0.0), jnp.add, 0.0)
        C8 = jnp.where(low, pltpu.roll(R8, H, axis=0) - B8, 0.0)
        M8 = lane_scan(C8, jnp.maximum, -jnp.inf)
        T8 = jnp.where(low, B8, pltpu.roll(M8, H, axis=0))
        TX = jnp.concatenate([T8, zpad], axis=0).T
        valid = lane_x < H
        bX = jnp.where(valid, TX, 0.0)
        cmX = jnp.where(valid, pltpu.roll(TX, V7X_LANES - H, axis=1), 0.0)
        igX = jnp.where(valid, pltpu.roll(G, V7X_LANES - H, axis=1), 0.0)
        mmaxX = jnp.maximum(cmX, m_row)
        bL = bX[L - 1:L, :]
        gkX = bL - bX + igX
        m_new = jnp.maximum(bL + m_row, jnp.max(gkX, axis=0, keepdims=True))
        return dict(C8=C8, mmaxX=mmaxX, m_new=m_new,
                    w_interX=jnp.exp(m_row - mmaxX),
                    enegX=jnp.exp(-(bX + mmaxX)),
                    decay=jnp.exp(bL + m_row - m_new),
                    wkX=jnp.exp(gkX - m_new))

    issue()
    gps = [gate_prep(0, m_row)]
    issue()
    d_mains = pool_window()
    gps.append(gate_prep(L, gps[0]["m_new"]))
    issue()
    pool_mix(d_mains)
    issue(2)

    for c in range(ts // L):
        r0 = c * L
        C8, mmaxX, w_interX, enegX, decay, wkX = (gps[c][n] for n in
                                                 ("C8", "mmaxX", "w_interX", "enegX", "decay", "wkX"))
        per_head = []
        rsX = None
        qnX = None
        for h in range(H):
            hc = h * DH
            q = conv_silu(hc, r0) * (DH ** -0.5)
            k = conv_silu(MLSTM_WIDTH + hc, r0)
            v_aug = jnp.concatenate([v_b[r0:r0 + L, hc:hc + DH], onehot[h]], axis=1)
            qb = q.astype(BF16)
            S = _dot_nt(qb, k.astype(BF16))
            issue()
            E = jnp.exp(jnp.where(causal, C8[h:h + 1, :] - mmaxX[:, h:h + 1], -jnp.inf))
            av = _dot((S * E).astype(BF16), v_aug)
            qc = _dot(qb, cs[h].astype(BF16))
            rsX = av[:, DH:] if rsX is None else rsX + av[:, DH:]
            qnX = qc[:, DH:] if qnX is None else qnX + qc[:, DH:]
            per_head.append(av[:, :DH] + w_interX[:, h:h + 1] * qc[:, :DH])
            kw = wkX[:, h:h + 1] * k
            cs[h] = decay[:, h:h + 1] * cs[h] + _dot_tn(kw.astype(BF16), v_aug)
            if c > 0 and h % 2 == 1:
                issue()

        denX = rsX + w_interX * qnX
        rX = 1.0 / jnp.maximum(jnp.abs(denX), enegX)

        for h in range(H):
            hc = h * DH
            hout = per_head[h] * rX[:, h:h + 1]
            hn = hout * lax.rsqrt(jnp.mean(hout * hout, axis=1, keepdims=True) + EPS)
            og = _sigmoid(og_b[r0:r0 + L, hc:hc + DH])
            mout = hn * gnorm_ref[:, hc:hc + DH] * og
            mix_b[r0:r0 + L, POOL_WIDTH + hc:POOL_WIDTH + hc + DH] = mout.astype(BF16)
    m_row = gps[-1]["m_new"]
    assert not pieces, "every projection piece must have an issue point"
    if "proj" in stages:
        history_rows()

    for h in range(H):
        c_ref[h] = cs[h]
    m_ref[0:1, :] = m_row


def _mixer(l, x2, gpre, win, wgate, bias, conv, pmat, wpool, pscale, gnorm, wout, gpost, seq):
    T = x2.shape[0]
    ts = TS_MIX
    nt = seq // ts
    ntot = T // ts
    return pl.pallas_call(
        functools.partial(_mixer_kernel, nt=nt, ntot=ntot),
        grid=(ntot + 2,),
        in_specs=[
            pl.BlockSpec((ts, D_MODEL), lambda s: (jnp.minimum(s, ntot - 1), 0)),
            pl.BlockSpec((ts, D_MODEL), lambda s: (jnp.maximum(s - 2, 0), 0)),
            _vector_spec(D_MODEL, l, G_MIX_PRE),
            _layer_spec((D_MODEL, OFF_GATE), l),
            _layer_spec((D_MODEL, V7X_LANES), l),
            _layer_spec((1, V7X_LANES), l),
            _layer_spec((CONV_WIDTH, 2 * MLSTM_WIDTH), l),
            _const_spec((POOL_GROUPS, POOL_BLOCK, POOL_BLOCK)),
            _layer_spec((POOL_GROUPS, POOL_GROUP_DIM, POOL_GROUP_DIM), l),
            _vector_spec(POOL_WIDTH, l, V_POOL_SCALE),
            _vector_spec(MLSTM_WIDTH, l, V_MLSTM_NORM),
            _layer_spec((D_MODEL, D_MODEL), l),
            _vector_spec(D_MODEL, l, G_MIX_POST),
        ],
        out_specs=pl.BlockSpec((ts, D_MODEL), lambda s: (jnp.maximum(s - 2, 0), 0)),
        out_shape=jax.ShapeDtypeStruct((T, D_MODEL), F32),
        scratch_shapes=[
            *[pltpu.VMEM((ts + POOL_HALO, POOL_WIDTH), F32)] * 2,
            *[pltpu.VMEM((ts + CONV_HALO, 2 * MLSTM_WIDTH), F32)] * 2,
            *[pltpu.VMEM((ts, MLSTM_WIDTH), BF16)] * 2,
            *[pltpu.VMEM((ts, MLSTM_WIDTH), F32)] * 2,
            *[pltpu.VMEM((ts, V7X_LANES), F32)] * 2,
            *[pltpu.VMEM((ts, D_MODEL), BF16)] * 2,
            pltpu.VMEM((MLSTM_HEADS, MLSTM_HEAD_DIM, 2 * MLSTM_HEAD_DIM), F32),
            pltpu.VMEM((V7X_SUBLANES, V7X_LANES), F32),
        ],
        compiler_params=pltpu.CompilerParams(
            dimension_semantics=("arbitrary",), vmem_limit_bytes=V7X_VMEM_LIMIT_BYTES),
        name="mixer",
    )(x2, x2, gpre, win, wgate, bias, conv, pmat, wpool, pscale, gnorm, wout, gpost)


def _memkv_kernel(mem_ref, g_ref, wk_ref, wv_ref, k_ref, v_ref):
    mb = _rmsnorm(mem_ref[0], g_ref[...]).astype(BF16)
    k_ref[0] = _dot_w(mb, wk_ref[...]).astype(BF16)
    v_ref[0] = _dot_w(mb, wv_ref[...]).astype(BF16)


def _memkv(l, mem, g, wk, wv):
    B = mem.shape[0]
    blk = pl.BlockSpec((1, N_MEM, D_MODEL), lambda b: (b, 0, 0))
    return pl.pallas_call(
        _memkv_kernel,
        grid=(B,),
        in_specs=[blk, _vector_spec(D_MODEL, l, G_MEM), _layer_spec((D_MODEL, D_MODEL), l),
                  _layer_spec((D_MODEL, D_MODEL), l)],
        out_specs=[blk, blk],
        out_shape=[jax.ShapeDtypeStruct((B, N_MEM, D_MODEL), BF16)] * 2,
        compiler_params=pltpu.CompilerParams(
            dimension_semantics=("arbitrary",), vmem_limit_bytes=V7X_VMEM_LIMIT_BYTES),
        name="memkv",
    )(mem, g, wk, wv)


def _xattn_kernel(x_ref, k_ref, v_ref, gpre_ref, wq_ref, wo_ref, gpost_ref, o_ref, att_ref):
    halves = [slice(r0, r0 + SUB_ROWS) for r0 in range(0, x_ref.shape[0], SUB_ROWS)]
    qbs = []
    for rows in halves:
        hb = _rmsnorm(x_ref[rows, :], gpre_ref[...]).astype(BF16)
        qbs.append((_dot(hb, wq_ref[...]) * (XATTN_HEAD_DIM ** -0.5)).astype(BF16))
    for h in range(XATTN_HEADS):
        cols = slice(h * XATTN_HEAD_DIM, (h + 1) * XATTN_HEAD_DIM)
        scores = [_dot_nt(qb[:, cols], k_ref[0, :, cols]) for qb in qbs]
        for rows, s in zip(halves, scores):
            p = jnp.exp(s - jnp.max(s, axis=1, keepdims=True))
            o = _dot(p.astype(BF16), v_ref[0, :, cols]) / jnp.sum(p, axis=1, keepdims=True)
            att_ref[rows, cols] = o.astype(BF16)
    for rows in halves:
        y = _dot(att_ref[rows, :], wo_ref[...])
        o_ref[rows, :] = x_ref[rows, :] + _rmsnorm(y, gpost_ref[...])


def _xattn(l, x2, k, v, gpre, wq, wo, gpost, batch, seq):
    T = x2.shape[0]
    ts = TS_ATT
    nt = seq // ts
    tok = lambda b, t: (b * nt + t, 0)
    kvb = pl.BlockSpec((1, N_MEM, D_MODEL), lambda b, t: (b, 0, 0))
    return pl.pallas_call(
        _xattn_kernel,
        grid=(batch, nt),
        in_specs=[
            pl.BlockSpec((ts, D_MODEL), tok), kvb, kvb,
            _vector_spec(D_MODEL, l, G_XATTN_PRE),
            _layer_spec((D_MODEL, D_MODEL), l),
            _layer_spec((D_MODEL, D_MODEL), l),
            _vector_spec(D_MODEL, l, G_XATTN_POST),
        ],
        out_specs=pl.BlockSpec((ts, D_MODEL), tok),
        out_shape=jax.ShapeDtypeStruct((T, D_MODEL), F32),
        scratch_shapes=[pltpu.VMEM((ts, D_MODEL), BF16)],
        compiler_params=pltpu.CompilerParams(
            dimension_semantics=("arbitrary", "arbitrary"), vmem_limit_bytes=V7X_VMEM_LIMIT_BYTES),
        name="xattn",
    )(x2, k, v, gpre, wq, wo, gpost)


def _ffn_kernel(x_ref, gpre_ref, wg_ref, wu_ref, wd_ref, gpost_ref, o_ref):
    tf = TF_FFN
    halves = [slice(r0, r0 + SUB_ROWS) for r0 in range(0, x_ref.shape[0], SUB_ROWS)]
    hbs = [_rmsnorm(x_ref[rows, :], gpre_ref[...]).astype(BF16) for rows in halves]
    accs = [None] * len(halves)

    def down(i, a, cols):
        contrib = _dot(a, wd_ref[cols, :])
        accs[i] = contrib if accs[i] is None else accs[i] + contrib

    pending = None
    for c in range(wg_ref.shape[1] // tf):
        cols = slice(c * tf, (c + 1) * tf)
        for i, hb in enumerate(hbs):
            gate = _dot(hb, wg_ref[:, cols])
            a = (gate * _sigmoid(gate) * _dot(hb, wu_ref[:, cols])).astype(BF16)
            if pending is not None:
                down(*pending)
            pending = (i, a, cols)
    down(*pending)
    for rows, acc in zip(halves, accs):
        o_ref[rows, :] = x_ref[rows, :] + _rmsnorm(acc, gpost_ref[...])


def _ffn(l, x2, gpre, wg, wu, wd, gpost):
    T = x2.shape[0]
    tm = TM_FFN
    d_ff = wg.shape[-1]
    return pl.pallas_call(
        _ffn_kernel,
        grid=(T // tm,),
        in_specs=[
            pl.BlockSpec((tm, D_MODEL), lambda i: (i, 0)),
            _vector_spec(D_MODEL, l, G_FFN_PRE),
            _layer_spec((D_MODEL, d_ff), l),
            _layer_spec((D_MODEL, d_ff), l),
            _layer_spec((d_ff, D_MODEL), l),
            _vector_spec(D_MODEL, l, G_FFN_POST),
        ],
        out_specs=pl.BlockSpec((tm, D_MODEL), lambda i: (i, 0)),
        out_shape=jax.ShapeDtypeStruct((T, D_MODEL), F32),
        compiler_params=pltpu.CompilerParams(
            dimension_semantics=("arbitrary",), vmem_limit_bytes=V7X_VMEM_LIMIT_BYTES),
        name="ffn",
    )(x2, gpre, wg, wu, wd, gpost)


def kernel(x, mem, w_in, b_gate, conv_qk, w_pool, pool_scale, mlstm_norm_g, w_out, g_mix_pre, g_mix_post, g_mem, g_xattn_pre, g_xattn_post, wq_x, wk_x, wv_x, wo_x, g_ffn_pre, g_ffn_post, w_gate, w_up, w_down):
    B, S, D = x.shape
    depth = w_in.shape[0]
    H = MLSTM_HEADS
    x2 = x.reshape(B * S, D)
    pmat = _pool_matrices()
    rows = lambda v: v.reshape(depth, 1, -1).astype(F32)
    table = lambda vs: jnp.stack([v.astype(F32) for v in vs], axis=1)[:, :, None, :]
    gains = table([g_mix_pre, g_mix_post, g_mem, g_xattn_pre, g_xattn_post, g_ffn_pre, g_ffn_post])
    mixvecs = table([pool_scale, mlstm_norm_g])
    bf = lambda w: w.astype(BF16)
    w_in_b = bf(w_in)
    wgate = jnp.concatenate(
        [w_in_b[:, :, OFF_GATE + H:], w_in_b[:, :, OFF_GATE:OFF_GATE + H],
         jnp.zeros((depth, D, V7X_LANES - 2 * H), BF16)], axis=2)
    bias = jnp.concatenate(
        [b_gate[:, H:], b_gate[:, :H], jnp.zeros((depth, V7X_LANES - 2 * H), b_gate.dtype)], axis=1)
    mixer_params = (gains, w_in_b, wgate, rows(bias), conv_qk.astype(F32), pmat, bf(w_pool),
                    mixvecs, mixvecs, bf(w_out), gains)
    memkv_params = (gains, wk_x, wv_x)
    xattn_params = (gains, bf(wq_x), bf(wo_x), gains)
    ffn_params = (gains, bf(w_gate), bf(w_up), bf(w_down), gains)

    for l in range(depth):
        x2 = _mixer(l, x2, *mixer_params, S)
        k, v = _memkv(l, mem, *memkv_params)
        x2 = _xattn(l, x2, k, v, *xattn_params, B, S)
        x2 = _ffn(l, x2, *ffn_params)

    return x2.reshape(B, S, D)
```

```python
import functools

import numpy as np
import jax
import jax.numpy as jnp
from jax import lax
from jax.experimental import pallas as pl
from jax.experimental.pallas import tpu as pltpu

F32 = jnp.float32
BF16 = jnp.bfloat16

EPS = 1e-6
D_MODEL = 1024
N_MEM = 256
POOL_WIDTH = 512
POOL_GROUPS = 4
POOL_GROUP_DIM = 128
POOL_WINDOWS = (2, 4, 8, 16)
MLSTM_WIDTH = 512
MLSTM_HEADS = 4
MLSTM_HEAD_DIM = 128
CONV_WIDTH = 4
XATTN_HEADS = 4
XATTN_HEAD_DIM = 256
OFF_Q = POOL_WIDTH
OFF_K = OFF_Q + MLSTM_WIDTH
OFF_V = OFF_K + MLSTM_WIDTH
OFF_O = OFF_V + MLSTM_WIDTH
OFF_GATE = OFF_O + MLSTM_WIDTH

V7X_LANES = 128
V7X_SUBLANES = 8
V7X_VMEM_LIMIT_BYTES = 56 * 1024 * 1024

G_MIX_PRE, G_MIX_POST, G_MEM, G_XATTN_PRE, G_XATTN_POST, G_FFN_PRE, G_FFN_POST = range(7)
V_POOL_SCALE, V_MLSTM_NORM = range(2)

IN_COLS_PAD = OFF_GATE + V7X_LANES
POOL_HALO = 16
CONV_HALO = 8
POOL_BLOCK = 256
PIECE_COLS = 256

TS_MIX = 512
L_CHUNK = 256
TS_ATT = 2048
TM_FFN = 1024
SUB_ROWS = 512
SUB_ROWS_FFN = 256
TF_FFN = 256


def _rmsnorm(x, g):
    return x * lax.rsqrt(jnp.mean(x * x, axis=-1, keepdims=True) + EPS) * g


def _sigmoid(x):
    return 1.0 / (1.0 + jnp.exp(-x))


def _log_sigmoid(x):
    return jnp.minimum(x, 0.0) - jnp.log(1.0 + jnp.exp(-jnp.abs(x)))


def _dot(a, b):
    return jnp.dot(a, b, preferred_element_type=F32)


def _dot_w(a, w):
    return lax.dot_general(a, w, (((1,), (0,)), ((), ())), preferred_element_type=F32)


def _dot_nt(a, b):
    return lax.dot_general(a, b, (((1,), (1,)), ((), ())), preferred_element_type=F32)


def _dot_tn(a, b):
    return lax.dot_general(a, b, (((0,), (0,)), ((), ())), preferred_element_type=F32)


def _const_spec(shape):
    nd = len(shape)
    return pl.BlockSpec(shape, lambda *_: (0,) * nd, pipeline_mode=pl.Buffered(1))


def _vector_spec(n, l, k):
    return pl.BlockSpec((None, None, 1, n), lambda *_: (l, k, 0, 0), pipeline_mode=pl.Buffered(1))


def _layer_spec(shape, l):
    nd = len(shape)
    return pl.BlockSpec((None,) + tuple(shape), lambda *_: (l,) + (0,) * nd, pipeline_mode=pl.Buffered(1))


def _pool_matrices():
    i = np.arange(POOL_BLOCK)[:, None]
    j = np.arange(POOL_BLOCK)[None, :]
    mats = []
    for win in POOL_WINDOWS:
        inside = (j <= i) & (j > i - win)
        mats.append(inside.astype(np.float32) / win - (i == j).astype(np.float32))
    return jnp.asarray(np.stack(mats), dtype=BF16)


def _mixer_kernel(*refs, nt, ntot):
    s = pl.program_id(0)

    def variant(step, stages):
        return functools.partial(_mixer_step, step % 2, 1 - step % 2, *refs, nt=nt, stages=stages)

    steady = jnp.logical_and(s >= 1, s <= ntot)
    pl.when(s == 0)(variant(0, ("proj",)))
    pl.when(jnp.logical_and(steady, lax.rem(s, 2) == 0))(variant(0, ("proj", "mix", "out")))
    pl.when(jnp.logical_and(steady, lax.rem(s, 2) == 1))(variant(1, ("proj", "mix", "out")))
    pl.when(s == ntot + 1)(variant(ntot + 1, ("out",)))


def _mixer_step(a, b, xn_ref, xb_ref, gpre_ref, win_ref, wgate_ref, bias_ref, conv_ref, pmat_ref, wpool_ref,
                pscale_ref, gnorm_ref, wout_ref, gpost_ref, o_ref,
                u0_ref, u1_ref, qk0_ref, qk1_ref, v0_ref, v1_ref, og0_ref, og1_ref,
                g0_ref, g1_ref, mix0_ref, mix1_ref, c_ref, m_ref, *, nt, stages):
    uext_a, uext_b = (u0_ref, u1_ref)[a], (u0_ref, u1_ref)[b]
    qkext_a, qkext_b = (qk0_ref, qk1_ref)[a], (qk0_ref, qk1_ref)[b]
    v_a, v_b = (v0_ref, v1_ref)[a], (v0_ref, v1_ref)[b]
    og_a, og_b = (og0_ref, og1_ref)[a], (og0_ref, og1_ref)[b]
    g_a, g_b = (g0_ref, g1_ref)[a], (g0_ref, g1_ref)[b]
    mix_a, mix_b = (mix0_ref, mix1_ref)[a], (mix0_ref, mix1_ref)[b]
    ts = xn_ref.shape[0]
    L = L_CHUNK
    H = MLSTM_HEADS
    DH = MLSTM_HEAD_DIM
    PC = PIECE_COLS
    s = pl.program_id(0)
    first_a = lax.rem(s, nt) == 0
    tb = lax.rem(s + nt - 1, nt)
    first_b = tb == 0

    hb = _rmsnorm(xn_ref[...], gpre_ref[...]).astype(BF16) if "proj" in stages else None

    out_parts = []

    def out_piece(j):
        def run():
            out_parts.append(_dot(mix_a[...], wout_ref[:, j * PC:(j + 1) * PC]))
            if len(out_parts) == D_MODEL // PC:
                ss = sum(jnp.sum(y * y, axis=1, keepdims=True) for y in out_parts)
                rs = lax.rsqrt(ss * (1.0 / D_MODEL) + EPS)
                for jj, y in enumerate(out_parts):
                    cols = slice(jj * PC, (jj + 1) * PC)
                    o_ref[:, cols] = xb_ref[:, cols] + y * rs * gpost_ref[:, cols]
        return run

    def proj_piece(c0):
        def run():
            w = win_ref[:, c0:c0 + PC] if c0 < OFF_GATE else wgate_ref[...]
            y = _dot(hb, w)
            if c0 < OFF_Q:
                uext_a[POOL_HALO:POOL_HALO + ts, c0:c0 + PC] = y
            elif c0 < OFF_V:
                qkext_a[CONV_HALO:CONV_HALO + ts, c0 - OFF_Q:c0 - OFF_Q + PC] = y
            elif c0 < OFF_O:
                v_a[:, c0 - OFF_V:c0 - OFF_V + PC] = y.astype(BF16)
            elif c0 < OFF_GATE:
                og_a[:, c0 - OFF_O:c0 - OFF_O + PC] = y
            else:
                g_a[...] = y + bias_ref[...]
        return run

    pieces = []
    if "out" in stages:
        pieces += [out_piece(j) for j in range(D_MODEL // PC)]
    if "proj" in stages:
        pieces += [proj_piece(c0) for c0 in range(0, IN_COLS_PAD, PC)]
    pieces.reverse()

    def issue(n=1):
        for _ in range(n):
            if pieces:
                pieces.pop()()

    def history_rows():
        if "mix" in stages:
            uext_a[0:POOL_HALO, :] = jnp.where(first_a, 0.0, uext_b[ts:ts + POOL_HALO, :])
            qkext_a[0:CONV_HALO, :] = jnp.where(first_a, 0.0, qkext_b[ts:ts + CONV_HALO, :])
        else:
            uext_a[0:POOL_HALO, :] = jnp.zeros((POOL_HALO, POOL_WIDTH), F32)
            qkext_a[0:CONV_HALO, :] = jnp.zeros((CONV_HALO, 2 * MLSTM_WIDTH), F32)

    if "mix" not in stages:
        issue(len(pieces))
        if "proj" in stages:
            history_rows()
            for ref in (mix_b, c_ref, m_ref):
                ref[...] = jnp.zeros_like(ref)
        return

    def pool_window():
        d_mains = []
        for blk in range(ts // POOL_BLOCK):
            r0 = blk * POOL_BLOCK
            for g in range(POOL_GROUPS):
                cols = slice(g * POOL_GROUP_DIM, (g + 1) * POOL_GROUP_DIM)
                ublk = uext_b[POOL_HALO + r0:POOL_HALO + r0 + POOL_BLOCK, cols]
                d_mains.append(_dot(pmat_ref[g], ublk.astype(BF16)))
        return d_mains

    def pool_mix(d_mains):
        for blk in range(ts // POOL_BLOCK):
            r0 = blk * POOL_BLOCK
            pos = tb * ts + r0 + lax.broadcasted_iota(jnp.int32, (POOL_HALO, 1), 0)
            for g, win in enumerate(POOL_WINDOWS):
                cols = slice(g * POOL_GROUP_DIM, (g + 1) * POOL_GROUP_DIM)
                cur = uext_b[POOL_HALO + r0:POOL_HALO + r0 + POOL_HALO, cols]
                acc = cur
                for k in range(1, win):
                    acc = acc + uext_b[POOL_HALO + r0 - k:POOL_HALO + r0 - k + POOL_HALO, cols]
                count = jnp.minimum(pos + 1, win).astype(F32)
                d_head = acc / count - cur
                d = jnp.concatenate([d_head, d_mains[blk * POOL_GROUPS + g][POOL_HALO:, :]], axis=0)
                y = _dot(d.astype(BF16), wpool_ref[g]) * pscale_ref[:, cols]
                mix_b[r0:r0 + POOL_BLOCK, cols] = y.astype(BF16)

    ri = lax.broadcasted_iota(jnp.int32, (L, L), 0)
    ci = lax.broadcasted_iota(jnp.int32, (L, L), 1)
    causal = ci <= ri
    lane_x = lax.broadcasted_iota(jnp.int32, (L, V7X_LANES), 1)
    sub8 = lax.broadcasted_iota(jnp.int32, (V7X_SUBLANES, L), 0)
    lane8 = lax.broadcasted_iota(jnp.int32, (V7X_SUBLANES, L), 1)
    zpad = jnp.zeros((V7X_LANES - V7X_SUBLANES, L), F32)
    onehot = [(lane_x == h).astype(BF16) for h in range(H)]
    cw = conv_ref[...]

    cs = [jnp.where(first_b, 0.0, c_ref[h]) for h in range(H)]
    m_row = jnp.where(first_b, 0.0, m_ref[0:1, :])

    def conv_silu(col0, r0):
        acc = None
        for j in range(CONV_WIDTH):
            off = CONV_HALO + r0 - (CONV_WIDTH - 1) + j
            term = cw[j:j + 1, col0:col0 + DH] * qkext_b[off:off + L, col0:col0 + DH]
            acc = term if acc is None else acc + term
        return acc * _sigmoid(acc)

    def lane_scan(x, op, fill):
        sh = 1
        while sh < L:
            x = op(x, jnp.where(lane8 >= sh, pltpu.roll(x, sh, axis=1), fill))
            sh *= 2
        return x

    def gate_prep(r0, m_row):
        G = g_b[r0:r0 + L, :]
        R8 = G.T[0:V7X_SUBLANES, :]
        low = sub8 < H
        B8 = lane_scan(jnp.where(low, _log_sigmoid(R8), 0.0), jnp.add, 0.0)
        C8 = jnp.where(low, pltpu.roll(R8, H, axis=0) - B8, 0.0)
        M8 = lane_scan(C8, jnp.maximum, -jnp.inf)
        T8 = jnp.where(low, B8, pltpu.roll(M8, H, axis=0))
        TX = jnp.concatenate([T8, zpad], axis=0).T
        valid = lane_x < H
        bX = jnp.where(valid, TX, 0.0)
        cmX = jnp.where(valid, pltpu.roll(TX, V7X_LANES - H, axis=1), 0.0)
        igX = jnp.where(valid, pltpu.roll(G, V7X_LANES - H, axis=1), 0.0)
        mmaxX = jnp.maximum(cmX, m_row)
        bL = bX[L - 1:L, :]
        gkX = bL - bX + igX
        m_new = jnp.maximum(bL + m_row, jnp.max(gkX, axis=0, keepdims=True))
        return dict(C8=C8, mmaxX=mmaxX, m_new=m_new,
                    w_interX=jnp.exp(m_row - mmaxX),
                    enegX=jnp.exp(-(bX + mmaxX)),
                    decay=jnp.exp(bL + m_row - m_new),
                    wkX=jnp.exp(gkX - m_new))

    issue()
    gps = [gate_prep(0, m_row)]
    issue()
    d_mains = pool_window()
    gps.append(gate_prep(L, gps[0]["m_new"]))
    issue()
    pool_mix(d_mains)
    issue(3)

    for c in range(ts // L):
        r0 = c * L
        C8, mmaxX, w_interX, enegX, decay, wkX = (gps[c][n] for n in
                                                 ("C8", "mmaxX", "w_interX", "enegX", "decay", "wkX"))
        per_head = []
        rsX = None
        qnX = None
        for h in range(H):
            hc = h * DH
            q = conv_silu(hc, r0) * (DH ** -0.5)
            k = conv_silu(MLSTM_WIDTH + hc, r0)
            v_aug = jnp.concatenate([v_b[r0:r0 + L, hc:hc + DH], onehot[h]], axis=1)
            qb = q.astype(BF16)
            S = _dot_nt(qb, k.astype(BF16))
            qc = _dot(qb, cs[h].astype(BF16))
            issue()
            E = jnp.exp(jnp.where(causal, C8[h:h + 1, :] - mmaxX[:, h:h + 1], -jnp.inf))
            av = _dot((S * E).astype(BF16), v_aug)
            rsX = av[:, DH:] if rsX is None else rsX + av[:, DH:]
            qnX = qc[:, DH:] if qnX is None else qnX + qc[:, DH:]
            per_head.append(av[:, :DH] + w_interX[:, h:h + 1] * qc[:, :DH])
            kw = wkX[:, h:h + 1] * k
            cs[h] = decay[:, h:h + 1] * cs[h] + _dot_tn(kw.astype(BF16), v_aug)
            if c > 0 and h == 3:
                issue()

        denX = rsX + w_interX * qnX
        rX = 1.0 / jnp.maximum(jnp.abs(denX), enegX)

        for h in range(H):
            hc = h * DH
            hout = per_head[h] * rX[:, h:h + 1]
            hn = hout * lax.rsqrt(jnp.mean(hout * hout, axis=1, keepdims=True) + EPS)
            og = _sigmoid(og_b[r0:r0 + L, hc:hc + DH])
            mout = hn * gnorm_ref[:, hc:hc + DH] * og
            mix_b[r0:r0 + L, POOL_WIDTH + hc:POOL_WIDTH + hc + DH] = mout.astype(BF16)
    m_row = gps[-1]["m_new"]
    assert not pieces, "every projection piece must have an issue point"
    if "proj" in stages:
        history_rows()

    for h in range(H):
        c_ref[h] = cs[h]
    m_ref[0:1, :] = m_row


def _mixer(l, x2, gpre, win, wgate, bias, conv, pmat, wpool, pscale, gnorm, wout, gpost, seq):
    T = x2.shape[0]
    ts = TS_MIX
    nt = seq // ts
    ntot = T // ts
    return pl.pallas_call(
        functools.partial(_mixer_kernel, nt=nt, ntot=ntot),
        grid=(ntot + 2,),
        in_specs=[
            pl.BlockSpec((ts, D_MODEL), lambda s: (jnp.minimum(s, ntot - 1), 0)),
            pl.BlockSpec((ts, D_MODEL), lambda s: (jnp.maximum(s - 2, 0), 0)),
            _vector_spec(D_MODEL, l, G_MIX_PRE),
            _layer_spec((D_MODEL, OFF_GATE), l),
            _layer_spec((D_MODEL, V7X_LANES), l),
            _layer_spec((1, V7X_LANES), l),
            _layer_spec((CONV_WIDTH, 2 * MLSTM_WIDTH), l),
            _const_spec((POOL_GROUPS, POOL_BLOCK, POOL_BLOCK)),
            _layer_spec((POOL_GROUPS, POOL_GROUP_DIM, POOL_GROUP_DIM), l),
            _vector_spec(POOL_WIDTH, l, V_POOL_SCALE),
            _vector_spec(MLSTM_WIDTH, l, V_MLSTM_NORM),
            _layer_spec((D_MODEL, D_MODEL), l),
            _vector_spec(D_MODEL, l, G_MIX_POST),
        ],
        out_specs=pl.BlockSpec((ts, D_MODEL), lambda s: (jnp.maximum(s - 2, 0), 0)),
        out_shape=jax.ShapeDtypeStruct((T, D_MODEL), F32),
        scratch_shapes=[
            *[pltpu.VMEM((ts + POOL_HALO, POOL_WIDTH), F32)] * 2,
            *[pltpu.VMEM((ts + CONV_HALO, 2 * MLSTM_WIDTH), F32)] * 2,
            *[pltpu.VMEM((ts, MLSTM_WIDTH), BF16)] * 2,
            *[pltpu.VMEM((ts, MLSTM_WIDTH), F32)] * 2,
            *[pltpu.VMEM((ts, V7X_LANES), F32)] * 2,
            *[pltpu.VMEM((ts, D_MODEL), BF16)] * 2,
            pltpu.VMEM((MLSTM_HEADS, MLSTM_HEAD_DIM, 2 * MLSTM_HEAD_DIM), F32),
            pltpu.VMEM((V7X_SUBLANES, V7X_LANES), F32),
        ],
        compiler_params=pltpu.CompilerParams(
            dimension_semantics=("arbitrary",), vmem_limit_bytes=V7X_VMEM_LIMIT_BYTES),
        name="mixer",
    )(x2, x2, gpre, win, wgate, bias, conv, pmat, wpool, pscale, gnorm, wout, gpost)


def _memkv_kernel(mem_ref, g_ref, wk_ref, wv_ref, k_ref, v_ref):
    mb = _rmsnorm(mem_ref[0], g_ref[...]).astype(BF16)
    k_ref[0] = _dot_w(mb, wk_ref[...]).astype(BF16)
    v_ref[0] = _dot_w(mb, wv_ref[...]).astype(BF16)


def _memkv(mem, g, wk, wv):
    depth = wk.shape[0]
    B = mem.shape[0]
    weight = pl.BlockSpec((None, D_MODEL, D_MODEL), lambda l, b: (l, 0, 0))
    out = pl.BlockSpec((None, 1, N_MEM, D_MODEL), lambda l, b: (l, b, 0, 0))
    return pl.pallas_call(
        _memkv_kernel,
        grid=(depth, B),
        in_specs=[pl.BlockSpec((1, N_MEM, D_MODEL), lambda l, b: (b, 0, 0)),
                  pl.BlockSpec((None, None, 1, D_MODEL), lambda l, b: (l, G_MEM, 0, 0)),
                  weight, weight],
        out_specs=[out, out],
        out_shape=[jax.ShapeDtypeStruct((depth, B, N_MEM, D_MODEL), BF16)] * 2,
        compiler_params=pltpu.CompilerParams(
            dimension_semantics=("arbitrary", "arbitrary"), vmem_limit_bytes=V7X_VMEM_LIMIT_BYTES),
        name="memkv",
    )(mem, g, wk, wv)


def _xattn_kernel(x_ref, k_ref, v_ref, gpre_ref, wq_ref, wo_ref, gpost_ref, o_ref, att_ref):
    halves = [slice(r0, r0 + SUB_ROWS) for r0 in range(0, x_ref.shape[0], SUB_ROWS)]
    qbs = []
    for rows in halves:
        hb = _rmsnorm(x_ref[rows, :], gpre_ref[...]).astype(BF16)
        qbs.append((_dot(hb, wq_ref[...]) * (XATTN_HEAD_DIM ** -0.5)).astype(BF16))
    for h in range(XATTN_HEADS):
        cols = slice(h * XATTN_HEAD_DIM, (h + 1) * XATTN_HEAD_DIM)
        scores = [_dot_nt(qb[:, cols], k_ref[0, :, cols]) for qb in qbs]
        for rows, s in zip(halves, scores):
            p = jnp.exp(s - jnp.max(s, axis=1, keepdims=True))
            o = _dot(p.astype(BF16), v_ref[0, :, cols]) / jnp.sum(p, axis=1, keepdims=True)
            att_ref[rows, cols] = o.astype(BF16)
    for rows in halves:
        y = _dot(att_ref[rows, :], wo_ref[...])
        o_ref[rows, :] = x_ref[rows, :] + _rmsnorm(y, gpost_ref[...])


def _xattn(l, x2, k, v, gpre, wq, wo, gpost, batch, seq):
    T = x2.shape[0]
    ts = TS_ATT
    nt = seq // ts
    tok = lambda b, t: (b * nt + t, 0)
    kvb = pl.BlockSpec((None, 1, N_MEM, D_MODEL), lambda b, t: (l, b, 0, 0))
    return pl.pallas_call(
        _xattn_kernel,
        grid=(batch, nt),
        in_specs=[
            pl.BlockSpec((ts, D_MODEL), tok), kvb, kvb,
            _vector_spec(D_MODEL, l, G_XATTN_PRE),
            _layer_spec((D_MODEL, D_MODEL), l),
            _layer_spec((D_MODEL, D_MODEL), l),
            _vector_spec(D_MODEL, l, G_XATTN_POST),
        ],
        out_specs=pl.BlockSpec((ts, D_MODEL), tok),
        out_shape=jax.ShapeDtypeStruct((T, D_MODEL), F32),
        scratch_shapes=[pltpu.VMEM((ts, D_MODEL), BF16)],
        compiler_params=pltpu.CompilerParams(
            dimension_semantics=("arbitrary", "arbitrary"), vmem_limit_bytes=V7X_VMEM_LIMIT_BYTES),
        name="xattn",
    )(x2, k, v, gpre, wq, wo, gpost)


def _ffn_kernel(x_ref, gpre_ref, wg_ref, wu_ref, wd_ref, gpost_ref, o_ref):
    tf = TF_FFN
    halves = [slice(r0, r0 + SUB_ROWS_FFN) for r0 in range(0, x_ref.shape[0], SUB_ROWS_FFN)]
    hbs = [_rmsnorm(x_ref[rows, :], gpre_ref[...]).astype(BF16) for rows in halves]
    accs = [None] * len(halves)

    def down(i, a, cols):
        contrib = _dot(a, wd_ref[cols, :])
        accs[i] = contrib if accs[i] is None else accs[i] + contrib

    pending = None
    for c in range(wg_ref.shape[1] // tf):
        cols = slice(c * tf, (c + 1) * tf)
        for i, hb in enumerate(hbs):
            gate = _dot(hb, wg_ref[:, cols])
            a = (gate * _sigmoid(gate) * _dot(hb, wu_ref[:, cols])).astype(BF16)
            if pending is not None:
                down(*pending)
            pending = (i, a, cols)
    down(*pending)
    for rows, acc in zip(halves, accs):
        o_ref[rows, :] = x_ref[rows, :] + _rmsnorm(acc, gpost_ref[...])


def _ffn(l, x2, gpre, wg, wu, wd, gpost):
    T = x2.shape[0]
    tm = TM_FFN
    d_ff = wg.shape[-1]
    return pl.pallas_call(
        _ffn_kernel,
        grid=(T // tm,),
        in_specs=[
            pl.BlockSpec((tm, D_MODEL), lambda i: (i, 0)),
            _vector_spec(D_MODEL, l, G_FFN_PRE),
            _layer_spec((D_MODEL, d_ff), l),
            _layer_spec((D_MODEL, d_ff), l),
            _layer_spec((d_ff, D_MODEL), l),
            _vector_spec(D_MODEL, l, G_FFN_POST),
        ],
        out_specs=pl.BlockSpec((tm, D_MODEL), lambda i: (i, 0)),
        out_shape=jax.ShapeDtypeStruct((T, D_MODEL), F32),
        compiler_params=pltpu.CompilerParams(
            dimension_semantics=("arbitrary",), vmem_limit_bytes=V7X_VMEM_LIMIT_BYTES),
        name="ffn",
    )(x2, gpre, wg, wu, wd, gpost)


def kernel(x, mem, w_in, b_gate, conv_qk, w_pool, pool_scale, mlstm_norm_g, w_out, g_mix_pre, g_mix_post, g_mem, g_xattn_pre, g_xattn_post, wq_x, wk_x, wv_x, wo_x, g_ffn_pre, g_ffn_post, w_gate, w_up, w_down):
    B, S, D = x.shape
    depth = w_in.shape[0]
    H = MLSTM_HEADS
    x2 = x.reshape(B * S, D)
    pmat = _pool_matrices()
    rows = lambda v: v.reshape(depth, 1, -1).astype(F32)
    table = lambda vs: jnp.stack([v.astype(F32) for v in vs], axis=1)[:, :, None, :]
    gains = table([g_mix_pre, g_mix_post, g_mem, g_xattn_pre, g_xattn_post, g_ffn_pre, g_ffn_post])
    mixvecs = table([pool_scale, mlstm_norm_g])
    bf = lambda w: w.astype(BF16)
    w_in_b = bf(w_in)
    wgate = jnp.concatenate(
        [w_in_b[:, :, OFF_GATE + H:], w_in_b[:, :, OFF_GATE:OFF_GATE + H],
         jnp.zeros((depth, D, V7X_LANES - 2 * H), BF16)], axis=2)
    bias = jnp.concatenate(
        [b_gate[:, H:], b_gate[:, :H], jnp.zeros((depth, V7X_LANES - 2 * H), b_gate.dtype)], axis=1)
    mixer_params = (gains, w_in_b, wgate, rows(bias), conv_qk.astype(F32), pmat, bf(w_pool),
                    mixvecs, mixvecs, bf(w_out), gains)
    xattn_params = (gains, bf(wq_x), bf(wo_x), gains)
    ffn_params = (gains, bf(w_gate), bf(w_up), bf(w_down), gains)
    k, v = _memkv(mem, gains, wk_x, wv_x)

    for l in range(depth):
        x2 = _mixer(l, x2, *mixer_params, S)
        x2 = _xattn(l, x2, k, v, *xattn_params, B, S)
        x2 = _ffn(l, x2, *ffn_params)

    return x2.reshape(B, S, D)
```

```python
import functools

import numpy as np
import jax
import jax.numpy as jnp
from jax import lax
from jax.experimental import pallas as pl
from jax.experimental.pallas import tpu as pltpu

F32 = jnp.float32
BF16 = jnp.bfloat16

EPS = 1e-6
D_MODEL = 1024
N_MEM = 256
POOL_WIDTH = 512
POOL_GROUPS = 4
POOL_GROUP_DIM = 128
POOL_WINDOWS = (2, 4, 8, 16)
MLSTM_WIDTH = 512
MLSTM_HEADS = 4
MLSTM_HEAD_DIM = 128
CONV_WIDTH = 4
XATTN_HEADS = 4
XATTN_HEAD_DIM = 256
OFF_Q = POOL_WIDTH
OFF_K = OFF_Q + MLSTM_WIDTH
OFF_V = OFF_K + MLSTM_WIDTH
OFF_O = OFF_V + MLSTM_WIDTH
OFF_GATE = OFF_O + MLSTM_WIDTH

V7X_LANES = 128
V7X_SUBLANES = 8
V7X_VMEM_LIMIT_BYTES = 56 * 1024 * 1024

G_MIX_PRE, G_MIX_POST, G_MEM, G_XATTN_PRE, G_XATTN_POST, G_FFN_PRE, G_FFN_POST = range(7)
V_POOL_SCALE, V_MLSTM_NORM = range(2)

IN_COLS_PAD = OFF_GATE + V7X_LANES
POOL_HALO = 16
CONV_HALO = 8
POOL_BLOCK = 256
PIECE_COLS = 256

TS_MIX = 512
L_CHUNK = 256
TS_ATT = 2048
TM_FFN = 1024
SUB_ROWS = 512
SUB_ROWS_FFN = 256
TF_FFN = 256


def _rmsnorm(x, g):
    return x * lax.rsqrt(jnp.mean(x * x, axis=-1, keepdims=True) + EPS) * g


def _sigmoid(x):
    return 1.0 / (1.0 + jnp.exp(-x))


def _log_sigmoid(x):
    return jnp.minimum(x, 0.0) - jnp.log(1.0 + jnp.exp(-jnp.abs(x)))


def _dot(a, b):
    return jnp.dot(a, b, preferred_element_type=F32)


def _dot_w(a, w):
    return lax.dot_general(a, w, (((1,), (0,)), ((), ())), preferred_element_type=F32)


def _dot_nt(a, b):
    return lax.dot_general(a, b, (((1,), (1,)), ((), ())), preferred_element_type=F32)


def _dot_tn(a, b):
    return lax.dot_general(a, b, (((0,), (0,)), ((), ())), preferred_element_type=F32)


def _const_spec(shape):
    nd = len(shape)
    return pl.BlockSpec(shape, lambda *_: (0,) * nd, pipeline_mode=pl.Buffered(1))


def _vector_spec(n, l, k):
    return pl.BlockSpec((None, None, 1, n), lambda *_: (l, k, 0, 0), pipeline_mode=pl.Buffered(1))


def _layer_spec(shape, l):
    nd = len(shape)
    return pl.BlockSpec((None,) + tuple(shape), lambda *_: (l,) + (0,) * nd, pipeline_mode=pl.Buffered(1))


def _pool_matrices():
    i = np.arange(POOL_BLOCK)[:, None]
    j = np.arange(POOL_BLOCK)[None, :]
    mats = []
    for win in POOL_WINDOWS:
        inside = (j <= i) & (j > i - win)
        mats.append(inside.astype(np.float32) / win - (i == j).astype(np.float32))
    return jnp.asarray(np.stack(mats), dtype=BF16)


def _mixer_kernel(*refs, nt, ntot):
    s = pl.program_id(0)

    def variant(step, stages):
        return functools.partial(_mixer_step, step % 2, 1 - step % 2, *refs, nt=nt, stages=stages)

    steady = jnp.logical_and(s >= 1, s <= ntot)
    pl.when(s == 0)(variant(0, ("proj",)))
    pl.when(jnp.logical_and(steady, lax.rem(s, 2) == 0))(variant(0, ("proj", "mix", "out")))
    pl.when(jnp.logical_and(steady, lax.rem(s, 2) == 1))(variant(1, ("proj", "mix", "out")))
    pl.when(s == ntot + 1)(variant(ntot + 1, ("out",)))


def _mixer_step(a, b, xn_ref, xb_ref, gpre_ref, win_ref, wgate_ref, bias_ref, conv_ref, pmat_ref, wpool_ref,
                pscale_ref, gnorm_ref, wout_ref, gpost_ref, o_ref,
                u0_ref, u1_ref, qk0_ref, qk1_ref, v0_ref, v1_ref, og0_ref, og1_ref,
                g0_ref, g1_ref, mix0_ref, mix1_ref, c_ref, m_ref, *, nt, stages):
    uext_a, uext_b = (u0_ref, u1_ref)[a], (u0_ref, u1_ref)[b]
    qkext_a, qkext_b = (qk0_ref, qk1_ref)[a], (qk0_ref, qk1_ref)[b]
    v_a, v_b = (v0_ref, v1_ref)[a], (v0_ref, v1_ref)[b]
    og_a, og_b = (og0_ref, og1_ref)[a], (og0_ref, og1_ref)[b]
    g_a, g_b = (g0_ref, g1_ref)[a], (g0_ref, g1_ref)[b]
    mix_a, mix_b = (mix0_ref, mix1_ref)[a], (mix0_ref, mix1_ref)[b]
    ts = xn_ref.shape[0]
    L = L_CHUNK
    H = MLSTM_HEADS
    DH = MLSTM_HEAD_DIM
    PC = PIECE_COLS
    s = pl.program_id(0)
    first_a = lax.rem(s, nt) == 0
    tb = lax.rem(s + nt - 1, nt)
    first_b = tb == 0

    hb = _rmsnorm(xn_ref[...], gpre_ref[...]).astype(BF16) if "proj" in stages else None

    out_parts = []

    def out_piece(j):
        def run():
            out_parts.append(_dot(mix_a[...], wout_ref[:, j * PC:(j + 1) * PC]))
            if len(out_parts) == D_MODEL // PC:
                ss = sum(jnp.sum(y * y, axis=1, keepdims=True) for y in out_parts)
                rs = lax.rsqrt(ss * (1.0 / D_MODEL) + EPS)
                for jj, y in enumerate(out_parts):
                    cols = slice(jj * PC, (jj + 1) * PC)
                    o_ref[:, cols] = xb_ref[:, cols] + y * rs * gpost_ref[:, cols]
        return run

    def proj_piece(c0):
        def run():
            w = win_ref[:, c0:c0 + PC] if c0 < OFF_GATE else wgate_ref[...]
            y = _dot(hb, w)
            if c0 < OFF_Q:
                uext_a[POOL_HALO:POOL_HALO + ts, c0:c0 + PC] = y
            elif c0 < OFF_V:
                qkext_a[CONV_HALO:CONV_HALO + ts, c0 - OFF_Q:c0 - OFF_Q + PC] = y
            elif c0 < OFF_O:
                v_a[:, c0 - OFF_V:c0 - OFF_V + PC] = y.astype(BF16)
            elif c0 < OFF_GATE:
                og_a[:, c0 - OFF_O:c0 - OFF_O + PC] = y
            else:
                g_a[...] = y + bias_ref[...]
        return run

    pieces = []
    if "out" in stages:
        pieces += [out_piece(j) for j in range(D_MODEL // PC)]
    if "proj" in stages:
        pieces += [proj_piece(c0) for c0 in range(0, IN_COLS_PAD, PC)]
    pieces.reverse()

    def issue(n=1):
        for _ in range(n):
            if pieces:
                pieces.pop()()

    def history_rows():
        if "mix" in stages:
            uext_a[0:POOL_HALO, :] = jnp.where(first_a, 0.0, uext_b[ts:ts + POOL_HALO, :])
            qkext_a[0:CONV_HALO, :] = jnp.where(first_a, 0.0, qkext_b[ts:ts + CONV_HALO, :])
        else:
            uext_a[0:POOL_HALO, :] = jnp.zeros((POOL_HALO, POOL_WIDTH), F32)
            qkext_a[0:CONV_HALO, :] = jnp.zeros((CONV_HALO, 2 * MLSTM_WIDTH), F32)

    if "mix" not in stages:
        issue(len(pieces))
        if "proj" in stages:
            history_rows()
            for ref in (mix_b, c_ref, m_ref):
                ref[...] = jnp.zeros_like(ref)
        return

    def pool_window():
        d_mains = []
        for blk in range(ts // POOL_BLOCK):
            r0 = blk * POOL_BLOCK
            for g in range(POOL_GROUPS):
                cols = slice(g * POOL_GROUP_DIM, (g + 1) * POOL_GROUP_DIM)
                ublk = uext_b[POOL_HALO + r0:POOL_HALO + r0 + POOL_BLOCK, cols]
                d_mains.append(_dot(pmat_ref[g], ublk.astype(BF16)))
        return d_mains

    def pool_mix(d_mains):
        for blk in range(ts // POOL_BLOCK):
            r0 = blk * POOL_BLOCK
            pos = tb * ts + r0 + lax.broadcasted_iota(jnp.int32, (POOL_HALO, 1), 0)
            for g, win in enumerate(POOL_WINDOWS):
                cols = slice(g * POOL_GROUP_DIM, (g + 1) * POOL_GROUP_DIM)
                cur = uext_b[POOL_HALO + r0:POOL_HALO + r0 + POOL_HALO, cols]
                acc = cur
                for k in range(1, win):
                    acc = acc + uext_b[POOL_HALO + r0 - k:POOL_HALO + r0 - k + POOL_HALO, cols]
                count = jnp.minimum(pos + 1, win).astype(F32)
                d_head = acc / count - cur
                d = jnp.concatenate([d_head, d_mains[blk * POOL_GROUPS + g][POOL_HALO:, :]], axis=0)
                y = _dot(d.astype(BF16), wpool_ref[g]) * pscale_ref[:, cols]
                mix_b[r0:r0 + POOL_BLOCK, cols] = y.astype(BF16)

    ri = lax.broadcasted_iota(jnp.int32, (L, L), 0)
    ci = lax.broadcasted_iota(jnp.int32, (L, L), 1)
    causal = ci <= ri
    lane_x = lax.broadcasted_iota(jnp.int32, (L, V7X_LANES), 1)
    sub8 = lax.broadcasted_iota(jnp.int32, (V7X_SUBLANES, L), 0)
    lane8 = lax.broadcasted_iota(jnp.int32, (V7X_SUBLANES, L), 1)
    zpad = jnp.zeros((V7X_LANES - V7X_SUBLANES, L), F32)
    onehot = [(lane_x == h).astype(BF16) for h in range(H)]
    cw = conv_ref[...]

    cs = [jnp.where(first_b, 0.0, c_ref[h]) for h in range(H)]
    m_row = jnp.where(first_b, 0.0, m_ref[0:1, :])

    def conv_silu(col0, r0):
        acc = None
        for j in range(CONV_WIDTH):
            off = CONV_HALO + r0 - (CONV_WIDTH - 1) + j
            term = cw[j:j + 1, col0:col0 + DH] * qkext_b[off:off + L, col0:col0 + DH]
            acc = term if acc is None else acc + term
        return acc * _sigmoid(acc)

    def lane_scan(x, op, fill):
        sh = 1
        while sh < L:
            x = op(x, jnp.where(lane8 >= sh, pltpu.roll(x, sh, axis=1), fill))
            sh *= 2
        return x

    def gate_prep(r0, m_row):
        G = g_b[r0:r0 + L, :]
        R8 = G.T[0:V7X_SUBLANES, :]
        low = sub8 < H
        B8 = lane_scan(jnp.where(low, _log_sigmoid(R8), 0.0), jnp.add, 0.0)
        C8 = jnp.where(low, pltpu.roll(R8, H, axis=0) - B8, 0.0)
        M8 = lane_scan(C8, jnp.maximum, -jnp.inf)
        T8 = jnp.where(low, B8, pltpu.roll(M8, H, axis=0))
        TX = jnp.concatenate([T8, zpad], axis=0).T
        valid = lane_x < H
        bX = jnp.where(valid, TX, 0.0)
        cmX = jnp.where(valid, pltpu.roll(TX, V7X_LANES - H, axis=1), 0.0)
        igX = jnp.where(valid, pltpu.roll(G, V7X_LANES - H, axis=1), 0.0)
        mmaxX = jnp.maximum(cmX, m_row)
        bL = bX[L - 1:L, :]
        gkX = bL - bX + igX
        m_new = jnp.maximum(bL + m_row, jnp.max(gkX, axis=0, keepdims=True))
        return dict(C8=C8, mmaxX=mmaxX, m_new=m_new,
                    w_interX=jnp.exp(m_row - mmaxX),
                    enegX=jnp.exp(-(bX + mmaxX)),
                    decay=jnp.exp(bL + m_row - m_new),
                    wkX=jnp.exp(gkX - m_new))

    issue()
    gps = [gate_prep(0, m_row)]
    issue()
    d_mains = pool_window()
    gps.append(gate_prep(L, gps[0]["m_new"]))
    issue()
    pool_mix(d_mains)
    issue(2)

    for c in range(ts // L):
        r0 = c * L
        C8, mmaxX, w_interX, enegX, decay, wkX = (gps[c][n] for n in
                                                 ("C8", "mmaxX", "w_interX", "enegX", "decay", "wkX"))
        per_head = []
        rsX = None
        qnX = None
        for h in range(H):
            hc = h * DH
            q = conv_silu(hc, r0) * (DH ** -0.5)
            k = conv_silu(MLSTM_WIDTH + hc, r0)
            v_aug = jnp.concatenate([v_b[r0:r0 + L, hc:hc + DH], onehot[h]], axis=1)
            qb = q.astype(BF16)
            S = _dot_nt(qb, k.astype(BF16))
            qc = _dot(qb, cs[h].astype(BF16))
            issue()
            E = jnp.exp(jnp.where(causal, C8[h:h + 1, :] - mmaxX[:, h:h + 1], -jnp.inf))
            av = _dot((S * E).astype(BF16), v_aug)
            rsX = av[:, DH:] if rsX is None else rsX + av[:, DH:]
            qnX = qc[:, DH:] if qnX is None else qnX + qc[:, DH:]
            per_head.append(av[:, :DH] + w_interX[:, h:h + 1] * qc[:, :DH])
            kw = wkX[:, h:h + 1] * k
            cs[h] = decay[:, h:h + 1] * cs[h] + _dot_tn(kw.astype(BF16), v_aug)
            if h == 3:
                issue()

        denX = rsX + w_interX * qnX
        rX = 1.0 / jnp.maximum(jnp.abs(denX), enegX)

        for h in range(H):
            hc = h * DH
            hout = per_head[h] * rX[:, h:h + 1]
            hn = hout * lax.rsqrt(jnp.mean(hout * hout, axis=1, keepdims=True) + EPS)
            og = _sigmoid(og_b[r0:r0 + L, hc:hc + DH])
            mout = hn * gnorm_ref[:, hc:hc + DH] * og
            mix_b[r0:r0 + L, POOL_WIDTH + hc:POOL_WIDTH + hc + DH] = mout.astype(BF16)
    m_row = gps[-1]["m_new"]
    assert not pieces, "every projection piece must have an issue point"
    if "proj" in stages:
        history_rows()

    for h in range(H):
        c_ref[h] = cs[h]
    m_ref[0:1, :] = m_row


def _mixer(l, x2, gpre, win, wgate, bias, conv, pmat, wpool, pscale, gnorm, wout, gpost, seq):
    T = x2.shape[0]
    ts = TS_MIX
    nt = seq // ts
    ntot = T // ts
    return pl.pallas_call(
        functools.partial(_mixer_kernel, nt=nt, ntot=ntot),
        grid=(ntot + 2,),
        in_specs=[
            pl.BlockSpec((ts, D_MODEL), lambda s: (jnp.minimum(s, ntot - 1), 0)),
            pl.BlockSpec((ts, D_MODEL), lambda s: (jnp.maximum(s - 2, 0), 0)),
            _vector_spec(D_MODEL, l, G_MIX_PRE),
            _layer_spec((D_MODEL, OFF_GATE), l),
            _layer_spec((D_MODEL, V7X_LANES), l),
            _layer_spec((1, V7X_LANES), l),
            _layer_spec((CONV_WIDTH, 2 * MLSTM_WIDTH), l),
            _const_spec((POOL_GROUPS, POOL_BLOCK, POOL_BLOCK)),
            _layer_spec((POOL_GROUPS, POOL_GROUP_DIM, POOL_GROUP_DIM), l),
            _vector_spec(POOL_WIDTH, l, V_POOL_SCALE),
            _vector_spec(MLSTM_WIDTH, l, V_MLSTM_NORM),
            _layer_spec((D_MODEL, D_MODEL), l),
            _vector_spec(D_MODEL, l, G_MIX_POST),
        ],
        out_specs=pl.BlockSpec((ts, D_MODEL), lambda s: (jnp.maximum(s - 2, 0), 0)),
        out_shape=jax.ShapeDtypeStruct((T, D_MODEL), F32),
        scratch_shapes=[
            *[pltpu.VMEM((ts + POOL_HALO, POOL_WIDTH), F32)] * 2,
            *[pltpu.VMEM((ts + CONV_HALO, 2 * MLSTM_WIDTH), F32)] * 2,
            *[pltpu.VMEM((ts, MLSTM_WIDTH), BF16)] * 2,
            *[pltpu.VMEM((ts, MLSTM_WIDTH), F32)] * 2,
            *[pltpu.VMEM((ts, V7X_LANES), F32)] * 2,
            *[pltpu.VMEM((ts, D_MODEL), BF16)] * 2,
            pltpu.VMEM((MLSTM_HEADS, MLSTM_HEAD_DIM, 2 * MLSTM_HEAD_DIM), F32),
            pltpu.VMEM((V7X_SUBLANES, V7X_LANES), F32),
        ],
        compiler_params=pltpu.CompilerParams(
            dimension_semantics=("arbitrary",), vmem_limit_bytes=V7X_VMEM_LIMIT_BYTES),
        name="mixer",
    )(x2, x2, gpre, win, wgate, bias, conv, pmat, wpool, pscale, gnorm, wout, gpost)


def _memkv_kernel(mem_ref, g_ref, wk_ref, wv_ref, k_ref, v_ref):
    mb = _rmsnorm(mem_ref[0], g_ref[...]).astype(BF16)
    k_ref[0] = _dot_w(mb, wk_ref[...]).astype(BF16)
    v_ref[0] = _dot_w(mb, wv_ref[...]).astype(BF16)


def _memkv(mem, g, wk, wv):
    depth = wk.shape[0]
    B = mem.shape[0]
    weight = pl.BlockSpec((None, D_MODEL, D_MODEL), lambda l, b: (l, 0, 0))
    out = pl.BlockSpec((None, 1, N_MEM, D_MODEL), lambda l, b: (l, b, 0, 0))
    return pl.pallas_call(
        _memkv_kernel,
        grid=(depth, B),
        in_specs=[pl.BlockSpec((1, N_MEM, D_MODEL), lambda l, b: (b, 0, 0)),
                  pl.BlockSpec((None, None, 1, D_MODEL), lambda l, b: (l, G_MEM, 0, 0)),
                  weight, weight],
        out_specs=[out, out],
        out_shape=[jax.ShapeDtypeStruct((depth, B, N_MEM, D_MODEL), BF16)] * 2,
        compiler_params=pltpu.CompilerParams(
            dimension_semantics=("arbitrary", "arbitrary"), vmem_limit_bytes=V7X_VMEM_LIMIT_BYTES),
        name="memkv",
    )(mem, g, wk, wv)


def _xattn_kernel(x_ref, k_ref, v_ref, gpre_ref, wq_ref, wo_ref, gpost_ref, o_ref, att_ref):
    halves = [slice(r0, r0 + SUB_ROWS) for r0 in range(0, x_ref.shape[0], SUB_ROWS)]
    qbs = []
    for rows in halves:
        hb = _rmsnorm(x_ref[rows, :], gpre_ref[...]).astype(BF16)
        qbs.append((_dot(hb, wq_ref[...]) * (XATTN_HEAD_DIM ** -0.5)).astype(BF16))
    for h in range(XATTN_HEADS):
        cols = slice(h * XATTN_HEAD_DIM, (h + 1) * XATTN_HEAD_DIM)
        scores = [_dot_nt(qb[:, cols], k_ref[0, :, cols]) for qb in qbs]
        for rows, s in zip(halves, scores):
            p = jnp.exp(s - jnp.max(s, axis=1, keepdims=True))
            o = _dot(p.astype(BF16), v_ref[0, :, cols]) / jnp.sum(p, axis=1, keepdims=True)
            att_ref[rows, cols] = o.astype(BF16)
    for rows in halves:
        y = _dot(att_ref[rows, :], wo_ref[...])
        o_ref[rows, :] = x_ref[rows, :] + _rmsnorm(y, gpost_ref[...])


def _xattn(l, x2, k, v, gpre, wq, wo, gpost, batch, seq):
    T = x2.shape[0]
    ts = TS_ATT
    nt = seq // ts
    tok = lambda b, t: (b * nt + t, 0)
    kvb = pl.BlockSpec((None, 1, N_MEM, D_MODEL), lambda b, t: (l, b, 0, 0))
    return pl.pallas_call(
        _xattn_kernel,
        grid=(batch, nt),
        in_specs=[
            pl.BlockSpec((ts, D_MODEL), tok), kvb, kvb,
            _vector_spec(D_MODEL, l, G_XATTN_PRE),
            _layer_spec((D_MODEL, D_MODEL), l),
            _layer_spec((D_MODEL, D_MODEL), l),
            _vector_spec(D_MODEL, l, G_XATTN_POST),
        ],
        out_specs=pl.BlockSpec((ts, D_MODEL), tok),
        out_shape=jax.ShapeDtypeStruct((T, D_MODEL), F32),
        scratch_shapes=[pltpu.VMEM((ts, D_MODEL), BF16)],
        compiler_params=pltpu.CompilerParams(
            dimension_semantics=("arbitrary", "arbitrary"), vmem_limit_bytes=V7X_VMEM_LIMIT_BYTES),
        name="xattn",
    )(x2, k, v, gpre, wq, wo, gpost)


def _ffn_kernel(x_ref, gpre_ref, wg_ref, wu_ref, wd_ref, gpost_ref, o_ref):
    tf = TF_FFN
    halves = [slice(r0, r0 + SUB_ROWS_FFN) for r0 in range(0, x_ref.shape[0], SUB_ROWS_FFN)]
    hbs = [_rmsnorm(x_ref[rows, :], gpre_ref[...]).astype(BF16) for rows in halves]
    accs = [None] * len(halves)

    def down(i, a, cols):
        contrib = _dot(a, wd_ref[cols, :])
        accs[i] = contrib if accs[i] is None else accs[i] + contrib

    pending = None
    for c in range(wg_ref.shape[1] // tf):
        cols = slice(c * tf, (c + 1) * tf)
        for i, hb in enumerate(hbs):
            gate = _dot(hb, wg_ref[:, cols])
            a = (gate * _sigmoid(gate) * _dot(hb, wu_ref[:, cols])).astype(BF16)
            if pending is not None:
                down(*pending)
            pending = (i, a, cols)
    down(*pending)
    for rows, acc in zip(halves, accs):
        o_ref[rows, :] = x_ref[rows, :] + _rmsnorm(acc, gpost_ref[...])


def _ffn(l, x2, gpre, wg, wu, wd, gpost):
    T = x2.shape[0]
    tm = TM_FFN
    d_ff = wg.shape[-1]
    return pl.pallas_call(
        _ffn_kernel,
        grid=(T // tm,),
        in_specs=[
            pl.BlockSpec((tm, D_MODEL), lambda i: (i, 0)),
            _vector_spec(D_MODEL, l, G_FFN_PRE),
            _layer_spec((D_MODEL, d_ff), l),
            _layer_spec((D_MODEL, d_ff), l),
            _layer_spec((d_ff, D_MODEL), l),
            _vector_spec(D_MODEL, l, G_FFN_POST),
        ],
        out_specs=pl.BlockSpec((tm, D_MODEL), lambda i: (i, 0)),
        out_shape=jax.ShapeDtypeStruct((T, D_MODEL), F32),
        compiler_params=pltpu.CompilerParams(
            dimension_semantics=("arbitrary",), vmem_limit_bytes=V7X_VMEM_LIMIT_BYTES),
        name="ffn",
    )(x2, gpre, wg, wu, wd, gpost)


def kernel(x, mem, w_in, b_gate, conv_qk, w_pool, pool_scale, mlstm_norm_g, w_out, g_mix_pre, g_mix_post, g_mem, g_xattn_pre, g_xattn_post, wq_x, wk_x, wv_x, wo_x, g_ffn_pre, g_ffn_post, w_gate, w_up, w_down):
    B, S, D = x.shape
    depth = w_in.shape[0]
    H = MLSTM_HEADS
    x2 = x.reshape(B * S, D)
    pmat = _pool_matrices()
    rows = lambda v: v.reshape(depth, 1, -1).astype(F32)
    table = lambda vs: jnp.stack([v.astype(F32) for v in vs], axis=1)[:, :, None, :]
    gains = table([g_mix_pre, g_mix_post, g_mem, g_xattn_pre, g_xattn_post, g_ffn_pre, g_ffn_post])
    mixvecs = table([pool_scale, mlstm_norm_g])
    bf = lambda w: w.astype(BF16)
    w_in_b = bf(w_in)
    wgate = jnp.concatenate(
        [w_in_b[:, :, OFF_GATE + H:], w_in_b[:, :, OFF_GATE:OFF_GATE + H],
         jnp.zeros((depth, D, V7X_LANES - 2 * H), BF16)], axis=2)
    bias = jnp.concatenate(
        [b_gate[:, H:], b_gate[:, :H], jnp.zeros((depth, V7X_LANES - 2 * H), b_gate.dtype)], axis=1)
    mixer_params = (gains, w_in_b, wgate, rows(bias), conv_qk.astype(F32), pmat, bf(w_pool),
                    mixvecs, mixvecs, bf(w_out), gains)
    xattn_params = (gains, bf(wq_x), bf(wo_x), gains)
    ffn_params = (gains, bf(w_gate), bf(w_up), bf(w_down), gains)
    k, v = _memkv(mem, gains, wk_x, wv_x)

    for l in range(depth):
        x2 = _mixer(l, x2, *mixer_params, S)
        x2 = _xattn(l, x2, k, v, *xattn_params, B, S)
        x2 = _ffn(l, x2, *ffn_params)

    return x2.reshape(B, S, D)
```

```python
import functools

import numpy as np
import jax
import jax.numpy as jnp
from jax import lax
from jax.experimental import pallas as pl
from jax.experimental.pallas import tpu as pltpu

F32 = jnp.float32
BF16 = jnp.bfloat16

EPS = 1e-6
D_MODEL = 1024
N_MEM = 256
POOL_WIDTH = 512
POOL_GROUPS = 4
POOL_GROUP_DIM = 128
POOL_WINDOWS = (2, 4, 8, 16)
MLSTM_WIDTH = 512
MLSTM_HEADS = 4
MLSTM_HEAD_DIM = 128
CONV_WIDTH = 4
XATTN_HEADS = 4
XATTN_HEAD_DIM = 256
OFF_Q = POOL_WIDTH
OFF_K = OFF_Q + MLSTM_WIDTH
OFF_V = OFF_K + MLSTM_WIDTH
OFF_O = OFF_V + MLSTM_WIDTH
OFF_GATE = OFF_O + MLSTM_WIDTH

V7X_LANES = 128
V7X_SUBLANES = 8
V7X_VMEM_LIMIT_BYTES = 56 * 1024 * 1024

G_MIX_PRE, G_MIX_POST, G_MEM, G_XATTN_PRE, G_XATTN_POST, G_FFN_PRE, G_FFN_POST = range(7)
V_POOL_SCALE, V_MLSTM_NORM = range(2)

IN_COLS_PAD = OFF_GATE + V7X_LANES
POOL_HALO = 16
CONV_HALO = 8
POOL_BLOCK = 256
PIECE_COLS = 256

TS_MIX = 512
L_CHUNK = 256
TS_ATT = 2048
TM_FFN = 1024
SUB_ROWS = 512
SUB_ROWS_FFN = 256
TF_FFN = 256


def _rmsnorm(x, g):
    return x * lax.rsqrt(jnp.mean(x * x, axis=-1, keepdims=True) + EPS) * g


def _sigmoid(x):
    return 1.0 / (1.0 + jnp.exp(-x))


def _log_sigmoid(x):
    return jnp.minimum(x, 0.0) - jnp.log(1.0 + jnp.exp(-jnp.abs(x)))


def _dot(a, b):
    return jnp.dot(a, b, preferred_element_type=F32)


def _dot_w(a, w):
    return lax.dot_general(a, w, (((1,), (0,)), ((), ())), preferred_element_type=F32)


def _dot_nt(a, b):
    return lax.dot_general(a, b, (((1,), (1,)), ((), ())), preferred_element_type=F32)


def _dot_tn(a, b):
    return lax.dot_general(a, b, (((0,), (0,)), ((), ())), preferred_element_type=F32)


def _const_spec(shape):
    nd = len(shape)
    return pl.BlockSpec(shape, lambda *_: (0,) * nd, pipeline_mode=pl.Buffered(1))


def _vector_spec(n, l, k):
    return pl.BlockSpec((None, None, 1, n), lambda *_: (l, k, 0, 0), pipeline_mode=pl.Buffered(1))


def _layer_spec(shape, l):
    nd = len(shape)
    return pl.BlockSpec((None,) + tuple(shape), lambda *_: (l,) + (0,) * nd, pipeline_mode=pl.Buffered(1))


def _pool_matrices():
    i = np.arange(POOL_BLOCK)[:, None]
    j = np.arange(POOL_BLOCK)[None, :]
    mats = []
    for win in POOL_WINDOWS:
        inside = (j <= i) & (j > i - win)
        mats.append(inside.astype(np.float32) / win - (i == j).astype(np.float32))
    return jnp.asarray(np.stack(mats), dtype=BF16)


def _mixer_kernel(*refs, nt, ntot):
    s = pl.program_id(0)

    def variant(step, stages):
        return functools.partial(_mixer_step, step % 2, 1 - step % 2, *refs, nt=nt, stages=stages)

    steady = jnp.logical_and(s >= 1, s <= ntot)
    pl.when(s == 0)(variant(0, ("proj",)))
    pl.when(jnp.logical_and(steady, lax.rem(s, 2) == 0))(variant(0, ("proj", "mix", "out")))
    pl.when(jnp.logical_and(steady, lax.rem(s, 2) == 1))(variant(1, ("proj", "mix", "out")))
    pl.when(s == ntot + 1)(variant(ntot + 1, ("out",)))


def _mixer_step(a, b, xn_ref, xb_ref, gpre_ref, win_ref, wgate_ref, bias_ref, conv_ref, pmat_ref, wpool_ref,
                pscale_ref, gnorm_ref, wout_ref, gpost_ref, o_ref,
                u0_ref, u1_ref, qk0_ref, qk1_ref, v0_ref, v1_ref, og0_ref, og1_ref,
                g0_ref, g1_ref, mix0_ref, mix1_ref, c_ref, m_ref, *, nt, stages):
    uext_a, uext_b = (u0_ref, u1_ref)[a], (u0_ref, u1_ref)[b]
    qkext_a, qkext_b = (qk0_ref, qk1_ref)[a], (qk0_ref, qk1_ref)[b]
    v_a, v_b = (v0_ref, v1_ref)[a], (v0_ref, v1_ref)[b]
    og_a, og_b = (og0_ref, og1_ref)[a], (og0_ref, og1_ref)[b]
    g_a, g_b = (g0_ref, g1_ref)[a], (g0_ref, g1_ref)[b]
    mix_a, mix_b = (mix0_ref, mix1_ref)[a], (mix0_ref, mix1_ref)[b]
    ts = xn_ref.shape[0]
    L = L_CHUNK
    H = MLSTM_HEADS
    DH = MLSTM_HEAD_DIM
    PC = PIECE_COLS
    s = pl.program_id(0)
    first_a = lax.rem(s, nt) == 0
    tb = lax.rem(s + nt - 1, nt)
    first_b = tb == 0

    hb = _rmsnorm(xn_ref[...], gpre_ref[...]).astype(BF16) if "proj" in stages else None

    out_parts = []

    def out_piece(j):
        def run():
            out_parts.append(_dot(mix_a[...], wout_ref[:, j * PC:(j + 1) * PC]))
            if len(out_parts) == D_MODEL // PC:
                ss = sum(jnp.sum(y * y, axis=1, keepdims=True) for y in out_parts)
                rs = lax.rsqrt(ss * (1.0 / D_MODEL) + EPS)
                for jj, y in enumerate(out_parts):
                    cols = slice(jj * PC, (jj + 1) * PC)
                    o_ref[:, cols] = xb_ref[:, cols] + y * rs * gpost_ref[:, cols]
        return run

    def proj_piece(c0):
        def run():
            w = win_ref[:, c0:c0 + PC] if c0 < OFF_GATE else wgate_ref[...]
            y = _dot(hb, w)
            if c0 < OFF_Q:
                uext_a[POOL_HALO:POOL_HALO + ts, c0:c0 + PC] = y
            elif c0 < OFF_V:
                qkext_a[CONV_HALO:CONV_HALO + ts, c0 - OFF_Q:c0 - OFF_Q + PC] = y
            elif c0 < OFF_O:
                v_a[:, c0 - OFF_V:c0 - OFF_V + PC] = y.astype(BF16)
            elif c0 < OFF_GATE:
                og_a[:, c0 - OFF_O:c0 - OFF_O + PC] = y
            else:
                g_a[...] = y + bias_ref[...]
        return run

    pieces = []
    if "out" in stages:
        pieces += [out_piece(j) for j in range(D_MODEL // PC)]
    if "proj" in stages:
        pieces += [proj_piece(c0) for c0 in range(0, IN_COLS_PAD, PC)]
    pieces.reverse()

    def issue(n=1):
        for _ in range(n):
            if pieces:
                pieces.pop()()

    def history_rows():
        if "mix" in stages:
            uext_a[0:POOL_HALO, :] = jnp.where(first_a, 0.0, uext_b[ts:ts + POOL_HALO, :])
            qkext_a[0:CONV_HALO, :] = jnp.where(first_a, 0.0, qkext_b[ts:ts + CONV_HALO, :])
        else:
            uext_a[0:POOL_HALO, :] = jnp.zeros((POOL_HALO, POOL_WIDTH), F32)
            qkext_a[0:CONV_HALO, :] = jnp.zeros((CONV_HALO, 2 * MLSTM_WIDTH), F32)

    if "mix" not in stages:
        issue(len(pieces))
        if "proj" in stages:
            history_rows()
            for ref in (mix_b, c_ref, m_ref):
                ref[...] = jnp.zeros_like(ref)
        return

    def pool_window():
        d_mains = []
        for blk in range(ts // POOL_BLOCK):
            r0 = blk * POOL_BLOCK
            for g in range(POOL_GROUPS):
                cols = slice(g * POOL_GROUP_DIM, (g + 1) * POOL_GROUP_DIM)
                ublk = uext_b[POOL_HALO + r0:POOL_HALO + r0 + POOL_BLOCK, cols]
                d_mains.append(_dot(pmat_ref[g], ublk.astype(BF16)))
        return d_mains

    def pool_mix(d_mains):
        for blk in range(ts // POOL_BLOCK):
            r0 = blk * POOL_BLOCK
            pos = tb * ts + r0 + lax.broadcasted_iota(jnp.int32, (POOL_HALO, 1), 0)
            for g, win in enumerate(POOL_WINDOWS):
                cols = slice(g * POOL_GROUP_DIM, (g + 1) * POOL_GROUP_DIM)
                cur = uext_b[POOL_HALO + r0:POOL_HALO + r0 + POOL_HALO, cols]
                acc = cur
                for k in range(1, win):
                    acc = acc + uext_b[POOL_HALO + r0 - k:POOL_HALO + r0 - k + POOL_HALO, cols]
                count = jnp.minimum(pos + 1, win).astype(F32)
                d_head = acc / count - cur
                d = jnp.concatenate([d_head, d_mains[blk * POOL_GROUPS + g][POOL_HALO:, :]], axis=0)
                y = _dot(d.astype(BF16), wpool_ref[g]) * pscale_ref[:, cols]
                mix_b[r0:r0 + POOL_BLOCK, cols] = y.astype(BF16)

    ri = lax.broadcasted_iota(jnp.int32, (L, L), 0)
    ci = lax.broadcasted_iota(jnp.int32, (L, L), 1)
    causal = ci <= ri
    lane_x = lax.broadcasted_iota(jnp.int32, (L, V7X_LANES), 1)
    sub8 = lax.broadcasted_iota(jnp.int32, (V7X_SUBLANES, L), 0)
    lane8 = lax.broadcasted_iota(jnp.int32, (V7X_SUBLANES, L), 1)
    zpad = jnp.zeros((V7X_LANES - V7X_SUBLANES, L), F32)
    onehot = [(lane_x == h).astype(BF16) for h in range(H)]
    cw = conv_ref[...]

    cs = [jnp.where(first_b, 0.0, c_ref[h]) for h in range(H)]
    m_row = jnp.where(first_b, 0.0, m_ref[0:1, :])

    def conv_silu(col0, r0):
        acc = None
        for j in range(CONV_WIDTH):
            off = CONV_HALO + r0 - (CONV_WIDTH - 1) + j
            term = cw[j:j + 1, col0:col0 + DH] * qkext_b[off:off + L, col0:col0 + DH]
            acc = term if acc is None else acc + term
        return acc * _sigmoid(acc)

    def lane_scan(x, op, fill):
        sh = 1
        while sh < L:
            x = op(x, jnp.where(lane8 >= sh, pltpu.roll(x, sh, axis=1), fill))
            sh *= 2
        return x

    def gate_prep(r0, m_row):
        G = g_b[r0:r0 + L, :]
        R8 = G.T[0:V7X_SUBLANES, :]
        low = sub8 < H
        B8 = lane_scan(jnp.where(low, _log_sigmoid(R8), 0.0), jnp.add, 0.0)
        C8 = jnp.where(low, pltpu.roll(R8, H, axis=0) - B8, 0.0)
        M8 = lane_scan(C8, jnp.maximum, -jnp.inf)
        T8 = jnp.where(low, B8, pltpu.roll(M8, H, axis=0))
        TX = jnp.concatenate([T8, zpad], axis=0).T
        valid = lane_x < H
        bX = jnp.where(valid, TX, 0.0)
        cmX = jnp.where(valid, pltpu.roll(TX, V7X_LANES - H, axis=1), 0.0)
        igX = jnp.where(valid, pltpu.roll(G, V7X_LANES - H, axis=1), 0.0)
        mmaxX = jnp.maximum(cmX, m_row)
        bL = bX[L - 1:L, :]
        gkX = bL - bX + igX
        m_new = jnp.maximum(bL + m_row, jnp.max(gkX, axis=0, keepdims=True))
        return dict(C8=C8, mmaxX=mmaxX, m_new=m_new,
                    w_interX=jnp.exp(m_row - mmaxX),
                    enegX=jnp.exp(-(bX + mmaxX)),
                    decay=jnp.exp(bL + m_row - m_new),
                    wkX=jnp.exp(gkX - m_new))

    issue()
    gps = [gate_prep(0, m_row)]
    issue()
    d_mains = pool_window()
    gps.append(gate_prep(L, gps[0]["m_new"]))
    issue()
    pool_mix(d_mains)
    issue(2)

    for c in range(ts // L):
        r0 = c * L
        C8, mmaxX, w_interX, enegX, decay, wkX = (gps[c][n] for n in
                                                 ("C8", "mmaxX", "w_interX", "enegX", "decay", "wkX"))
        per_head = []
        rsX = None
        qnX = None
        for h in range(H):
            hc = h * DH
            q = conv_silu(hc, r0) * (DH ** -0.5)
            k = conv_silu(MLSTM_WIDTH + hc, r0)
            v_aug = jnp.concatenate([v_b[r0:r0 + L, hc:hc + DH], onehot[h]], axis=1)
            qb = q.astype(BF16)
            S = _dot_nt(qb, k.astype(BF16))
            qc = _dot(qb, cs[h].astype(BF16))
            issue()
            E = jnp.exp(jnp.where(causal, C8[h:h + 1, :] - mmaxX[:, h:h + 1], -jnp.inf))
            av = _dot((S * E).astype(BF16), v_aug)
            rsX = av[:, DH:] if rsX is None else rsX + av[:, DH:]
            qnX = qc[:, DH:] if qnX is None else qnX + qc[:, DH:]
            per_head.append(av[:, :DH] + w_interX[:, h:h + 1] * qc[:, :DH])
            kw = wkX[:, h:h + 1] * k
            cs[h] = decay[:, h:h + 1] * cs[h] + _dot_tn(kw.astype(BF16), v_aug)
            if c > 0 and h % 2 == 0:
                issue()

        denX = rsX + w_interX * qnX
        rX = 1.0 / jnp.maximum(jnp.abs(denX), enegX)

        for h in range(H):
            hc = h * DH
            hout = per_head[h] * rX[:, h:h + 1]
            hn = hout * lax.rsqrt(jnp.mean(hout * hout, axis=1, keepdims=True) + EPS)
            og = _sigmoid(og_b[r0:r0 + L, hc:hc + DH])
            mout = hn * gnorm_ref[:, hc:hc + DH] * og
            mix_b[r0:r0 + L, POOL_WIDTH + hc:POOL_WIDTH + hc + DH] = mout.astype(BF16)
    m_row = gps[-1]["m_new"]
    assert not pieces, "every projection piece must have an issue point"
    if "proj" in stages:
        history_rows()

    for h in range(H):
        c_ref[h] = cs[h]
    m_ref[0:1, :] = m_row


def _mixer(l, x2, gpre, win, wgate, bias, conv, pmat, wpool, pscale, gnorm, wout, gpost, seq):
    T = x2.shape[0]
    ts = TS_MIX
    nt = seq // ts
    ntot = T // ts
    return pl.pallas_call(
        functools.partial(_mixer_kernel, nt=nt, ntot=ntot),
        grid=(ntot + 2,),
        in_specs=[
            pl.BlockSpec((ts, D_MODEL), lambda s: (jnp.minimum(s, ntot - 1), 0)),
            pl.BlockSpec((ts, D_MODEL), lambda s: (jnp.maximum(s - 2, 0), 0)),
            _vector_spec(D_MODEL, l, G_MIX_PRE),
            _layer_spec((D_MODEL, OFF_GATE), l),
            _layer_spec((D_MODEL, V7X_LANES), l),
            _layer_spec((1, V7X_LANES), l),
            _layer_spec((CONV_WIDTH, 2 * MLSTM_WIDTH), l),
            _const_spec((POOL_GROUPS, POOL_BLOCK, POOL_BLOCK)),
            _layer_spec((POOL_GROUPS, POOL_GROUP_DIM, POOL_GROUP_DIM), l),
            _vector_spec(POOL_WIDTH, l, V_POOL_SCALE),
            _vector_spec(MLSTM_WIDTH, l, V_MLSTM_NORM),
            _layer_spec((D_MODEL, D_MODEL), l),
            _vector_spec(D_MODEL, l, G_MIX_POST),
        ],
        out_specs=pl.BlockSpec((ts, D_MODEL), lambda s: (jnp.maximum(s - 2, 0), 0)),
        out_shape=jax.ShapeDtypeStruct((T, D_MODEL), F32),
        scratch_shapes=[
            *[pltpu.VMEM((ts + POOL_HALO, POOL_WIDTH), F32)] * 2,
            *[pltpu.VMEM((ts + CONV_HALO, 2 * MLSTM_WIDTH), F32)] * 2,
            *[pltpu.VMEM((ts, MLSTM_WIDTH), BF16)] * 2,
            *[pltpu.VMEM((ts, MLSTM_WIDTH), F32)] * 2,
            *[pltpu.VMEM((ts, V7X_LANES), F32)] * 2,
            *[pltpu.VMEM((ts, D_MODEL), BF16)] * 2,
            pltpu.VMEM((MLSTM_HEADS, MLSTM_HEAD_DIM, 2 * MLSTM_HEAD_DIM), F32),
            pltpu.VMEM((V7X_SUBLANES, V7X_LANES), F32),
        ],
        compiler_params=pltpu.CompilerParams(
            dimension_semantics=("arbitrary",), vmem_limit_bytes=V7X_VMEM_LIMIT_BYTES),
        name="mixer",
    )(x2, x2, gpre, win, wgate, bias, conv, pmat, wpool, pscale, gnorm, wout, gpost)


def _memkv_kernel(mem_ref, g_ref, wk_ref, wv_ref, k_ref, v_ref):
    mb = _rmsnorm(mem_ref[0], g_ref[...]).astype(BF16)
    k_ref[0] = _dot_w(mb, wk_ref[...]).astype(BF16)
    v_ref[0] = _dot_w(mb, wv_ref[...]).astype(BF16)


def _memkv(mem, g, wk, wv):
    depth = wk.shape[0]
    B = mem.shape[0]
    weight = pl.BlockSpec((None, D_MODEL, D_MODEL), lambda l, b: (l, 0, 0))
    out = pl.BlockSpec((None, 1, N_MEM, D_MODEL), lambda l, b: (l, b, 0, 0))
    return pl.pallas_call(
        _memkv_kernel,
        grid=(depth, B),
        in_specs=[pl.BlockSpec((1, N_MEM, D_MODEL), lambda l, b: (b, 0, 0)),
                  pl.BlockSpec((None, None, 1, D_MODEL), lambda l, b: (l, G_MEM, 0, 0)),
                  weight, weight],
        out_specs=[out, out],
        out_shape=[jax.ShapeDtypeStruct((depth, B, N_MEM, D_MODEL), BF16)] * 2,
        compiler_params=pltpu.CompilerParams(
            dimension_semantics=("arbitrary", "arbitrary"), vmem_limit_bytes=V7X_VMEM_LIMIT_BYTES),
        name="memkv",
    )(mem, g, wk, wv)


def _xattn_kernel(x_ref, k_ref, v_ref, gpre_ref, wq_ref, wo_ref, gpost_ref, o_ref, att_ref):
    halves = [slice(r0, r0 + SUB_ROWS) for r0 in range(0, x_ref.shape[0], SUB_ROWS)]
    qbs = []
    for rows in halves:
        hb = _rmsnorm(x_ref[rows, :], gpre_ref[...]).astype(BF16)
        qbs.append((_dot(hb, wq_ref[...]) * (XATTN_HEAD_DIM ** -0.5)).astype(BF16))
    for h in range(XATTN_HEADS):
        cols = slice(h * XATTN_HEAD_DIM, (h + 1) * XATTN_HEAD_DIM)
        scores = [_dot_nt(qb[:, cols], k_ref[0, :, cols]) for qb in qbs]
        for rows, s in zip(halves, scores):
            p = jnp.exp(s - jnp.max(s, axis=1, keepdims=True))
            o = _dot(p.astype(BF16), v_ref[0, :, cols]) / jnp.sum(p, axis=1, keepdims=True)
            att_ref[rows, cols] = o.astype(BF16)
    for rows in halves:
        y = _dot(att_ref[rows, :], wo_ref[...])
        o_ref[rows, :] = x_ref[rows, :] + _rmsnorm(y, gpost_ref[...])


def _xattn(l, x2, k, v, gpre, wq, wo, gpost, batch, seq):
    T = x2.shape[0]
    ts = TS_ATT
    nt = seq // ts
    tok = lambda b, t: (b * nt + t, 0)
    kvb = pl.BlockSpec((None, 1, N_MEM, D_MODEL), lambda b, t: (l, b, 0, 0))
    return pl.pallas_call(
        _xattn_kernel,
        grid=(batch, nt),
        in_specs=[
            pl.BlockSpec((ts, D_MODEL), tok), kvb, kvb,
            _vector_spec(D_MODEL, l, G_XATTN_PRE),
            _layer_spec((D_MODEL, D_MODEL), l),
            _layer_spec((D_MODEL, D_MODEL), l),
            _vector_spec(D_MODEL, l, G_XATTN_POST),
        ],
        out_specs=pl.BlockSpec((ts, D_MODEL), tok),
        out_shape=jax.ShapeDtypeStruct((T, D_MODEL), F32),
        scratch_shapes=[pltpu.VMEM((ts, D_MODEL), BF16)],
        compiler_params=pltpu.CompilerParams(
            dimension_semantics=("arbitrary", "arbitrary"), vmem_limit_bytes=V7X_VMEM_LIMIT_BYTES),
        name="xattn",
    )(x2, k, v, gpre, wq, wo, gpost)


def _ffn_kernel(x_ref, gpre_ref, wg_ref, wu_ref, wd_ref, gpost_ref, o_ref):
    tf = TF_FFN
    halves = [slice(r0, r0 + SUB_ROWS_FFN) for r0 in range(0, x_ref.shape[0], SUB_ROWS_FFN)]
    hbs = [_rmsnorm(x_ref[rows, :], gpre_ref[...]).astype(BF16) for rows in halves]
    accs = [None] * len(halves)

    def down(i, a, cols):
        contrib = _dot(a, wd_ref[cols, :])
        accs[i] = contrib if accs[i] is None else accs[i] + contrib

    pending = None
    for c in range(wg_ref.shape[1] // tf):
        cols = slice(c * tf, (c + 1) * tf)
        for i, hb in enumerate(hbs):
            gate = _dot(hb, wg_ref[:, cols])
            a = (gate * _sigmoid(gate) * _dot(hb, wu_ref[:, cols])).astype(BF16)
            if pending is not None:
                down(*pending)
            pending = (i, a, cols)
    down(*pending)
    for rows, acc in zip(halves, accs):
        o_ref[rows, :] = x_ref[rows, :] + _rmsnorm(acc, gpost_ref[...])


def _ffn(l, x2, gpre, wg, wu, wd, gpost):
    T = x2.shape[0]
    tm = TM_FFN
    d_ff = wg.shape[-1]
    return pl.pallas_call(
        _ffn_kernel,
        grid=(T // tm,),
        in_specs=[
            pl.BlockSpec((tm, D_MODEL), lambda i: (i, 0)),
            _vector_spec(D_MODEL, l, G_FFN_PRE),
            _layer_spec((D_MODEL, d_ff), l),
            _layer_spec((D_MODEL, d_ff), l),
            _layer_spec((d_ff, D_MODEL), l),
            _vector_spec(D_MODEL, l, G_FFN_POST),
        ],
        out_specs=pl.BlockSpec((tm, D_MODEL), lambda i: (i, 0)),
        out_shape=jax.ShapeDtypeStruct((T, D_MODEL), F32),
        compiler_params=pltpu.CompilerParams(
            dimension_semantics=("arbitrary",), vmem_limit_bytes=V7X_VMEM_LIMIT_BYTES),
        name="ffn",
    )(x2, gpre, wg, wu, wd, gpost)


def kernel(x, mem, w_in, b_gate, conv_qk, w_pool, pool_scale, mlstm_norm_g, w_out, g_mix_pre, g_mix_post, g_mem, g_xattn_pre, g_xattn_post, wq_x, wk_x, wv_x, wo_x, g_ffn_pre, g_ffn_post, w_gate, w_up, w_down):
    B, S, D = x.shape
    depth = w_in.shape[0]
    H = MLSTM_HEADS
    x2 = x.reshape(B * S, D)
    pmat = _pool_matrices()
    rows = lambda v: v.reshape(depth, 1, -1).astype(F32)
    table = lambda vs: jnp.stack([v.astype(F32) for v in vs], axis=1)[:, :, None, :]
    gains = table([g_mix_pre, g_mix_post, g_mem, g_xattn_pre, g_xattn_post, g_ffn_pre, g_ffn_post])
    mixvecs = table([pool_scale, mlstm_norm_g])
    bf = lambda w: w.astype(BF16)
    w_in_b = bf(w_in)
    wgate = jnp.concatenate(
        [w_in_b[:, :, OFF_GATE + H:], w_in_b[:, :, OFF_GATE:OFF_GATE + H],
         jnp.zeros((depth, D, V7X_LANES - 2 * H), BF16)], axis=2)
    bias = jnp.concatenate(
        [b_gate[:, H:], b_gate[:, :H], jnp.zeros((depth, V7X_LANES - 2 * H), b_gate.dtype)], axis=1)
    mixer_params = (gains, w_in_b, wgate, rows(bias), conv_qk.astype(F32), pmat, bf(w_pool),
                    mixvecs, mixvecs, bf(w_out), gains)
    xattn_params = (gains, bf(wq_x), bf(wo_x), gains)
    ffn_params = (gains, bf(w_gate), bf(w_up), bf(w_down), gains)
    k, v = _memkv(mem, gains, wk_x, wv_x)

    for l in range(depth):
        x2 = _mixer(l, x2, *mixer_params, S)
        x2 = _xattn(l, x2, k, v, *xattn_params, B, S)
        x2 = _ffn(l, x2, *ffn_params)

    return x2.reshape(B, S, D)
```

```python
import functools

import numpy as np
import jax
import jax.numpy as jnp
from jax import lax
from jax.experimental import pallas as pl
from jax.experimental.pallas import tpu as pltpu

F32 = jnp.float32
BF16 = jnp.bfloat16

EPS = 1e-6
D_MODEL = 1024
N_MEM = 256
POOL_WIDTH = 512
POOL_GROUPS = 4
POOL_GROUP_DIM = 128
POOL_WINDOWS = (2, 4, 8, 16)
MLSTM_WIDTH = 512
MLSTM_HEADS = 4
MLSTM_HEAD_DIM = 128
CONV_WIDTH = 4
XATTN_HEADS = 4
XATTN_HEAD_DIM = 256
OFF_Q = POOL_WIDTH
OFF_K = OFF_Q + MLSTM_WIDTH
OFF_V = OFF_K + MLSTM_WIDTH
OFF_O = OFF_V + MLSTM_WIDTH
OFF_GATE = OFF_O + MLSTM_WIDTH

V7X_LANES = 128
V7X_SUBLANES = 8
V7X_VMEM_LIMIT_BYTES = 56 * 1024 * 1024

G_MIX_PRE, G_MIX_POST, G_MEM, G_XATTN_PRE, G_XATTN_POST, G_FFN_PRE, G_FFN_POST = range(7)
V_POOL_SCALE, V_MLSTM_NORM = range(2)

IN_COLS_PAD = OFF_GATE + V7X_LANES
POOL_HALO = 16
CONV_HALO = 8
POOL_BLOCK = 256
PIECE_COLS = 256

TS_MIX = 512
L_CHUNK = 256
TS_ATT = 2048
TM_FFN = 1024
SUB_ROWS = 512
SUB_ROWS_FFN = 256
TF_FFN = 256


def _rmsnorm(x, g):
    return x * lax.rsqrt(jnp.mean(x * x, axis=-1, keepdims=True) + EPS) * g


def _sigmoid(x):
    return 1.0 / (1.0 + jnp.exp(-x))


def _log_sigmoid(x):
    return jnp.minimum(x, 0.0) - jnp.log(1.0 + jnp.exp(-jnp.abs(x)))


def _dot(a, b):
    return jnp.dot(a, b, preferred_element_type=F32)


def _dot_w(a, w):
    return lax.dot_general(a, w, (((1,), (0,)), ((), ())), preferred_element_type=F32)


def _dot_nt(a, b):
    return lax.dot_general(a, b, (((1,), (1,)), ((), ())), preferred_element_type=F32)


def _dot_tn(a, b):
    return lax.dot_general(a, b, (((0,), (0,)), ((), ())), preferred_element_type=F32)


def _const_spec(shape):
    nd = len(shape)
    return pl.BlockSpec(shape, lambda *_: (0,) * nd, pipeline_mode=pl.Buffered(1))


def _vector_spec(n, l, k):
    return pl.BlockSpec((None, None, 1, n), lambda *_: (l, k, 0, 0), pipeline_mode=pl.Buffered(1))


def _layer_spec(shape, l):
    nd = len(shape)
    return pl.BlockSpec((None,) + tuple(shape), lambda *_: (l,) + (0,) * nd, pipeline_mode=pl.Buffered(1))


def _pool_matrices():
    i = np.arange(POOL_BLOCK)[:, None]
    j = np.arange(POOL_BLOCK)[None, :]
    mats = []
    for win in POOL_WINDOWS:
        inside = (j <= i) & (j > i - win)
        mats.append(inside.astype(np.float32) / win - (i == j).astype(np.float32))
    return jnp.asarray(np.stack(mats), dtype=BF16)


def _mixer_kernel(*refs, nt, ntot):
    s = pl.program_id(0)

    def variant(step, stages):
        return functools.partial(_mixer_step, step % 2, 1 - step % 2, *refs, nt=nt, stages=stages)

    steady = jnp.logical_and(s >= 1, s <= ntot)
    pl.when(s == 0)(variant(0, ("proj",)))
    pl.when(jnp.logical_and(steady, lax.rem(s, 2) == 0))(variant(0, ("proj", "mix", "out")))
    pl.when(jnp.logical_and(steady, lax.rem(s, 2) == 1))(variant(1, ("proj", "mix", "out")))
    pl.when(s == ntot + 1)(variant(ntot + 1, ("out",)))


def _mixer_step(a, b, xn_ref, xb_ref, gpre_ref, win_ref, wgate_ref, bias_ref, conv_ref, pmat_ref, wpool_ref,
                pscale_ref, gnorm_ref, wout_ref, gpost_ref, o_ref,
                u0_ref, u1_ref, qk0_ref, qk1_ref, v0_ref, v1_ref, og0_ref, og1_ref,
                g0_ref, g1_ref, mix0_ref, mix1_ref, c_ref, m_ref, *, nt, stages):
    uext_a, uext_b = (u0_ref, u1_ref)[a], (u0_ref, u1_ref)[b]
    qkext_a, qkext_b = (qk0_ref, qk1_ref)[a], (qk0_ref, qk1_ref)[b]
    v_a, v_b = (v0_ref, v1_ref)[a], (v0_ref, v1_ref)[b]
    og_a, og_b = (og0_ref, og1_ref)[a], (og0_ref, og1_ref)[b]
    g_a, g_b = (g0_ref, g1_ref)[a], (g0_ref, g1_ref)[b]
    mix_a, mix_b = (mix0_ref, mix1_ref)[a], (mix0_ref, mix1_ref)[b]
    ts = xn_ref.shape[0]
    L = L_CHUNK
    H = MLSTM_HEADS
    DH = MLSTM_HEAD_DIM
    PC = PIECE_COLS
    s = pl.program_id(0)
    first_a = lax.rem(s, nt) == 0
    tb = lax.rem(s + nt - 1, nt)
    first_b = tb == 0

    hb = _rmsnorm(xn_ref[...], gpre_ref[...]).astype(BF16) if "proj" in stages else None

    out_parts = []

    def out_piece(j):
        def run():
            out_parts.append(_dot(mix_a[...], wout_ref[:, j * PC:(j + 1) * PC]))
            if len(out_parts) == D_MODEL // PC:
                ss = sum(jnp.sum(y * y, axis=1, keepdims=True) for y in out_parts)
                rs = lax.rsqrt(ss * (1.0 / D_MODEL) + EPS)
                for jj, y in enumerate(out_parts):
                    cols = slice(jj * PC, (jj + 1) * PC)
                    o_ref[:, cols] = xb_ref[:, cols] + y * rs * gpost_ref[:, cols]
        return run

    def proj_piece(c0):
        def run():
            w = win_ref[:, c0:c0 + PC] if c0 < OFF_GATE else wgate_ref[...]
            y = _dot(hb, w)
            if c0 < OFF_Q:
                uext_a[POOL_HALO:POOL_HALO + ts, c0:c0 + PC] = y
            elif c0 < OFF_V:
                qkext_a[CONV_HALO:CONV_HALO + ts, c0 - OFF_Q:c0 - OFF_Q + PC] = y
            elif c0 < OFF_O:
                v_a[:, c0 - OFF_V:c0 - OFF_V + PC] = y.astype(BF16)
            elif c0 < OFF_GATE:
                og_a[:, c0 - OFF_O:c0 - OFF_O + PC] = y
            else:
                g_a[...] = y + bias_ref[...]
        return run

    pieces = []
    if "out" in stages:
        pieces += [out_piece(j) for j in range(D_MODEL // PC)]
    if "proj" in stages:
        pieces += [proj_piece(c0) for c0 in range(0, IN_COLS_PAD, PC)]
    pieces.reverse()

    def issue(n=1):
        for _ in range(n):
            if pieces:
                pieces.pop()()

    def history_rows():
        if "mix" in stages:
            uext_a[0:POOL_HALO, :] = jnp.where(first_a, 0.0, uext_b[ts:ts + POOL_HALO, :])
            qkext_a[0:CONV_HALO, :] = jnp.where(first_a, 0.0, qkext_b[ts:ts + CONV_HALO, :])
        else:
            uext_a[0:POOL_HALO, :] = jnp.zeros((POOL_HALO, POOL_WIDTH), F32)
            qkext_a[0:CONV_HALO, :] = jnp.zeros((CONV_HALO, 2 * MLSTM_WIDTH), F32)

    if "mix" not in stages:
        issue(len(pieces))
        if "proj" in stages:
            history_rows()
            for ref in (mix_b, c_ref, m_ref):
                ref[...] = jnp.zeros_like(ref)
        return

    def pool_window():
        d_mains = []
        for blk in range(ts // POOL_BLOCK):
            r0 = blk * POOL_BLOCK
            for g in range(POOL_GROUPS):
                cols = slice(g * POOL_GROUP_DIM, (g + 1) * POOL_GROUP_DIM)
                ublk = uext_b[POOL_HALO + r0:POOL_HALO + r0 + POOL_BLOCK, cols]
                d_mains.append(_dot(pmat_ref[g], ublk.astype(BF16)))
        return d_mains

    def pool_mix(d_mains):
        for blk in range(ts // POOL_BLOCK):
            r0 = blk * POOL_BLOCK
            pos = tb * ts + r0 + lax.broadcasted_iota(jnp.int32, (POOL_HALO, 1), 0)
            for g, win in enumerate(POOL_WINDOWS):
                cols = slice(g * POOL_GROUP_DIM, (g + 1) * POOL_GROUP_DIM)
                cur = uext_b[POOL_HALO + r0:POOL_HALO + r0 + POOL_HALO, cols]
                acc = cur
                for k in range(1, win):
                    acc = acc + uext_b[POOL_HALO + r0 - k:POOL_HALO + r0 - k + POOL_HALO, cols]
                count = jnp.minimum(pos + 1, win).astype(F32)
                d_head = acc / count - cur
                d = jnp.concatenate([d_head, d_mains[blk * POOL_GROUPS + g][POOL_HALO:, :]], axis=0)
                y = _dot(d.astype(BF16), wpool_ref[g]) * pscale_ref[:, cols]
                mix_b[r0:r0 + POOL_BLOCK, cols] = y.astype(BF16)

    ri = lax.broadcasted_iota(jnp.int32, (L, L), 0)
    ci = lax.broadcasted_iota(jnp.int32, (L, L), 1)
    causal = ci <= ri
    lane_x = lax.broadcasted_iota(jnp.int32, (L, V7X_LANES), 1)
    sub8 = lax.broadcasted_iota(jnp.int32, (V7X_SUBLANES, L), 0)
    lane8 = lax.broadcasted_iota(jnp.int32, (V7X_SUBLANES, L), 1)
    zpad = jnp.zeros((V7X_LANES - V7X_SUBLANES, L), F32)
    onehot = [(lane_x == h).astype(BF16) for h in range(H)]
    cw = conv_ref[...]

    cs = [jnp.where(first_b, 0.0, c_ref[h]) for h in range(H)]
    m_row = jnp.where(first_b, 0.0, m_ref[0:1, :])

    def conv_silu(col0, r0):
        acc = None
        for j in range(CONV_WIDTH):
            off = CONV_HALO + r0 - (CONV_WIDTH - 1) + j
            term = cw[j:j + 1, col0:col0 + DH] * qkext_b[off:off + L, col0:col0 + DH]
            acc = term if acc is None else acc + term
        return acc * _sigmoid(acc)

    def lane_scan(x, op, fill):
        sh = 1
        while sh < L:
            x = op(x, jnp.where(lane8 >= sh, pltpu.roll(x, sh, axis=1), fill))
            sh *= 2
        return x

    def gate_prep(r0, m_row):
        G = g_b[r0:r0 + L, :]
        R8 = G.T[0:V7X_SUBLANES, :]
        low = sub8 < H
        B8 = lane_scan(jnp.where(low, _log_sigmoid(R8), 0.0), jnp.add, 0.0)
        C8 = jnp.where(low, pltpu.roll(R8, H, axis=0) - B8, 0.0)
        M8 = lane_scan(C8, jnp.maximum, -jnp.inf)
        T8 = jnp.where(low, B8, pltpu.roll(M8, H, axis=0))
        TX = jnp.concatenate([T8, zpad], axis=0).T
        valid = lane_x < H
        bX = jnp.where(valid, TX, 0.0)
        cmX = jnp.where(valid, pltpu.roll(TX, V7X_LANES - H, axis=1), 0.0)
        igX = jnp.where(valid, pltpu.roll(G, V7X_LANES - H, axis=1), 0.0)
        mmaxX = jnp.maximum(cmX, m_row)
        bL = bX[L - 1:L, :]
        gkX = bL - bX + igX
        m_new = jnp.maximum(bL + m_row, jnp.max(gkX, axis=0, keepdims=True))
        return dict(C8=C8, mmaxX=mmaxX, m_new=m_new,
                    w_interX=jnp.exp(m_row - mmaxX),
                    enegX=jnp.exp(-(bX + mmaxX)),
                    decay=jnp.exp(bL + m_row - m_new),
                    wkX=jnp.exp(gkX - m_new))

    issue()
    gps = [gate_prep(0, m_row)]
    issue()
    d_mains = pool_window()
    gps.append(gate_prep(L, gps[0]["m_new"]))
    issue()
    pool_mix(d_mains)
    issue(2)

    for c in range(ts // L):
        r0 = c * L
        C8, mmaxX, w_interX, enegX, decay, wkX = (gps[c][n] for n in
                                                 ("C8", "mmaxX", "w_interX", "enegX", "decay", "wkX"))
        per_head = []
        rsX = None
        qnX = None
        for h in range(H):
            hc = h * DH
            q = conv_silu(hc, r0) * (DH ** -0.5)
            k = conv_silu(MLSTM_WIDTH + hc, r0)
            v_aug = jnp.concatenate([v_b[r0:r0 + L, hc:hc + DH], onehot[h]], axis=1)
            qb = q.astype(BF16)
            S = _dot_nt(qb, k.astype(BF16))
            qc = _dot(qb, cs[h].astype(BF16))
            issue()
            E = jnp.exp(jnp.where(causal, C8[h:h + 1, :] - mmaxX[:, h:h + 1], -jnp.inf))
            av = _dot((S * E).astype(BF16), v_aug)
            rsX = av[:, DH:] if rsX is None else rsX + av[:, DH:]
            qnX = qc[:, DH:] if qnX is None else qnX + qc[:, DH:]
            per_head.append(av[:, :DH] + w_interX[:, h:h + 1] * qc[:, :DH])
            kw = wkX[:, h:h + 1] * k
            cs[h] = decay[:, h:h + 1] * cs[h] + _dot_tn(kw.astype(BF16), v_aug)
            if c > 0 and h < 2:
                issue()

        denX = rsX + w_interX * qnX
        rX = 1.0 / jnp.maximum(jnp.abs(denX), enegX)

        for h in range(H):
            hc = h * DH
            hout = per_head[h] * rX[:, h:h + 1]
            hn = hout * lax.rsqrt(jnp.mean(hout * hout, axis=1, keepdims=True) + EPS)
            og = _sigmoid(og_b[r0:r0 + L, hc:hc + DH])
            mout = hn * gnorm_ref[:, hc:hc + DH] * og
            mix_b[r0:r0 + L, POOL_WIDTH + hc:POOL_WIDTH + hc + DH] = mout.astype(BF16)
    m_row = gps[-1]["m_new"]
    assert not pieces, "every projection piece must have an issue point"
    if "proj" in stages:
        history_rows()

    for h in range(H):
        c_ref[h] = cs[h]
    m_ref[0:1, :] = m_row


def _mixer(l, x2, gpre, win, wgate, bias, conv, pmat, wpool, pscale, gnorm, wout, gpost, seq):
    T = x2.shape[0]
    ts = TS_MIX
    nt = seq // ts
    ntot = T // ts
    return pl.pallas_call(
        functools.partial(_mixer_kernel, nt=nt, ntot=ntot),
        grid=(ntot + 2,),
        in_specs=[
            pl.BlockSpec((ts, D_MODEL), lambda s: (jnp.minimum(s, ntot - 1), 0)),
            pl.BlockSpec((ts, D_MODEL), lambda s: (jnp.maximum(s - 2, 0), 0)),
            _vector_spec(D_MODEL, l, G_MIX_PRE),
            _layer_spec((D_MODEL, OFF_GATE), l),
            _layer_spec((D_MODEL, V7X_LANES), l),
            _layer_spec((1, V7X_LANES), l),
            _layer_spec((CONV_WIDTH, 2 * MLSTM_WIDTH), l),
            _const_spec((POOL_GROUPS, POOL_BLOCK, POOL_BLOCK)),
            _layer_spec((POOL_GROUPS, POOL_GROUP_DIM, POOL_GROUP_DIM), l),
            _vector_spec(POOL_WIDTH, l, V_POOL_SCALE),
            _vector_spec(MLSTM_WIDTH, l, V_MLSTM_NORM),
            _layer_spec((D_MODEL, D_MODEL), l),
            _vector_spec(D_MODEL, l, G_MIX_POST),
        ],
        out_specs=pl.BlockSpec((ts, D_MODEL), lambda s: (jnp.maximum(s - 2, 0), 0)),
        out_shape=jax.ShapeDtypeStruct((T, D_MODEL), F32),
        scratch_shapes=[
            *[pltpu.VMEM((ts + POOL_HALO, POOL_WIDTH), F32)] * 2,
            *[pltpu.VMEM((ts + CONV_HALO, 2 * MLSTM_WIDTH), F32)] * 2,
            *[pltpu.VMEM((ts, MLSTM_WIDTH), BF16)] * 2,
            *[pltpu.VMEM((ts, MLSTM_WIDTH), F32)] * 2,
            *[pltpu.VMEM((ts, V7X_LANES), F32)] * 2,
            *[pltpu.VMEM((ts, D_MODEL), BF16)] * 2,
            pltpu.VMEM((MLSTM_HEADS, MLSTM_HEAD_DIM, 2 * MLSTM_HEAD_DIM), F32),
            pltpu.VMEM((V7X_SUBLANES, V7X_LANES), F32),
        ],
        compiler_params=pltpu.CompilerParams(
            dimension_semantics=("arbitrary",), vmem_limit_bytes=V7X_VMEM_LIMIT_BYTES),
        name="mixer",
    )(x2, x2, gpre, win, wgate, bias, conv, pmat, wpool, pscale, gnorm, wout, gpost)


def _memkv_kernel(mem_ref, g_ref, wk_ref, wv_ref, k_ref, v_ref):
    mb = _rmsnorm(mem_ref[0], g_ref[...]).astype(BF16)
    k_ref[0] = _dot_w(mb, wk_ref[...]).astype(BF16)
    v_ref[0] = _dot_w(mb, wv_ref[...]).astype(BF16)


def _memkv(mem, g, wk, wv):
    depth = wk.shape[0]
    B = mem.shape[0]
    weight = pl.BlockSpec((None, D_MODEL, D_MODEL), lambda l, b: (l, 0, 0))
    out = pl.BlockSpec((None, 1, N_MEM, D_MODEL), lambda l, b: (l, b, 0, 0))
    return pl.pallas_call(
        _memkv_kernel,
        grid=(depth, B),
        in_specs=[pl.BlockSpec((1, N_MEM, D_MODEL), lambda l, b: (b, 0, 0)),
                  pl.BlockSpec((None, None, 1, D_MODEL), lambda l, b: (l, G_MEM, 0, 0)),
                  weight, weight],
        out_specs=[out, out],
        out_shape=[jax.ShapeDtypeStruct((depth, B, N_MEM, D_MODEL), BF16)] * 2,
        compiler_params=pltpu.CompilerParams(
            dimension_semantics=("arbitrary", "arbitrary"), vmem_limit_bytes=V7X_VMEM_LIMIT_BYTES),
        name="memkv",
    )(mem, g, wk, wv)


def _xattn_kernel(x_ref, k_ref, v_ref, gpre_ref, wq_ref, wo_ref, gpost_ref, o_ref, att_ref):
    halves = [slice(r0, r0 + SUB_ROWS) for r0 in range(0, x_ref.shape[0], SUB_ROWS)]
    qbs = []
    for rows in halves:
        hb = _rmsnorm(x_ref[rows, :], gpre_ref[...]).astype(BF16)
        qbs.append((_dot(hb, wq_ref[...]) * (XATTN_HEAD_DIM ** -0.5)).astype(BF16))
    for h in range(XATTN_HEADS):
        cols = slice(h * XATTN_HEAD_DIM, (h + 1) * XATTN_HEAD_DIM)
        scores = [_dot_nt(qb[:, cols], k_ref[0, :, cols]) for qb in qbs]
        for rows, s in zip(halves, scores):
            p = jnp.exp(s - jnp.max(s, axis=1, keepdims=True))
            o = _dot(p.astype(BF16), v_ref[0, :, cols]) / jnp.sum(p, axis=1, keepdims=True)
            att_ref[rows, cols] = o.astype(BF16)
    for rows in halves:
        y = _dot(att_ref[rows, :], wo_ref[...])
        o_ref[rows, :] = x_ref[rows, :] + _rmsnorm(y, gpost_ref[...])


def _xattn(l, x2, k, v, gpre, wq, wo, gpost, batch, seq):
    T = x2.shape[0]
    ts = TS_ATT
    nt = seq // ts
    tok = lambda b, t: (b * nt + t, 0)
    kvb = pl.BlockSpec((None, 1, N_MEM, D_MODEL), lambda b, t: (l, b, 0, 0))
    return pl.pallas_call(
        _xattn_kernel,
        grid=(batch, nt),
        in_specs=[
            pl.BlockSpec((ts, D_MODEL), tok), kvb, kvb,
            _vector_spec(D_MODEL, l, G_XATTN_PRE),
            _layer_spec((D_MODEL, D_MODEL), l),
            _layer_spec((D_MODEL, D_MODEL), l),
            _vector_spec(D_MODEL, l, G_XATTN_POST),
        ],
        out_specs=pl.BlockSpec((ts, D_MODEL), tok),
        out_shape=jax.ShapeDtypeStruct((T, D_MODEL), F32),
        scratch_shapes=[pltpu.VMEM((ts, D_MODEL), BF16)],
        compiler_params=pltpu.CompilerParams(
            dimension_semantics=("arbitrary", "arbitrary"), vmem_limit_bytes=V7X_VMEM_LIMIT_BYTES),
        name="xattn",
    )(x2, k, v, gpre, wq, wo, gpost)


def _ffn_kernel(x_ref, gpre_ref, wg_ref, wu_ref, wd_ref, gpost_ref, o_ref):
    tf = TF_FFN
    halves = [slice(r0, r0 + SUB_ROWS_FFN) for r0 in range(0, x_ref.shape[0], SUB_ROWS_FFN)]
    hbs = [_rmsnorm(x_ref[rows, :], gpre_ref[...]).astype(BF16) for rows in halves]
    accs = [None] * len(halves)

    def down(i, a, cols):
        contrib = _dot(a, wd_ref[cols, :])
        accs[i] = contrib if accs[i] is None else accs[i] + contrib

    pending = None
    for c in range(wg_ref.shape[1] // tf):
        cols = slice(c * tf, (c + 1) * tf)
        for i, hb in enumerate(hbs):
            gate = _dot(hb, wg_ref[:, cols])
            a = (gate * _sigmoid(gate) * _dot(hb, wu_ref[:, cols])).astype(BF16)
            if pending is not None:
                down(*pending)
            pending = (i, a, cols)
    down(*pending)
    for rows, acc in zip(halves, accs):
        o_ref[rows, :] = x_ref[rows, :] + _rmsnorm(acc, gpost_ref[...])


def _ffn(l, x2, gpre, wg, wu, wd, gpost):
    T = x2.shape[0]
    tm = TM_FFN
    d_ff = wg.shape[-1]
    return pl.pallas_call(
        _ffn_kernel,
        grid=(T // tm,),
        in_specs=[
            pl.BlockSpec((tm, D_MODEL), lambda i: (i, 0)),
            _vector_spec(D_MODEL, l, G_FFN_PRE),
            _layer_spec((D_MODEL, d_ff), l),
            _layer_spec((D_MODEL, d_ff), l),
            _layer_spec((d_ff, D_MODEL), l),
            _vector_spec(D_MODEL, l, G_FFN_POST),
        ],
        out_specs=pl.BlockSpec((tm, D_MODEL), lambda i: (i, 0)),
        out_shape=jax.ShapeDtypeStruct((T, D_MODEL), F32),
        compiler_params=pltpu.CompilerParams(
            dimension_semantics=("arbitrary",), vmem_limit_bytes=V7X_VMEM_LIMIT_BYTES),
        name="ffn",
    )(x2, gpre, wg, wu, wd, gpost)


def kernel(x, mem, w_in, b_gate, conv_qk, w_pool, pool_scale, mlstm_norm_g, w_out, g_mix_pre, g_mix_post, g_mem, g_xattn_pre, g_xattn_post, wq_x, wk_x, wv_x, wo_x, g_ffn_pre, g_ffn_post, w_gate, w_up, w_down):
    B, S, D = x.shape
    depth = w_in.shape[0]
    H = MLSTM_HEADS
    x2 = x.reshape(B * S, D)
    pmat = _pool_matrices()
    rows = lambda v: v.reshape(depth, 1, -1).astype(F32)
    table = lambda vs: jnp.stack([v.astype(F32) for v in vs], axis=1)[:, :, None, :]
    gains = table([g_mix_pre, g_mix_post, g_mem, g_xattn_pre, g_xattn_post, g_ffn_pre, g_ffn_post])
    mixvecs = table([pool_scale, mlstm_norm_g])
    bf = lambda w: w.astype(BF16)
    w_in_b = bf(w_in)
    wgate = jnp.concatenate(
        [w_in_b[:, :, OFF_GATE + H:], w_in_b[:, :, OFF_GATE:OFF_GATE + H],
         jnp.zeros((depth, D, V7X_LANES - 2 * H), BF16)], axis=2)
    bias = jnp.concatenate(
        [b_gate[:, H:], b_gate[:, :H], jnp.zeros((depth, V7X_LANES - 2 * H), b_gate.dtype)], axis=1)
    mixer_params = (gains, w_in_b, wgate, rows(bias), conv_qk.astype(F32), pmat, bf(w_pool),
                    mixvecs, mixvecs, bf(w_out), gains)
    xattn_params = (gains, bf(wq_x), bf(wo_x), gains)
    ffn_params = (gains, bf(w_gate), bf(w_up), bf(w_down), gains)
    k, v = _memkv(mem, gains, wk_x, wv_x)

    for l in range(depth):
        x2 = _mixer(l, x2, *mixer_params, S)
        x2 = _xattn(l, x2, k, v, *xattn_params, B, S)
        x2 = _ffn(l, x2, *ffn_params)

    return x2.reshape(B, S, D)
```

```python
import functools

import numpy as np
import jax
import jax.numpy as jnp
from jax import lax
from jax.experimental import pallas as pl
from jax.experimental.pallas import tpu as pltpu

F32 = jnp.float32
BF16 = jnp.bfloat16

EPS = 1e-6
D_MODEL = 1024
N_MEM = 256
POOL_WIDTH = 512
POOL_GROUPS = 4
POOL_GROUP_DIM = 128
POOL_WINDOWS = (2, 4, 8, 16)
MLSTM_WIDTH = 512
MLSTM_HEADS = 4
MLSTM_HEAD_DIM = 128
CONV_WIDTH = 4
XATTN_HEADS = 4
XATTN_HEAD_DIM = 256
OFF_Q = POOL_WIDTH
OFF_K = OFF_Q + MLSTM_WIDTH
OFF_V = OFF_K + MLSTM_WIDTH
OFF_O = OFF_V + MLSTM_WIDTH
OFF_GATE = OFF_O + MLSTM_WIDTH

V7X_LANES = 128
V7X_SUBLANES = 8
V7X_VMEM_LIMIT_BYTES = 56 * 1024 * 1024

G_MIX_PRE, G_MIX_POST, G_MEM, G_XATTN_PRE, G_XATTN_POST, G_FFN_PRE, G_FFN_POST = range(7)
V_POOL_SCALE, V_MLSTM_NORM = range(2)

IN_COLS_PAD = OFF_GATE + V7X_LANES
POOL_HALO = 16
CONV_HALO = 8
POOL_BLOCK = 256
PIECE_COLS = 256

TS_MIX = 512
L_CHUNK = 256
TS_ATT = 2048
TM_FFN = 1024
SUB_ROWS = 512
SUB_ROWS_FFN = 256
TF_FFN = 256


def _rmsnorm(x, g):
    return x * lax.rsqrt(jnp.mean(x * x, axis=-1, keepdims=True) + EPS) * g


def _sigmoid(x):
    return 1.0 / (1.0 + jnp.exp(-x))


def _log_sigmoid(x):
    return jnp.minimum(x, 0.0) - jnp.log(1.0 + jnp.exp(-jnp.abs(x)))


def _dot(a, b):
    return jnp.dot(a, b, preferred_element_type=F32)


def _dot_w(a, w):
    return lax.dot_general(a, w, (((1,), (0,)), ((), ())), preferred_element_type=F32)


def _dot_nt(a, b):
    return lax.dot_general(a, b, (((1,), (1,)), ((), ())), preferred_element_type=F32)


def _dot_tn(a, b):
    return lax.dot_general(a, b, (((0,), (0,)), ((), ())), preferred_element_type=F32)


def _const_spec(shape):
    nd = len(shape)
    return pl.BlockSpec(shape, lambda *_: (0,) * nd, pipeline_mode=pl.Buffered(1))


def _vector_spec(n, l, k):
    return pl.BlockSpec((None, None, 1, n), lambda *_: (l, k, 0, 0), pipeline_mode=pl.Buffered(1))


def _layer_spec(shape, l):
    nd = len(shape)
    return pl.BlockSpec((None,) + tuple(shape), lambda *_: (l,) + (0,) * nd, pipeline_mode=pl.Buffered(1))


def _pool_matrices():
    i = np.arange(POOL_BLOCK)[:, None]
    j = np.arange(POOL_BLOCK)[None, :]
    mats = []
    for win in POOL_WINDOWS:
        inside = (j <= i) & (j > i - win)
        mats.append(inside.astype(np.float32) / win - (i == j).astype(np.float32))
    return jnp.asarray(np.stack(mats), dtype=BF16)


def _mixer_kernel(*refs, nt, ntot):
    s = pl.program_id(0)

    def variant(step, stages):
        return functools.partial(_mixer_step, step % 2, 1 - step % 2, *refs, nt=nt, stages=stages)

    steady = jnp.logical_and(s >= 1, s <= ntot)
    pl.when(s == 0)(variant(0, ("proj",)))
    pl.when(jnp.logical_and(steady, lax.rem(s, 2) == 0))(variant(0, ("proj", "mix", "out")))
    pl.when(jnp.logical_and(steady, lax.rem(s, 2) == 1))(variant(1, ("proj", "mix", "out")))
    pl.when(s == ntot + 1)(variant(ntot + 1, ("out",)))


def _mixer_step(a, b, xn_ref, xb_ref, gpre_ref, win_ref, wgate_ref, bias_ref, conv_ref, pmat_ref, wpool_ref,
                pscale_ref, gnorm_ref, wout_ref, gpost_ref, o_ref,
                u0_ref, u1_ref, qk0_ref, qk1_ref, v0_ref, v1_ref, og0_ref, og1_ref,
                g0_ref, g1_ref, mix0_ref, mix1_ref, c_ref, m_ref, *, nt, stages):
    uext_a, uext_b = (u0_ref, u1_ref)[a], (u0_ref, u1_ref)[b]
    qkext_a, qkext_b = (qk0_ref, qk1_ref)[a], (qk0_ref, qk1_ref)[b]
    v_a, v_b = (v0_ref, v1_ref)[a], (v0_ref, v1_ref)[b]
    og_a, og_b = (og0_ref, og1_ref)[a], (og0_ref, og1_ref)[b]
    g_a, g_b = (g0_ref, g1_ref)[a], (g0_ref, g1_ref)[b]
    mix_a, mix_b = (mix0_ref, mix1_ref)[a], (mix0_ref, mix1_ref)[b]
    ts = xn_ref.shape[0]
    L = L_CHUNK
    H = MLSTM_HEADS
    DH = MLSTM_HEAD_DIM
    PC = PIECE_COLS
    s = pl.program_id(0)
    first_a = lax.rem(s, nt) == 0
    tb = lax.rem(s + nt - 1, nt)
    first_b = tb == 0

    hb = _rmsnorm(xn_ref[...], gpre_ref[...]).astype(BF16) if "proj" in stages else None

    out_parts = []

    def out_piece(j):
        def run():
            out_parts.append(_dot(mix_a[...], wout_ref[:, j * PC:(j + 1) * PC]))
            if len(out_parts) == D_MODEL // PC:
                ss = sum(jnp.sum(y * y, axis=1, keepdims=True) for y in out_parts)
                rs = lax.rsqrt(ss * (1.0 / D_MODEL) + EPS)
                for jj, y in enumerate(out_parts):
                    cols = slice(jj * PC, (jj + 1) * PC)
                    o_ref[:, cols] = xb_ref[:, cols] + y * rs * gpost_ref[:, cols]
        return run

    def proj_piece(c0):
        def run():
            w = win_ref[:, c0:c0 + PC] if c0 < OFF_GATE else wgate_ref[...]
            y = _dot(hb, w)
            if c0 < OFF_Q:
                uext_a[POOL_HALO:POOL_HALO + ts, c0:c0 + PC] = y
            elif c0 < OFF_V:
                qkext_a[CONV_HALO:CONV_HALO + ts, c0 - OFF_Q:c0 - OFF_Q + PC] = y
            elif c0 < OFF_O:
                v_a[:, c0 - OFF_V:c0 - OFF_V + PC] = y.astype(BF16)
            elif c0 < OFF_GATE:
                og_a[:, c0 - OFF_O:c0 - OFF_O + PC] = y
            else:
                g_a[...] = y + bias_ref[...]
        return run

    pieces = []
    if "out" in stages:
        pieces += [out_piece(j) for j in range(D_MODEL // PC)]
    if "proj" in stages:
        pieces += [proj_piece(c0) for c0 in range(0, IN_COLS_PAD, PC)]
    pieces.reverse()

    def issue(n=1):
        for _ in range(n):
            if pieces:
                pieces.pop()()

    def history_rows():
        if "mix" in stages:
            uext_a[0:POOL_HALO, :] = jnp.where(first_a, 0.0, uext_b[ts:ts + POOL_HALO, :])
            qkext_a[0:CONV_HALO, :] = jnp.where(first_a, 0.0, qkext_b[ts:ts + CONV_HALO, :])
        else:
            uext_a[0:POOL_HALO, :] = jnp.zeros((POOL_HALO, POOL_WIDTH), F32)
            qkext_a[0:CONV_HALO, :] = jnp.zeros((CONV_HALO, 2 * MLSTM_WIDTH), F32)

    if "mix" not in stages:
        issue(len(pieces))
        if "proj" in stages:
            history_rows()
            for ref in (mix_b, c_ref, m_ref):
                ref[...] = jnp.zeros_like(ref)
        return

    def pool_window():
        d_mains = []
        for blk in range(ts // POOL_BLOCK):
            r0 = blk * POOL_BLOCK
            for g in range(POOL_GROUPS):
                cols = slice(g * POOL_GROUP_DIM, (g + 1) * POOL_GROUP_DIM)
                ublk = uext_b[POOL_HALO + r0:POOL_HALO + r0 + POOL_BLOCK, cols]
                d_mains.append(_dot(pmat_ref[g], ublk.astype(BF16)))
        return d_mains

    def pool_mix(d_mains):
        for blk in range(ts // POOL_BLOCK):
            r0 = blk * POOL_BLOCK
            pos = tb * ts + r0 + lax.broadcasted_iota(jnp.int32, (POOL_HALO, 1), 0)
            for g, win in enumerate(POOL_WINDOWS):
                cols = slice(g * POOL_GROUP_DIM, (g + 1) * POOL_GROUP_DIM)
                cur = uext_b[POOL_HALO + r0:POOL_HALO + r0 + POOL_HALO, cols]
                acc = cur
                for k in range(1, win):
                    acc = acc + uext_b[POOL_HALO + r0 - k:POOL_HALO + r0 - k + POOL_HALO, cols]
                count = jnp.minimum(pos + 1, win).astype(F32)
                d_head = acc / count - cur
                d = jnp.concatenate([d_head, d_mains[blk * POOL_GROUPS + g][POOL_HALO:, :]], axis=0)
                y = _dot(d.astype(BF16), wpool_ref[g]) * pscale_ref[:, cols]
                mix_b[r0:r0 + POOL_BLOCK, cols] = y.astype(BF16)

    ri = lax.broadcasted_iota(jnp.int32, (L, L), 0)
    ci = lax.broadcasted_iota(jnp.int32, (L, L), 1)
    causal = ci <= ri
    lane_x = lax.broadcasted_iota(jnp.int32, (L, V7X_LANES), 1)
    sub8 = lax.broadcasted_iota(jnp.int32, (V7X_SUBLANES, L), 0)
    lane8 = lax.broadcasted_iota(jnp.int32, (V7X_SUBLANES, L), 1)
    zpad = jnp.zeros((V7X_LANES - V7X_SUBLANES, L), F32)
    onehot = [(lane_x == h).astype(BF16) for h in range(H)]
    cw = conv_ref[...]

    cs = [jnp.where(first_b, 0.0, c_ref[h]) for h in range(H)]
    m_row = jnp.where(first_b, 0.0, m_ref[0:1, :])

    def conv_silu(col0, r0):
        acc = None
        for j in range(CONV_WIDTH):
            off = CONV_HALO + r0 - (CONV_WIDTH - 1) + j
            term = cw[j:j + 1, col0:col0 + DH] * qkext_b[off:off + L, col0:col0 + DH]
            acc = term if acc is None else acc + term
        return acc * _sigmoid(acc)

    def lane_scan(x, op, fill):
        sh = 1
        while sh < L:
            x = op(x, jnp.where(lane8 >= sh, pltpu.roll(x, sh, axis=1), fill))
            sh *= 2
        return x

    def gate_prep(r0, m_row):
        G = g_b[r0:r0 + L, :]
        R8 = G.T[0:V7X_SUBLANES, :]
        low = sub8 < H
        B8 = lane_scan(jnp.where(low, _log_sigmoid(R8), 0.0), jnp.add, 0.0)
        C8 = jnp.where(low, pltpu.roll(R8, H, axis=0) - B8, 0.0)
        M8 = lane_scan(C8, jnp.maximum, -jnp.inf)
        T8 = jnp.where(low, B8, pltpu.roll(M8, H, axis=0))
        TX = jnp.concatenate([T8, zpad], axis=0).T
        valid = lane_x < H
        bX = jnp.where(valid, TX, 0.0)
        cmX = jnp.where(valid, pltpu.roll(TX, V7X_LANES - H, axis=1), 0.0)
        igX = jnp.where(valid, pltpu.roll(G, V7X_LANES - H, axis=1), 0.0)
        mmaxX = jnp.maximum(cmX, m_row)
        bL = bX[L - 1:L, :]
        gkX = bL - bX + igX
        m_new = jnp.maximum(bL + m_row, jnp.max(gkX, axis=0, keepdims=True))
        return dict(C8=C8, mmaxX=mmaxX, m_new=m_new,
                    w_interX=jnp.exp(m_row - mmaxX),
                    enegX=jnp.exp(-(bX + mmaxX)),
                    decay=jnp.exp(bL + m_row - m_new),
                    wkX=jnp.exp(gkX - m_new))

    issue()
    gps = [gate_prep(0, m_row)]
    issue()
    d_mains = pool_window()
    gps.append(gate_prep(L, gps[0]["m_new"]))
    issue()
    pool_mix(d_mains)
    issue(2)

    for c in range(ts // L):
        r0 = c * L
        C8, mmaxX, w_interX, enegX, decay, wkX = (gps[c][n] for n in
                                                 ("C8", "mmaxX", "w_interX", "enegX", "decay", "wkX"))
        per_head = []
        rsX = None
        qnX = None
        for h in range(H):
            hc = h * DH
            q = conv_silu(hc, r0) * (DH ** -0.5)
            k = conv_silu(MLSTM_WIDTH + hc, r0)
            v_aug = jnp.concatenate([v_b[r0:r0 + L, hc:hc + DH], onehot[h]], axis=1)
            qb = q.astype(BF16)
            S = _dot_nt(qb, k.astype(BF16))
            qc = _dot(qb, cs[h].astype(BF16))
            issue()
            E = jnp.exp(jnp.where(causal, C8[h:h + 1, :] - mmaxX[:, h:h + 1], -jnp.inf))
            av = _dot((S * E).astype(BF16), v_aug)
            rsX = av[:, DH:] if rsX is None else rsX + av[:, DH:]
            qnX = qc[:, DH:] if qnX is None else qnX + qc[:, DH:]
            per_head.append(av[:, :DH] + w_interX[:, h:h + 1] * qc[:, :DH])
            kw = wkX[:, h:h + 1] * k
            cs[h] = decay[:, h:h + 1] * cs[h] + _dot_tn(kw.astype(BF16), v_aug)
            if c > 0 and h in (0, 3):
                issue()

        denX = rsX + w_interX * qnX
        rX = 1.0 / jnp.maximum(jnp.abs(denX), enegX)

        for h in range(H):
            hc = h * DH
            hout = per_head[h] * rX[:, h:h + 1]
            hn = hout * lax.rsqrt(jnp.mean(hout * hout, axis=1, keepdims=True) + EPS)
            og = _sigmoid(og_b[r0:r0 + L, hc:hc + DH])
            mout = hn * gnorm_ref[:, hc:hc + DH] * og
            mix_b[r0:r0 + L, POOL_WIDTH + hc:POOL_WIDTH + hc + DH] = mout.astype(BF16)
    m_row = gps[-1]["m_new"]
    assert not pieces, "every projection piece must have an issue point"
    if "proj" in stages:
        history_rows()

    for h in range(H):
        c_ref[h] = cs[h]
    m_ref[0:1, :] = m_row


def _mixer(l, x2, gpre, win, wgate, bias, conv, pmat, wpool, pscale, gnorm, wout, gpost, seq):
    T = x2.shape[0]
    ts = TS_MIX
    nt = seq // ts
    ntot = T // ts
    return pl.pallas_call(
        functools.partial(_mixer_kernel, nt=nt, ntot=ntot),
        grid=(ntot + 2,),
        in_specs=[
            pl.BlockSpec((ts, D_MODEL), lambda s: (jnp.minimum(s, ntot - 1), 0)),
            pl.BlockSpec((ts, D_MODEL), lambda s: (jnp.maximum(s - 2, 0), 0)),
            _vector_spec(D_MODEL, l, G_MIX_PRE),
            _layer_spec((D_MODEL, OFF_GATE), l),
            _layer_spec((D_MODEL, V7X_LANES), l),
            _layer_spec((1, V7X_LANES), l),
            _layer_spec((CONV_WIDTH, 2 * MLSTM_WIDTH), l),
            _const_spec((POOL_GROUPS, POOL_BLOCK, POOL_BLOCK)),
            _layer_spec((POOL_GROUPS, POOL_GROUP_DIM, POOL_GROUP_DIM), l),
            _vector_spec(POOL_WIDTH, l, V_POOL_SCALE),
            _vector_spec(MLSTM_WIDTH, l, V_MLSTM_NORM),
            _layer_spec((D_MODEL, D_MODEL), l),
            _vector_spec(D_MODEL, l, G_MIX_POST),
        ],
        out_specs=pl.BlockSpec((ts, D_MODEL), lambda s: (jnp.maximum(s - 2, 0), 0)),
        out_shape=jax.ShapeDtypeStruct((T, D_MODEL), F32),
        scratch_shapes=[
            *[pltpu.VMEM((ts + POOL_HALO, POOL_WIDTH), F32)] * 2,
            *[pltpu.VMEM((ts + CONV_HALO, 2 * MLSTM_WIDTH), F32)] * 2,
            *[pltpu.VMEM((ts, MLSTM_WIDTH), BF16)] * 2,
            *[pltpu.VMEM((ts, MLSTM_WIDTH), F32)] * 2,
            *[pltpu.VMEM((ts, V7X_LANES), F32)] * 2,
            *[pltpu.VMEM((ts, D_MODEL), BF16)] * 2,
            pltpu.VMEM((MLSTM_HEADS, MLSTM_HEAD_DIM, 2 * MLSTM_HEAD_DIM), F32),
            pltpu.VMEM((V7X_SUBLANES, V7X_LANES), F32),
        ],
        compiler_params=pltpu.CompilerParams(
            dimension_semantics=("arbitrary",), vmem_limit_bytes=V7X_VMEM_LIMIT_BYTES),
        name="mixer",
    )(x2, x2, gpre, win, wgate, bias, conv, pmat, wpool, pscale, gnorm, wout, gpost)


def _memkv_kernel(mem_ref, g_ref, wk_ref, wv_ref, k_ref, v_ref):
    mb = _rmsnorm(mem_ref[0], g_ref[...]).astype(BF16)
    k_ref[0] = _dot_w(mb, wk_ref[...]).astype(BF16)
    v_ref[0] = _dot_w(mb, wv_ref[...]).astype(BF16)


def _memkv(mem, g, wk, wv):
    depth = wk.shape[0]
    B = mem.shape[0]
    weight = pl.BlockSpec((None, D_MODEL, D_MODEL), lambda l, b: (l, 0, 0))
    out = pl.BlockSpec((None, 1, N_MEM, D_MODEL), lambda l, b: (l, b, 0, 0))
    return pl.pallas_call(
        _memkv_kernel,
        grid=(depth, B),
        in_specs=[pl.BlockSpec((1, N_MEM, D_MODEL), lambda l, b: (b, 0, 0)),
                  pl.BlockSpec((None, None, 1, D_MODEL), lambda l, b: (l, G_MEM, 0, 0)),
                  weight, weight],
        out_specs=[out, out],
        out_shape=[jax.ShapeDtypeStruct((depth, B, N_MEM, D_MODEL), BF16)] * 2,
        compiler_params=pltpu.CompilerParams(
            dimension_semantics=("arbitrary", "arbitrary"), vmem_limit_bytes=V7X_VMEM_LIMIT_BYTES),
        name="memkv",
    )(mem, g, wk, wv)


def _xattn_kernel(x_ref, k_ref, v_ref, gpre_ref, wq_ref, wo_ref, gpost_ref, o_ref, att_ref):
    halves = [slice(r0, r0 + SUB_ROWS) for r0 in range(0, x_ref.shape[0], SUB_ROWS)]
    qbs = []
    for rows in halves:
        hb = _rmsnorm(x_ref[rows, :], gpre_ref[...]).astype(BF16)
        qbs.append((_dot(hb, wq_ref[...]) * (XATTN_HEAD_DIM ** -0.5)).astype(BF16))
    for h in range(XATTN_HEADS):
        cols = slice(h * XATTN_HEAD_DIM, (h + 1) * XATTN_HEAD_DIM)
        scores = [_dot_nt(qb[:, cols], k_ref[0, :, cols]) for qb in qbs]
        for rows, s in zip(halves, scores):
            p = jnp.exp(s - jnp.max(s, axis=1, keepdims=True))
            o = _dot(p.astype(BF16), v_ref[0, :, cols]) / jnp.sum(p, axis=1, keepdims=True)
            att_ref[rows, cols] = o.astype(BF16)
    for rows in halves:
        y = _dot(att_ref[rows, :], wo_ref[...])
        o_ref[rows, :] = x_ref[rows, :] + _rmsnorm(y, gpost_ref[...])


def _xattn(l, x2, k, v, gpre, wq, wo, gpost, batch, seq):
    T = x2.shape[0]
    ts = TS_ATT
    nt = seq // ts
    tok = lambda b, t: (b * nt + t, 0)
    kvb = pl.BlockSpec((None, 1, N_MEM, D_MODEL), lambda b, t: (l, b, 0, 0))
    return pl.pallas_call(
        _xattn_kernel,
        grid=(batch, nt),
        in_specs=[
            pl.BlockSpec((ts, D_MODEL), tok), kvb, kvb,
            _vector_spec(D_MODEL, l, G_XATTN_PRE),
            _layer_spec((D_MODEL, D_MODEL), l),
            _layer_spec((D_MODEL, D_MODEL), l),
            _vector_spec(D_MODEL, l, G_XATTN_POST),
        ],
        out_specs=pl.BlockSpec((ts, D_MODEL), tok),
        out_shape=jax.ShapeDtypeStruct((T, D_MODEL), F32),
        scratch_shapes=[pltpu.VMEM((ts, D_MODEL), BF16)],
        compiler_params=pltpu.CompilerParams(
            dimension_semantics=("arbitrary", "arbitrary"), vmem_limit_bytes=V7X_VMEM_LIMIT_BYTES),
        name="xattn",
    )(x2, k, v, gpre, wq, wo, gpost)


def _ffn_kernel(x_ref, gpre_ref, wg_ref, wu_ref, wd_ref, gpost_ref, o_ref):
    tf = TF_FFN
    halves = [slice(r0, r0 + SUB_ROWS_FFN) for r0 in range(0, x_ref.shape[0], SUB_ROWS_FFN)]
    hbs = [_rmsnorm(x_ref[rows, :], gpre_ref[...]).astype(BF16) for rows in halves]
    accs = [None] * len(halves)

    def down(i, a, cols):
        contrib = _dot(a, wd_ref[cols, :])
        accs[i] = contrib if accs[i] is None else accs[i] + contrib

    pending = None
    for c in range(wg_ref.shape[1] // tf):
        cols = slice(c * tf, (c + 1) * tf)
        for i, hb in enumerate(hbs):
            gate = _dot(hb, wg_ref[:, cols])
            a = (gate * _sigmoid(gate) * _dot(hb, wu_ref[:, cols])).astype(BF16)
            if pending is not None:
                down(*pending)
            pending = (i, a, cols)
    down(*pending)
    for rows, acc in zip(halves, accs):
        o_ref[rows, :] = x_ref[rows, :] + _rmsnorm(acc, gpost_ref[...])


def _ffn(l, x2, gpre, wg, wu, wd, gpost):
    T = x2.shape[0]
    tm = TM_FFN
    d_ff = wg.shape[-1]
    return pl.pallas_call(
        _ffn_kernel,
        grid=(T // tm,),
        in_specs=[
            pl.BlockSpec((tm, D_MODEL), lambda i: (i, 0)),
            _vector_spec(D_MODEL, l, G_FFN_PRE),
            _layer_spec((D_MODEL, d_ff), l),
            _layer_spec((D_MODEL, d_ff), l),
            _layer_spec((d_ff, D_MODEL), l),
            _vector_spec(D_MODEL, l, G_FFN_POST),
        ],
        out_specs=pl.BlockSpec((tm, D_MODEL), lambda i: (i, 0)),
        out_shape=jax.ShapeDtypeStruct((T, D_MODEL), F32),
        compiler_params=pltpu.CompilerParams(
            dimension_semantics=("arbitrary",), vmem_limit_bytes=V7X_VMEM_LIMIT_BYTES),
        name="ffn",
    )(x2, gpre, wg, wu, wd, gpost)


def kernel(x, mem, w_in, b_gate, conv_qk, w_pool, pool_scale, mlstm_norm_g, w_out, g_mix_pre, g_mix_post, g_mem, g_xattn_pre, g_xattn_post, wq_x, wk_x, wv_x, wo_x, g_ffn_pre, g_ffn_post, w_gate, w_up, w_down):
    B, S, D = x.shape
    depth = w_in.shape[0]
    H = MLSTM_HEADS
    x2 = x.reshape(B * S, D)
    pmat = _pool_matrices()
    rows = lambda v: v.reshape(depth, 1, -1).astype(F32)
    table = lambda vs: jnp.stack([v.astype(F32) for v in vs], axis=1)[:, :, None, :]
    gains = table([g_mix_pre, g_mix_post, g_mem, g_xattn_pre, g_xattn_post, g_ffn_pre, g_ffn_post])
    mixvecs = table([pool_scale, mlstm_norm_g])
    bf = lambda w: w.astype(BF16)
    w_in_b = bf(w_in)
    wgate = jnp.concatenate(
        [w_in_b[:, :, OFF_GATE + H:], w_in_b[:, :, OFF_GATE:OFF_GATE + H],
         jnp.zeros((depth, D, V7X_LANES - 2 * H), BF16)], axis=2)
    bias = jnp.concatenate(
        [b_gate[:, H:], b_gate[:, :H], jnp.zeros((depth, V7X_LANES - 2 * H), b_gate.dtype)], axis=1)
    mixer_params = (gains, w_in_b, wgate, rows(bias), conv_qk.astype(F32), pmat, bf(w_pool),
                    mixvecs, mixvecs, bf(w_out), gains)
    xattn_params = (gains, bf(wq_x), bf(wo_x), gains)
    ffn_params = (gains, bf(w_gate), bf(w_up), bf(w_down), gains)
    k, v = _memkv(mem, gains, wk_x, wv_x)

    for l in range(depth):
        x2 = _mixer(l, x2, *mixer_params, S)
        x2 = _xattn(l, x2, k, v, *xattn_params, B, S)
        x2 = _ffn(l, x2, *ffn_params)

    return x2.reshape(B, S, D)
```

```python
import functools

import numpy as np
import jax
import jax.numpy as jnp
from jax import lax
from jax.experimental import pallas as pl
from jax.experimental.pallas import tpu as pltpu

F32 = jnp.float32
BF16 = jnp.bfloat16

EPS = 1e-6
D_MODEL = 1024
N_MEM = 256
POOL_WIDTH = 512
POOL_GROUPS = 4
POOL_GROUP_DIM = 128
POOL_WINDOWS = (2, 4, 8, 16)
MLSTM_WIDTH = 512
MLSTM_HEADS = 4
MLSTM_HEAD_DIM = 128
CONV_WIDTH = 4
XATTN_HEADS = 4
XATTN_HEAD_DIM = 256
OFF_Q = POOL_WIDTH
OFF_K = OFF_Q + MLSTM_WIDTH
OFF_V = OFF_K + MLSTM_WIDTH
OFF_O = OFF_V + MLSTM_WIDTH
OFF_GATE = OFF_O + MLSTM_WIDTH

V7X_LANES = 128
V7X_SUBLANES = 8
V7X_VMEM_LIMIT_BYTES = 56 * 1024 * 1024

G_MIX_PRE, G_MIX_POST, G_MEM, G_XATTN_PRE, G_XATTN_POST, G_FFN_PRE, G_FFN_POST = range(7)
V_POOL_SCALE, V_MLSTM_NORM = range(2)

IN_COLS_PAD = OFF_GATE + V7X_LANES
POOL_HALO = 16
CONV_HALO = 8
POOL_BLOCK = 256
PIECE_COLS = 256

TS_MIX = 512
L_CHUNK = 256
TS_ATT = 2048
TM_FFN = 1024
SUB_ROWS = 512
SUB_ROWS_FFN = 256
TF_FFN = 256


def _rmsnorm(x, g):
    return x * lax.rsqrt(jnp.mean(x * x, axis=-1, keepdims=True) + EPS) * g


def _sigmoid(x):
    return 1.0 / (1.0 + jnp.exp(-x))


def _log_sigmoid(x):
    return jnp.minimum(x, 0.0) - jnp.log(1.0 + jnp.exp(-jnp.abs(x)))


def _dot(a, b):
    return jnp.dot(a, b, preferred_element_type=F32)


def _dot_w(a, w):
    return lax.dot_general(a, w, (((1,), (0,)), ((), ())), preferred_element_type=F32)


def _dot_nt(a, b):
    return lax.dot_general(a, b, (((1,), (1,)), ((), ())), preferred_element_type=F32)


def _dot_tn(a, b):
    return lax.dot_general(a, b, (((0,), (0,)), ((), ())), preferred_element_type=F32)


def _const_spec(shape):
    nd = len(shape)
    return pl.BlockSpec(shape, lambda *_: (0,) * nd, pipeline_mode=pl.Buffered(1))


def _vector_spec(n, l, k):
    return pl.BlockSpec((None, None, 1, n), lambda *_: (l, k, 0, 0), pipeline_mode=pl.Buffered(1))


def _layer_spec(shape, l):
    nd = len(shape)
    return pl.BlockSpec((None,) + tuple(shape), lambda *_: (l,) + (0,) * nd, pipeline_mode=pl.Buffered(1))


def _pool_matrices():
    i = np.arange(POOL_BLOCK)[:, None]
    j = np.arange(POOL_BLOCK)[None, :]
    mats = []
    for win in POOL_WINDOWS:
        inside = (j <= i) & (j > i - win)
        mats.append(inside.astype(np.float32) / win - (i == j).astype(np.float32))
    return jnp.asarray(np.stack(mats), dtype=BF16)


def _mixer_kernel(*refs, nt, ntot):
    s = pl.program_id(0)

    def variant(step, stages):
        return functools.partial(_mixer_step, step % 2, 1 - step % 2, *refs, nt=nt, stages=stages)

    steady = jnp.logical_and(s >= 1, s <= ntot)
    pl.when(s == 0)(variant(0, ("proj",)))
    pl.when(jnp.logical_and(steady, lax.rem(s, 2) == 0))(variant(0, ("proj", "mix", "out")))
    pl.when(jnp.logical_and(steady, lax.rem(s, 2) == 1))(variant(1, ("proj", "mix", "out")))
    pl.when(s == ntot + 1)(variant(ntot + 1, ("out",)))


def _mixer_step(a, b, xn_ref, xb_ref, gpre_ref, win_ref, wgate_ref, bias_ref, conv_ref, pmat_ref, wpool_ref,
                pscale_ref, gnorm_ref, wout_ref, gpost_ref, o_ref,
                u0_ref, u1_ref, qk0_ref, qk1_ref, v0_ref, v1_ref, og0_ref, og1_ref,
                g0_ref, g1_ref, mix0_ref, mix1_ref, c_ref, m_ref, *, nt, stages):
    uext_a, uext_b = (u0_ref, u1_ref)[a], (u0_ref, u1_ref)[b]
    qkext_a, qkext_b = (qk0_ref, qk1_ref)[a], (qk0_ref, qk1_ref)[b]
    v_a, v_b = (v0_ref, v1_ref)[a], (v0_ref, v1_ref)[b]
    og_a, og_b = (og0_ref, og1_ref)[a], (og0_ref, og1_ref)[b]
    g_a, g_b = (g0_ref, g1_ref)[a], (g0_ref, g1_ref)[b]
    mix_a, mix_b = (mix0_ref, mix1_ref)[a], (mix0_ref, mix1_ref)[b]
    ts = xn_ref.shape[0]
    L = L_CHUNK
    H = MLSTM_HEADS
    DH = MLSTM_HEAD_DIM
    PC = PIECE_COLS
    s = pl.program_id(0)
    first_a = lax.rem(s, nt) == 0
    tb = lax.rem(s + nt - 1, nt)
    first_b = tb == 0

    hb = _rmsnorm(xn_ref[...], gpre_ref[...]).astype(BF16) if "proj" in stages else None

    out_parts = []

    def out_piece(j):
        def run():
            out_parts.append(_dot(mix_a[...], wout_ref[:, j * PC:(j + 1) * PC]))
            if len(out_parts) == D_MODEL // PC:
                ss = sum(jnp.sum(y * y, axis=1, keepdims=True) for y in out_parts)
                rs = lax.rsqrt(ss * (1.0 / D_MODEL) + EPS)
                for jj, y in enumerate(out_parts):
                    cols = slice(jj * PC, (jj + 1) * PC)
                    o_ref[:, cols] = xb_ref[:, cols] + y * rs * gpost_ref[:, cols]
        return run

    def proj_piece(c0):
        def run():
            w = win_ref[:, c0:c0 + PC] if c0 < OFF_GATE else wgate_ref[...]
            y = _dot(hb, w)
            if c0 < OFF_Q:
                uext_a[POOL_HALO:POOL_HALO + ts, c0:c0 + PC] = y
            elif c0 < OFF_V:
                qkext_a[CONV_HALO:CONV_HALO + ts, c0 - OFF_Q:c0 - OFF_Q + PC] = y
            elif c0 < OFF_O:
                v_a[:, c0 - OFF_V:c0 - OFF_V + PC] = y.astype(BF16)
            elif c0 < OFF_GATE:
                og_a[:, c0 - OFF_O:c0 - OFF_O + PC] = y
            else:
                g_a[...] = y + bias_ref[...]
        return run

    pieces = []
    if "out" in stages:
        pieces += [out_piece(j) for j in range(D_MODEL // PC)]
    if "proj" in stages:
        pieces += [proj_piece(c0) for c0 in range(0, IN_COLS_PAD, PC)]
    pieces.reverse()

    def issue(n=1):
        for _ in range(n):
            if pieces:
                pieces.pop()()

    def history_rows():
        if "mix" in stages:
            uext_a[0:POOL_HALO, :] = jnp.where(first_a, 0.0, uext_b[ts:ts + POOL_HALO, :])
            qkext_a[0:CONV_HALO, :] = jnp.where(first_a, 0.0, qkext_b[ts:ts + CONV_HALO, :])
        else:
            uext_a[0:POOL_HALO, :] = jnp.zeros((POOL_HALO, POOL_WIDTH), F32)
            qkext_a[0:CONV_HALO, :] = jnp.zeros((CONV_HALO, 2 * MLSTM_WIDTH), F32)

    if "mix" not in stages:
        issue(len(pieces))
        if "proj" in stages:
            history_rows()
            for ref in (mix_b, c_ref, m_ref):
                ref[...] = jnp.zeros_like(ref)
        return

    def pool_window():
        d_mains = []
        for blk in range(ts // POOL_BLOCK):
            r0 = blk * POOL_BLOCK
            for g in range(POOL_GROUPS):
                cols = slice(g * POOL_GROUP_DIM, (g + 1) * POOL_GROUP_DIM)
                ublk = uext_b[POOL_HALO + r0:POOL_HALO + r0 + POOL_BLOCK, cols]
                d_mains.append(_dot(pmat_ref[g], ublk.astype(BF16)))
        return d_mains

    def pool_mix(d_mains):
        for blk in range(ts // POOL_BLOCK):
            r0 = blk * POOL_BLOCK
            pos = tb * ts + r0 + lax.broadcasted_iota(jnp.int32, (POOL_HALO, 1), 0)
            for g, win in enumerate(POOL_WINDOWS):
                cols = slice(g * POOL_GROUP_DIM, (g + 1) * POOL_GROUP_DIM)
                cur = uext_b[POOL_HALO + r0:POOL_HALO + r0 + POOL_HALO, cols]
                acc = cur
                for k in range(1, win):
                    acc = acc + uext_b[POOL_HALO + r0 - k:POOL_HALO + r0 - k + POOL_HALO, cols]
                count = jnp.minimum(pos + 1, win).astype(F32)
                d_head = acc / count - cur
                d = jnp.concatenate([d_head, d_mains[blk * POOL_GROUPS + g][POOL_HALO:, :]], axis=0)
                y = _dot(d.astype(BF16), wpool_ref[g]) * pscale_ref[:, cols]
                mix_b[r0:r0 + POOL_BLOCK, cols] = y.astype(BF16)

    ri = lax.broadcasted_iota(jnp.int32, (L, L), 0)
    ci = lax.broadcasted_iota(jnp.int32, (L, L), 1)
    causal = ci <= ri
    lane_x = lax.broadcasted_iota(jnp.int32, (L, V7X_LANES), 1)
    sub8 = lax.broadcasted_iota(jnp.int32, (V7X_SUBLANES, L), 0)
    lane8 = lax.broadcasted_iota(jnp.int32, (V7X_SUBLANES, L), 1)
    zpad = jnp.zeros((V7X_LANES - V7X_SUBLANES, L), F32)
    onehot = [(lane_x == h).astype(BF16) for h in range(H)]
    cw = conv_ref[...]

    cs = [jnp.where(first_b, 0.0, c_ref[h]) for h in range(H)]
    m_row = jnp.where(first_b, 0.0, m_ref[0:1, :])

    def conv_silu(col0, r0):
        acc = None
        for j in range(CONV_WIDTH):
            off = CONV_HALO + r0 - (CONV_WIDTH - 1) + j
            term = cw[j:j + 1, col0:col0 + DH] * qkext_b[off:off + L, col0:col0 + DH]
            acc = term if acc is None else acc + term
        return acc * _sigmoid(acc)

    def lane_scan(x, op, fill):
        sh = 1
        while sh < L:
            x = op(x, jnp.where(lane8 >= sh, pltpu.roll(x, sh, axis=1), fill))
            sh *= 2
        return x

    def gate_prep(r0, m_row):
        G = g_b[r0:r0 + L, :]
        R8 = G.T[0:V7X_SUBLANES, :]
        low = sub8 < H
        B8 = lane_scan(jnp.where(low, _log_sigmoid(R8), 0.0), jnp.add, 0.0)
        C8 = jnp.where(low, pltpu.roll(R8, H, axis=0) - B8, 0.0)
        M8 = lane_scan(C8, jnp.maximum, -jnp.inf)
        T8 = jnp.where(low, B8, pltpu.roll(M8, H, axis=0))
        TX = jnp.concatenate([T8, zpad], axis=0).T
        valid = lane_x < H
        bX = jnp.where(valid, TX, 0.0)
        cmX = jnp.where(valid, pltpu.roll(TX, V7X_LANES - H, axis=1), 0.0)
        igX = jnp.where(valid, pltpu.roll(G, V7X_LANES - H, axis=1), 0.0)
        mmaxX = jnp.maximum(cmX, m_row)
        bL = bX[L - 1:L, :]
        gkX = bL - bX + igX
        m_new = jnp.maximum(bL + m_row, jnp.max(gkX, axis=0, keepdims=True))
        return dict(C8=C8, mmaxX=mmaxX, m_new=m_new,
                    w_interX=jnp.exp(m_row - mmaxX),
                    enegX=jnp.exp(-(bX + mmaxX)),
                    decay=jnp.exp(bL + m_row - m_new),
                    wkX=jnp.exp(gkX - m_new))

    issue()
    gps = [gate_prep(0, m_row)]
    issue()
    d_mains = pool_window()
    gps.append(gate_prep(L, gps[0]["m_new"]))
    issue()
    pool_mix(d_mains)
    issue(2)

    for c in range(ts // L):
        r0 = c * L
        C8, mmaxX, w_interX, enegX, decay, wkX = (gps[c][n] for n in
                                                 ("C8", "mmaxX", "w_interX", "enegX", "decay", "wkX"))
        per_head = []
        rsX = None
        qnX = None
        for h in range(H):
            hc = h * DH
            q = conv_silu(hc, r0) * (DH ** -0.5)
            k = conv_silu(MLSTM_WIDTH + hc, r0)
            v_aug = jnp.concatenate([v_b[r0:r0 + L, hc:hc + DH], onehot[h]], axis=1)
            qb = q.astype(BF16)
            S = _dot_nt(qb, k.astype(BF16))
            qc = _dot(qb, cs[h].astype(BF16))
            issue()
            kw = wkX[:, h:h + 1] * k
            cs[h] = decay[:, h:h + 1] * cs[h] + _dot_tn(kw.astype(BF16), v_aug)
            E = jnp.exp(jnp.where(causal, C8[h:h + 1, :] - mmaxX[:, h:h + 1], -jnp.inf))
            av = _dot((S * E).astype(BF16), v_aug)
            rsX = av[:, DH:] if rsX is None else rsX + av[:, DH:]
            qnX = qc[:, DH:] if qnX is None else qnX + qc[:, DH:]
            per_head.append(av[:, :DH] + w_interX[:, h:h + 1] * qc[:, :DH])
            if c > 0 and h % 2 == 0:
                issue()

        denX = rsX + w_interX * qnX
        rX = 1.0 / jnp.maximum(jnp.abs(denX), enegX)

        for h in range(H):
            hc = h * DH
            hout = per_head[h] * rX[:, h:h + 1]
            hn = hout * lax.rsqrt(jnp.mean(hout * hout, axis=1, keepdims=True) + EPS)
            og = _sigmoid(og_b[r0:r0 + L, hc:hc + DH])
            mout = hn * gnorm_ref[:, hc:hc + DH] * og
            mix_b[r0:r0 + L, POOL_WIDTH + hc:POOL_WIDTH + hc + DH] = mout.astype(BF16)
    m_row = gps[-1]["m_new"]
    assert not pieces, "every projection piece must have an issue point"
    if "proj" in stages:
        history_rows()

    for h in range(H):
        c_ref[h] = cs[h]
    m_ref[0:1, :] = m_row


def _mixer(l, x2, gpre, win, wgate, bias, conv, pmat, wpool, pscale, gnorm, wout, gpost, seq):
    T = x2.shape[0]
    ts = TS_MIX
    nt = seq // ts
    ntot = T // ts
    return pl.pallas_call(
        functools.partial(_mixer_kernel, nt=nt, ntot=ntot),
        grid=(ntot + 2,),
        in_specs=[
            pl.BlockSpec((ts, D_MODEL), lambda s: (jnp.minimum(s, ntot - 1), 0)),
            pl.BlockSpec((ts, D_MODEL), lambda s: (jnp.maximum(s - 2, 0), 0)),
            _vector_spec(D_MODEL, l, G_MIX_PRE),
            _layer_spec((D_MODEL, OFF_GATE), l),
            _layer_spec((D_MODEL, V7X_LANES), l),
            _layer_spec((1, V7X_LANES), l),
            _layer_spec((CONV_WIDTH, 2 * MLSTM_WIDTH), l),
            _const_spec((POOL_GROUPS, POOL_BLOCK, POOL_BLOCK)),
            _layer_spec((POOL_GROUPS, POOL_GROUP_DIM, POOL_GROUP_DIM), l),
            _vector_spec(POOL_WIDTH, l, V_POOL_SCALE),
            _vector_spec(MLSTM_WIDTH, l, V_MLSTM_NORM),
            _layer_spec((D_MODEL, D_MODEL), l),
            _vector_spec(D_MODEL, l, G_MIX_POST),
        ],
        out_specs=pl.BlockSpec((ts, D_MODEL), lambda s: (jnp.maximum(s - 2, 0), 0)),
        out_shape=jax.ShapeDtypeStruct((T, D_MODEL), F32),
        scratch_shapes=[
            *[pltpu.VMEM((ts + POOL_HALO, POOL_WIDTH), F32)] * 2,
            *[pltpu.VMEM((ts + CONV_HALO, 2 * MLSTM_WIDTH), F32)] * 2,
            *[pltpu.VMEM((ts, MLSTM_WIDTH), BF16)] * 2,
            *[pltpu.VMEM((ts, MLSTM_WIDTH), F32)] * 2,
            *[pltpu.VMEM((ts, V7X_LANES), F32)] * 2,
            *[pltpu.VMEM((ts, D_MODEL), BF16)] * 2,
            pltpu.VMEM((MLSTM_HEADS, MLSTM_HEAD_DIM, 2 * MLSTM_HEAD_DIM), F32),
            pltpu.VMEM((V7X_SUBLANES, V7X_LANES), F32),
        ],
        compiler_params=pltpu.CompilerParams(
            dimension_semantics=("arbitrary",), vmem_limit_bytes=V7X_VMEM_LIMIT_BYTES),
        name="mixer",
    )(x2, x2, gpre, win, wgate, bias, conv, pmat, wpool, pscale, gnorm, wout, gpost)


def _memkv_kernel(mem_ref, g_ref, wk_ref, wv_ref, k_ref, v_ref):
    mb = _rmsnorm(mem_ref[0], g_ref[...]).astype(BF16)
    k_ref[0] = _dot_w(mb, wk_ref[...]).astype(BF16)
    v_ref[0] = _dot_w(mb, wv_ref[...]).astype(BF16)


def _memkv(mem, g, wk, wv):
    depth = wk.shape[0]
    B = mem.shape[0]
    weight = pl.BlockSpec((None, D_MODEL, D_MODEL), lambda l, b: (l, 0, 0))
    out = pl.BlockSpec((None, 1, N_MEM, D_MODEL), lambda l, b: (l, b, 0, 0))
    return pl.pallas_call(
        _memkv_kernel,
        grid=(depth, B),
        in_specs=[pl.BlockSpec((1, N_MEM, D_MODEL), lambda l, b: (b, 0, 0)),
                  pl.BlockSpec((None, None, 1, D_MODEL), lambda l, b: (l, G_MEM, 0, 0)),
                  weight, weight],
        out_specs=[out, out],
        out_shape=[jax.ShapeDtypeStruct((depth, B, N_MEM, D_MODEL), BF16)] * 2,
        compiler_params=pltpu.CompilerParams(
            dimension_semantics=("arbitrary", "arbitrary"), vmem_limit_bytes=V7X_VMEM_LIMIT_BYTES),
        name="memkv",
    )(mem, g, wk, wv)


def _xattn_kernel(x_ref, k_ref, v_ref, gpre_ref, wq_ref, wo_ref, gpost_ref, o_ref, att_ref):
    halves = [slice(r0, r0 + SUB_ROWS) for r0 in range(0, x_ref.shape[0], SUB_ROWS)]
    qbs = []
    for rows in halves:
        hb = _rmsnorm(x_ref[rows, :], gpre_ref[...]).astype(BF16)
        qbs.append((_dot(hb, wq_ref[...]) * (XATTN_HEAD_DIM ** -0.5)).astype(BF16))
    for h in range(XATTN_HEADS):
        cols = slice(h * XATTN_HEAD_DIM, (h + 1) * XATTN_HEAD_DIM)
        scores = [_dot_nt(qb[:, cols], k_ref[0, :, cols]) for qb in qbs]
        for rows, s in zip(halves, scores):
            p = jnp.exp(s - jnp.max(s, axis=1, keepdims=True))
            o = _dot(p.astype(BF16), v_ref[0, :, cols]) / jnp.sum(p, axis=1, keepdims=True)
            att_ref[rows, cols] = o.astype(BF16)
    for rows in halves:
        y = _dot(att_ref[rows, :], wo_ref[...])
        o_ref[rows, :] = x_ref[rows, :] + _rmsnorm(y, gpost_ref[...])


def _xattn(l, x2, k, v, gpre, wq, wo, gpost, batch, seq):
    T = x2.shape[0]
    ts = TS_ATT
    nt = seq // ts
    tok = lambda b, t: (b * nt + t, 0)
    kvb = pl.BlockSpec((None, 1, N_MEM, D_MODEL), lambda b, t: (l, b, 0, 0))
    return pl.pallas_call(
        _xattn_kernel,
        grid=(batch, nt),
        in_specs=[
            pl.BlockSpec((ts, D_MODEL), tok), kvb, kvb,
            _vector_spec(D_MODEL, l, G_XATTN_PRE),
            _layer_spec((D_MODEL, D_MODEL), l),
            _layer_spec((D_MODEL, D_MODEL), l),
            _vector_spec(D_MODEL, l, G_XATTN_POST),
        ],
        out_specs=pl.BlockSpec((ts, D_MODEL), tok),
        out_shape=jax.ShapeDtypeStruct((T, D_MODEL), F32),
        scratch_shapes=[pltpu.VMEM((ts, D_MODEL), BF16)],
        compiler_params=pltpu.CompilerParams(
            dimension_semantics=("arbitrary", "arbitrary"), vmem_limit_bytes=V7X_VMEM_LIMIT_BYTES),
        name="xattn",
    )(x2, k, v, gpre, wq, wo, gpost)


def _ffn_kernel(x_ref, gpre_ref, wg_ref, wu_ref, wd_ref, gpost_ref, o_ref):
    tf = TF_FFN
    halves = [slice(r0, r0 + SUB_ROWS_FFN) for r0 in range(0, x_ref.shape[0], SUB_ROWS_FFN)]
    hbs = [_rmsnorm(x_ref[rows, :], gpre_ref[...]).astype(BF16) for rows in halves]
    accs = [None] * len(halves)

    def down(i, a, cols):
        contrib = _dot(a, wd_ref[cols, :])
        accs[i] = contrib if accs[i] is None else accs[i] + contrib

    pending = None
    for c in range(wg_ref.shape[1] // tf):
        cols = slice(c * tf, (c + 1) * tf)
        for i, hb in enumerate(hbs):
            gate = _dot(hb, wg_ref[:, cols])
            a = (gate * _sigmoid(gate) * _dot(hb, wu_ref[:, cols])).astype(BF16)
            if pending is not None:
                down(*pending)
            pending = (i, a, cols)
    down(*pending)
    for rows, acc in zip(halves, accs):
        o_ref[rows, :] = x_ref[rows, :] + _rmsnorm(acc, gpost_ref[...])


def _ffn(l, x2, gpre, wg, wu, wd, gpost):
    T = x2.shape[0]
    tm = TM_FFN
    d_ff = wg.shape[-1]
    return pl.pallas_call(
        _ffn_kernel,
        grid=(T // tm,),
        in_specs=[
            pl.BlockSpec((tm, D_MODEL), lambda i: (i, 0)),
            _vector_spec(D_MODEL, l, G_FFN_PRE),
            _layer_spec((D_MODEL, d_ff), l),
            _layer_spec((D_MODEL, d_ff), l),
            _layer_spec((d_ff, D_MODEL), l),
            _vector_spec(D_MODEL, l, G_FFN_POST),
        ],
        out_specs=pl.BlockSpec((tm, D_MODEL), lambda i: (i, 0)),
        out_shape=jax.ShapeDtypeStruct((T, D_MODEL), F32),
        compiler_params=pltpu.CompilerParams(
            dimension_semantics=("arbitrary",), vmem_limit_bytes=V7X_VMEM_LIMIT_BYTES),
        name="ffn",
    )(x2, gpre, wg, wu, wd, gpost)


def kernel(x, mem, w_in, b_gate, conv_qk, w_pool, pool_scale, mlstm_norm_g, w_out, g_mix_pre, g_mix_post, g_mem, g_xattn_pre, g_xattn_post, wq_x, wk_x, wv_x, wo_x, g_ffn_pre, g_ffn_post, w_gate, w_up, w_down):
    B, S, D = x.shape
    depth = w_in.shape[0]
    H = MLSTM_HEADS
    x2 = x.reshape(B * S, D)
    pmat = _pool_matrices()
    rows = lambda v: v.reshape(depth, 1, -1).astype(F32)
    table = lambda vs: jnp.stack([v.astype(F32) for v in vs], axis=1)[:, :, None, :]
    gains = table([g_mix_pre, g_mix_post, g_mem, g_xattn_pre, g_xattn_post, g_ffn_pre, g_ffn_post])
    mixvecs = table([pool_scale, mlstm_norm_g])
    bf = lambda w: w.astype(BF16)
    w_in_b = bf(w_in)
    wgate = jnp.concatenate(
        [w_in_b[:, :, OFF_GATE + H:], w_in_b[:, :, OFF_GATE:OFF_GATE + H],
         jnp.zeros((depth, D, V7X_LANES - 2 * H), BF16)], axis=2)
    bias = jnp.concatenate(
        [b_gate[:, H:], b_gate[:, :H], jnp.zeros((depth, V7X_LANES - 2 * H), b_gate.dtype)], axis=1)
    mixer_params = (gains, w_in_b, wgate, rows(bias), conv_qk.astype(F32), pmat, bf(w_pool),
                    mixvecs, mixvecs, bf(w_out), gains)
    xattn_params = (gains, bf(wq_x), bf(wo_x), gains)
    ffn_params = (gains, bf(w_gate), bf(w_up), bf(w_down), gains)
    k, v = _memkv(mem, gains, wk_x, wv_x)

    for l in range(depth):
        x2 = _mixer(l, x2, *mixer_params, S)
        x2 = _xattn(l, x2, k, v, *xattn_params, B, S)
        x2 = _ffn(l, x2, *ffn_params)

    return x2.reshape(B, S, D)
```
